```python
import jax, jax.numpy as jnp
from jax import lax
import numpy as np

D_MODEL = 2048
BATCH = 1
SEQ = 8192
DEPTH = 1

PLE_DIM = 256
HG_HEADS = 8
HG_DK = 128
HG_DV = 128
HG_WIDTH = HG_HEADS * HG_DK
HG_CHUNK = 64
MLA_HEADS = 8
MLA_NOPE = 128
MLA_ROPE = 64
MLA_V = 128
MLA_Q_LORA = 512
MLA_KV_LORA = 512
MLA_QK = MLA_NOPE + MLA_ROPE
ROPE_THETA = 10000.0
ATTN_BLOCK = 128
PEER_HEADS = 8
PEER_NKEYS = 128
PEER_N = PEER_NKEYS * PEER_NKEYS
PEER_QDIM = 256
PEER_TOPK = 16
PEER_BLOCK = 128
IN_SIZES = (HG_WIDTH, HG_WIDTH, HG_WIDTH, HG_WIDTH,
            MLA_Q_LORA, MLA_KV_LORA, MLA_ROPE, D_MODEL, D_MODEL)
IN_WIDTH = 4 * HG_WIDTH + MLA_Q_LORA + MLA_KV_LORA + MLA_ROPE + 2 * D_MODEL
NORM_EPS = 1e-6
MASK_VALUE = -1e30

kernel_name = "hybrid_hgrn2_mla_peer_block"


def rms_norm(x, gain):
    xf = x.astype(jnp.float32)
    y = xf * lax.rsqrt(jnp.mean(xf * xf, axis=-1, keepdims=True) + NORM_EPS)
    return (y * gain.astype(jnp.float32)).astype(x.dtype)


def apply_rope(x, positions):
    half = x.shape[-1] // 2
    freqs = ROPE_THETA ** (-jnp.arange(half, dtype=jnp.float32) / half)
    ang = positions.astype(jnp.float32)[..., None] * freqs
    if x.ndim == 4:
        ang = ang[:, :, None, :]
    cos, sin = jnp.cos(ang), jnp.sin(ang)
    xf = x.astype(jnp.float32)
    x1, x2 = xf[..., :half], xf[..., half:]
    out = jnp.concatenate([x1 * cos - x2 * sin, x2 * cos + x1 * sin], axis=-1)
    return out.astype(x.dtype)


def hgrn2_chunk_scan(q, k, v, g):
    B, T, H, dk = q.shape
    dv = v.shape[-1]
    C = HG_CHUNK
    NC = T // C

    def to_chunks(a):
        return a.reshape(B, NC, C, H, a.shape[-1]).transpose(0, 3, 1, 2, 4)

    q, k, v, g = to_chunks(q), to_chunks(k), to_chunks(v), to_chunks(g)
    b = jnp.cumsum(g, axis=3)
    b_ref = b[:, :, :, C // 2 - 1:C // 2, :]
    q_in = q * jnp.exp(b - b_ref)
    k_in = k * jnp.exp(b_ref - b)
    A = jnp.einsum('bhncd,bhnsd->bhncs', q_in, k_in)
    causal = jnp.tril(jnp.ones((C, C), dtype=bool))
    A = jnp.where(causal, A, 0.0)
    o_intra = jnp.einsum('bhncs,bhnsv->bhncv', A, v)

    b_last = b[:, :, :, -1, :]
    k_dec = k * jnp.exp(b_last[:, :, :, None, :] - b)
    dS = jnp.einsum('bhncd,bhncv->bhndv', k_dec, v)
    decay = jnp.exp(b_last)

    def step(S, inp):
        dec_n, dS_n = inp
        return dec_n[..., None] * S + dS_n, S

    S0 = jnp.zeros((B, H, dk, dv), jnp.float32)
    _, S_prev = lax.scan(step, S0, (jnp.moveaxis(decay, 2, 0), jnp.moveaxis(dS, 2, 0)))
    S_prev = jnp.moveaxis(S_prev, 0, 2)
    o_inter = jnp.einsum('bhncd,bhndv->bhncv', q * jnp.exp(b), S_prev)
    o = o_intra + o_inter
    return o.transpose(0, 2, 3, 1, 4).reshape(B, T, H, dv)


def hgrn2_branch(hq, hf, hi, hog, lb, gain):
    B, T, _ = hq.shape
    f = lb + (1.0 - lb) * jax.nn.sigmoid(hf.astype(jnp.float32))
    g = jnp.log(f)
    k = 1.0 - f
    shp = (B, T, HG_HEADS, HG_DK)
    o = hgrn2_chunk_scan(hq.astype(jnp.float32).reshape(shp), k.reshape(shp),
                         hi.astype(jnp.float32).reshape(B, T, HG_HEADS, HG_DV), g.reshape(shp))
    o = o * lax.rsqrt(jnp.mean(o * o, axis=-1, keepdims=True) + NORM_EPS)
    o = o * gain.astype(jnp.float32).reshape(HG_HEADS, HG_DV)
    o = o.reshape(B, T, HG_HEADS * HG_DV) * jax.nn.silu(hog.astype(jnp.float32))
    return o.astype(hq.dtype)


def causal_block_attention(q, k, v):
    B, H, T, dq = q.shape
    dv = v.shape[-1]
    NB = T // ATTN_BLOCK
    scale = 1.0 / np.sqrt(dq).astype(np.float32)
    qb = q.reshape(B, H, NB, ATTN_BLOCK, dq).transpose(2, 0, 1, 3, 4)
    key_idx = jnp.arange(T)

    def one_block(args):
        q_blk, bi = args
        s = jnp.einsum('bhqd,bhkd->bhqk', q_blk, k,
                       preferred_element_type=jnp.float32) * scale
        q_idx = bi * ATTN_BLOCK + jnp.arange(ATTN_BLOCK)
        mask = key_idx[None, :] <= q_idx[:, None]
        s = jnp.where(mask, s, MASK_VALUE)
        pr = jax.nn.softmax(s, axis=-1).astype(v.dtype)
        return jnp.einsum('bhqk,bhkd->bhqd', pr, v)

    out = lax.map(one_block, (qb, jnp.arange(NB)))
    return out.transpose(1, 2, 0, 3, 4).reshape(B, H, T, dv)


def mla_branch(mq, mkv, mkr, positions, q_norm, kv_norm, w_uq, w_ukv):
    B, T, _ = mq.shape
    H = MLA_HEADS
    cq = rms_norm(mq, q_norm)
    q = (cq @ w_uq).reshape(B, T, H, MLA_QK)
    q_nope, q_rope = q[..., :MLA_NOPE], apply_rope(q[..., MLA_NOPE:], positions)
    ckv = rms_norm(mkv, kv_norm)
    kv = (ckv @ w_ukv).reshape(B, T, H, MLA_NOPE + MLA_V)
    k_nope, v = kv[..., :MLA_NOPE], kv[..., MLA_NOPE:]
    k_rope = apply_rope(mkr, positions)
    k_rope = jnp.broadcast_to(k_rope[:, :, None, :], (B, T, H, MLA_ROPE))
    qf = jnp.concatenate([q_nope, q_rope], axis=-1).transpose(0, 2, 1, 3)
    kf = jnp.concatenate([k_nope, k_rope], axis=-1).transpose(0, 2, 1, 3)
    o = causal_block_attention(qf, kf, v.transpose(0, 2, 1, 3))
    return o.transpose(0, 2, 1, 3).reshape(B, T, H * MLA_V)


def peer_ffn(h, wq, k1, k2, u_tab, v_tab):
    B, T, D = h.shape
    N = B * T
    K = PEER_TOPK
    half = PEER_QDIM // 2
    hf = h.reshape(N, D)
    q = (hf @ wq).reshape(N, PEER_HEADS, 2, half)
    s1 = jnp.einsum('nhc,kc->nhk', q[:, :, 0], k1)
    s2 = jnp.einsum('nhc,kc->nhk', q[:, :, 1], k2)
    v1, i1 = lax.top_k(s1, K)
    v2, i2 = lax.top_k(s2, K)
    cand = (v1[..., :, None] + v2[..., None, :]).reshape(N, PEER_HEADS, K * K)
    cv, ci = lax.top_k(cand, K)
    e_idx = (jnp.take_along_axis(i1, ci // K, axis=-1) * PEER_NKEYS
             + jnp.take_along_axis(i2, ci % K, axis=-1))
    gate = jax.nn.softmax(cv.astype(jnp.float32), axis=-1).astype(h.dtype)
    NB = N // PEER_BLOCK
    h_b = hf.reshape(NB, PEER_BLOCK, D)
    e_b = e_idx.reshape(NB, PEER_BLOCK, PEER_HEADS, K)
    g_b = gate.reshape(NB, PEER_BLOCK, PEER_HEADS, K)

    def one_block(args):
        hb, eb, gb = args
        z = jnp.einsum('thkd,td->thk', u_tab[eb], hb)
        a = jax.nn.gelu(z, approximate=False) * gb
        return jnp.einsum('thk,thkd->td', a, v_tab[eb])

    out = lax.map(one_block, (h_b, e_b, g_b))
    return out.reshape(B, T, D)


def setup_inputs(seed: int = 0) -> dict:
    key = jax.random.key(seed)
    ks = jax.random.split(key, 24)
    f32 = jnp.float32

    def nrm(k, shape, scale):
        return jax.random.normal(k, shape, f32) * scale

    def gain(k, shape):
        return 1.0 + 0.05 * jax.random.normal(k, shape, f32)

    L, D = DEPTH, D_MODEL
    return {
        "x": nrm(ks[0], (BATCH, SEQ, D), 1.0),
        "p": nrm(ks[1], (DEPTH, BATCH, SEQ, PLE_DIM), 1.0),
        "positions": jnp.broadcast_to(jnp.arange(SEQ, dtype=jnp.int32), (BATCH, SEQ)),
        "norm_mix": gain(ks[2], (L, D)),
        "w_in": nrm(ks[3], (L, D, IN_WIDTH), D ** -0.5),
        "lb_logits": nrm(ks[4], (DEPTH + 1, HG_WIDTH), 0.5),
        "hg_norm": gain(ks[5], (L, HG_WIDTH)),
        "mla_q_norm": gain(ks[6], (L, MLA_Q_LORA)),
        "mla_kv_norm": gain(ks[7], (L, MLA_KV_LORA)),
        "w_uq": nrm(ks[8], (L, MLA_Q_LORA, MLA_HEADS * MLA_QK), MLA_Q_LORA ** -0.5),
        "w_ukv": nrm(ks[9], (L, MLA_KV_LORA, MLA_HEADS * (MLA_NOPE + MLA_V)), MLA_KV_LORA ** -0.5),
        "w_a": nrm(ks[10], (L, HG_WIDTH, D), HG_WIDTH ** -0.5),
        "w_b": nrm(ks[11], (L, MLA_HEADS * MLA_V, D), (MLA_HEADS * MLA_V) ** -0.5),
        "w_o": nrm(ks[12], (L, D, D), D ** -0.5),
        "norm_ffn": gain(ks[13], (L, D)),
        "peer_wq": nrm(ks[14], (L, D, PEER_HEADS * PEER_QDIM), D ** -0.5),
        "peer_k1": nrm(ks[15], (L, PEER_NKEYS, PEER_QDIM // 2), (PEER_QDIM // 2) ** -0.5),
        "peer_k2": nrm(ks[16], (L, PEER_NKEYS, PEER_QDIM // 2), (PEER_QDIM // 2) ** -0.5),
        "peer_u": nrm(ks[17], (L, PEER_N, D), D ** -0.5),
        "peer_v": nrm(ks[18], (L, PEER_N, D), PEER_HEADS ** -0.5),
        "norm_ple": gain(ks[19], (L, D)),
        "w_pg": nrm(ks[20], (L, D, D), D ** -0.5),
        "w_pe": nrm(ks[21], (L, PLE_DIM, D), PLE_DIM ** -0.5),
        "norm_final": gain(ks[22], (D,)),
    }


def reference(x, p, positions, norm_mix, w_in, lb_logits, hg_norm, mla_q_norm, mla_kv_norm,
              w_uq, w_ukv, w_a, w_b, w_o, norm_ffn, peer_wq, peer_k1, peer_k2, peer_u, peer_v,
              norm_ple, w_pg, w_pe, norm_final):
    lb_all = jnp.cumsum(jax.nn.softmax(lb_logits.astype(jnp.float32), axis=0), axis=0)
    split_points = [int(s) for s in np.cumsum(IN_SIZES)[:-1]]
    for i in range(DEPTH):
        h = rms_norm(x, norm_mix[i])
        proj = h @ w_in[i]
        hq, hf, hi, hog, mq, mkv, mkr, ga, gb = jnp.split(proj, split_points, axis=-1)
        o_a = hgrn2_branch(hq, hf, hi, hog, lb_all[i], hg_norm[i])
        o_b = mla_branch(mq, mkv, mkr, positions, mla_q_norm[i], mla_kv_norm[i],
                         w_uq[i], w_ukv[i])
        y = jax.nn.sigmoid(ga) * (o_a @ w_a[i]) + jax.nn.sigmoid(gb) * (o_b @ w_b[i])
        x = x + y @ w_o[i]
        x = x + peer_ffn(rms_norm(x, norm_ffn[i]), peer_wq[i], peer_k1[i], peer_k2[i],
                         peer_u[i], peer_v[i])
        hp = rms_norm(x, norm_ple[i])
        x = x + jax.nn.sigmoid(hp @ w_pg[i]) * (p[i] @ w_pe[i])
    return rms_norm(x, norm_final)
```

```python
import functools

import jax
import jax.numpy as jnp
import numpy as np
from jax import lax
from jax.experimental import pallas as pl
from jax.experimental.pallas import tpu as pltpu

D_MODEL = 2048
HG_HEADS = 8
HG_DK = 128
HG_DV = 128
HG_WIDTH = HG_HEADS * HG_DK
HG_CHUNK = 64
MLA_HEADS = 8
MLA_NOPE = 128
MLA_ROPE = 64
MLA_V = 128
MLA_Q_LORA = 512
MLA_KV_LORA = 512
MLA_QK = MLA_NOPE + MLA_ROPE
ROPE_THETA = 10000.0
PEER_HEADS = 8
PEER_NKEYS = 128
PEER_QDIM = 256
PEER_TOPK = 16
NORM_EPS = 1e-6
MASK_VALUE = -1e30

LANES = 128
VMEM_LIMIT = 52 * 1024 * 1024

_NT = (((1,), (1,)), ((), ()))
_TN = (((0,), (0,)), ((), ()))

F32 = jnp.float32
BF16 = jnp.bfloat16


def _params(*sem):
    return pltpu.CompilerParams(dimension_semantics=sem, vmem_limit_bytes=VMEM_LIMIT)


def _rms(x, gain):
    return x * lax.rsqrt(jnp.mean(x * x, axis=-1, keepdims=True) + NORM_EPS) * gain


def _rmsnorm_kernel(x_ref, g_ref, o_ref):
    o_ref[...] = _rms(x_ref[...], g_ref[...]).astype(o_ref.dtype)


def _rmsnorm(x, gain, tm=512):
    n, d = x.shape
    return pl.pallas_call(
        _rmsnorm_kernel,
        grid=(n // tm,),
        in_specs=[pl.BlockSpec((tm, d), lambda i: (i, 0)), pl.BlockSpec((1, d), lambda i: (0, 0))],
        out_specs=pl.BlockSpec((tm, d), lambda i: (i, 0)),
        out_shape=jax.ShapeDtypeStruct((n, d), BF16),
        compiler_params=_params("parallel"),
        name="rmsnorm",
    )(x, gain.reshape(1, d))


def _matmul_kernel(a_ref, b_ref, o_ref):
    o_ref[...] = jnp.dot(a_ref[...], b_ref[...], preferred_element_type=F32).astype(o_ref.dtype)


def _matmul(a, b, out_dtype, tm, tn, name):
    m, k = a.shape
    _, n = b.shape
    return pl.pallas_call(
        _matmul_kernel,
        grid=(m // tm, n // tn),
        in_specs=[pl.BlockSpec((tm, k), lambda i, j: (i, 0)), pl.BlockSpec((k, tn), lambda i, j: (0, j))],
        out_specs=pl.BlockSpec((tm, tn), lambda i, j: (i, j)),
        out_shape=jax.ShapeDtypeStruct((m, n), out_dtype),
        compiler_params=_params("parallel", "arbitrary"),
        name=name,
    )(a, b)


def _hgrn_kernel(q_ref, f_ref, i_ref, og_ref, lbl_ref, gain_ref, o_ref, st_ref, *, layer, chunks):
    @pl.when(pl.program_id(1) == 0)
    def _():
        st_ref[...] = jnp.zeros_like(st_ref)

    lbl = lbl_ref[...]
    e = jnp.exp(lbl - jnp.max(lbl, axis=0, keepdims=True))
    sm = e / jnp.sum(e, axis=0, keepdims=True)
    lb = jnp.sum(sm[: layer + 1], axis=0, keepdims=True)
    gain = gain_ref[...]
    c = HG_CHUNK
    row = lax.broadcasted_iota(jnp.int32, (c, c), 0)
    col = lax.broadcasted_iota(jnp.int32, (c, c), 1)
    causal = row >= col
    tri = causal.astype(F32)
    for n in range(chunks):
        sl = pl.ds(n * c, c)
        q = q_ref[sl, :]
        v = i_ref[sl, :]
        og = og_ref[sl, :]
        f = lb + (1.0 - lb) * jax.nn.sigmoid(f_ref[sl, :])
        g = jnp.log(f)
        k = 1.0 - f
        b = jnp.dot(tri, g, preferred_element_type=F32, precision=lax.Precision.HIGHEST)
        b_mid = b[c // 2 - 1:c // 2, :]
        b_last = b[c - 1:c, :]
        q_in = (q * jnp.exp(b - b_mid)).astype(BF16)
        k_in = (k * jnp.exp(b_mid - b)).astype(BF16)
        a = lax.dot_general(q_in, k_in, _NT, preferred_element_type=F32)
        a = jnp.where(causal, a, 0.0).astype(BF16)
        vb = v.astype(BF16)
        o = jnp.dot(a, vb, preferred_element_type=F32)
        st = st_ref[...]
        qe = (q * jnp.exp(b)).astype(BF16)
        o = o + lax.dot_general(qe, st.astype(BF16), _NT, preferred_element_type=F32)
        k_dec = (k * jnp.exp(b_last - b)).astype(BF16)
        dst = lax.dot_general(vb, k_dec, _TN, preferred_element_type=F32)
        st_ref[...] = jnp.exp(b_last) * st + dst
        o = o * lax.rsqrt(jnp.mean(o * o, axis=-1, keepdims=True) + NORM_EPS) * gain
        o = o * (og * jax.nn.sigmoid(og))
        o_ref[sl, :] = o.astype(o_ref.dtype)


def _hgrn(proj, lb_logits, gain, layer, tt=512):
    n = proj.shape[0]
    h = HG_HEADS
    nl = lb_logits.shape[0]
    blk = lambda g: pl.BlockSpec((tt, HG_DK), lambda hh, t, g=g: (t, g * h + hh))
    return pl.pallas_call(
        functools.partial(_hgrn_kernel, layer=layer, chunks=tt // HG_CHUNK),
        grid=(h, n // tt),
        in_specs=[blk(0), blk(1), blk(2), blk(3),
                  pl.BlockSpec((nl, HG_DK), lambda hh, t: (0, hh)),
                  pl.BlockSpec((1, HG_DV), lambda hh, t: (0, hh))],
        out_specs=pl.BlockSpec((tt, HG_DV), lambda hh, t: (t, hh)),
        out_shape=jax.ShapeDtypeStruct((n, h * HG_DV), BF16),
        scratch_shapes=[pltpu.VMEM((HG_DV, HG_DK), F32)],
        compiler_params=_params("parallel", "arbitrary"),
        name="hgrn2",
    )(proj, proj, proj, proj, lb_logits, gain.reshape(1, -1))


def _mla_prep_kernel(p_ref, pos_ref, fr_ref, qn_ref_g, kvn_ref_g, wq_ref, wkv_ref,
                     qn_ref, qr_ref, kn_ref, v_ref, kr_ref, *, scale):
    hw = MLA_HEADS * MLA_NOPE
    ang = pos_ref[...].astype(F32) * fr_ref[...]
    lane = lax.broadcasted_iota(jnp.int32, ang.shape, 1)
    cosv = jnp.where(lane < MLA_ROPE, jnp.cos(ang), 0.0)
    sinv = jnp.where(lane < MLA_ROPE, jnp.sin(ang), 0.0)

    def rope(y):
        return y * cosv + pltpu.roll(y, MLA_ROPE, 1) * sinv

    cq = _rms(p_ref[:, :MLA_Q_LORA], qn_ref_g[...]).astype(BF16)
    qres = jnp.dot(cq, wq_ref[...], preferred_element_type=F32)
    qn_ref[...] = (qres[:, :hw] * scale).astype(qn_ref.dtype)
    for h in range(MLA_HEADS):
        y = qres[:, hw + h * LANES: hw + (h + 1) * LANES]
        qr_ref[:, h * LANES:(h + 1) * LANES] = (rope(y) * scale).astype(qr_ref.dtype)
    ckv = _rms(p_ref[:, MLA_Q_LORA:MLA_Q_LORA + MLA_KV_LORA], kvn_ref_g[...]).astype(BF16)
    kvres = jnp.dot(ckv, wkv_ref[...], preferred_element_type=F32)
    kn_ref[...] = kvres[:, :hw].astype(kn_ref.dtype)
    v_ref[...] = kvres[:, hw:].astype(v_ref.dtype)
    y = p_ref[:, MLA_Q_LORA + MLA_KV_LORA:]
    kr_ref[...] = rope(y).astype(kr_ref.dtype)


def _mla_prep(proj2, positions, q_norm, kv_norm, wq, wkv, tm=256):
    n, pw = proj2.shape
    hw = MLA_HEADS * MLA_NOPE
    half = MLA_ROPE // 2
    freqs = ROPE_THETA ** (-jnp.arange(half, dtype=F32) / half)
    fr = jnp.concatenate([freqs, freqs, jnp.zeros((LANES - MLA_ROPE,), F32)]).reshape(1, LANES)
    scale = float(1.0 / np.sqrt(MLA_QK).astype(np.float32))
    full = lambda shape: pl.BlockSpec(shape, lambda i: (0, 0))
    tok = lambda w: pl.BlockSpec((tm, w), lambda i: (i, 0))
    return pl.pallas_call(
        functools.partial(_mla_prep_kernel, scale=scale),
        grid=(n // tm,),
        in_specs=[tok(pw), tok(1), full((1, LANES)), full((1, MLA_Q_LORA)), full((1, MLA_KV_LORA)),
                  full(wq.shape), full(wkv.shape)],
        out_specs=[tok(hw), tok(MLA_HEADS * LANES), tok(hw), tok(MLA_HEADS * MLA_V), tok(LANES)],
        out_shape=[jax.ShapeDtypeStruct((n, hw), BF16),
                   jax.ShapeDtypeStruct((n, MLA_HEADS * LANES), BF16),
                   jax.ShapeDtypeStruct((n, hw), BF16),
                   jax.ShapeDtypeStruct((n, MLA_HEADS * MLA_V), BF16),
                   jax.ShapeDtypeStruct((n, LANES), BF16)],
        compiler_params=_params("parallel"),
        name="mla_prep",
    )(proj2, positions.reshape(n, 1), fr, q_norm.reshape(1, -1), kv_norm.reshape(1, -1), wq, wkv)


def _attn_kernel(qi_ref, kj_ref, qn_ref, qr_ref, kn_ref, kr_ref, v_ref, o_ref, m_ref, l_ref, acc_ref, *, blk):
    s_id = pl.program_id(1)
    qi = qi_ref[s_id]
    kj = kj_ref[s_id]

    @pl.when(kj == 0)
    def _():
        m_ref[...] = jnp.full_like(m_ref, -jnp.inf)
        l_ref[...] = jnp.zeros_like(l_ref)
        acc_ref[...] = jnp.zeros_like(acc_ref)

    q = jnp.concatenate([qn_ref[...], qr_ref[...]], axis=1)
    k = jnp.concatenate([kn_ref[...], kr_ref[...]], axis=1)
    s = lax.dot_general(q, k, _NT, preferred_element_type=F32)

    def update(s):
        m_prev = m_ref[...]
        m_new = jnp.maximum(m_prev, jnp.max(s, axis=1, keepdims=True))
        alpha = jnp.exp(m_prev - m_new)
        p = jnp.exp(s - m_new)
        l_ref[...] = alpha * l_ref[...] + jnp.sum(p, axis=1, keepdims=True)
        acc_ref[...] = alpha * acc_ref[...] + jnp.dot(p.astype(BF16), v_ref[...], preferred_element_type=F32)
        m_ref[...] = m_new

    @pl.when(kj < qi)
    def _():
        update(s)

    @pl.when(kj == qi)
    def _():
        row = lax.broadcasted_iota(jnp.int32, (blk, blk), 0)
        col = lax.broadcasted_iota(jnp.int32, (blk, blk), 1)
        update(jnp.where(col <= row, s, MASK_VALUE))
        o_ref[...] = (acc_ref[...] / l_ref[...]).astype(o_ref.dtype)


def _attention(qn, qr, kn, kr, v, blk=512):
    n = qn.shape[0]
    nb = n // blk
    qi = np.array([i for i in range(nb) for j in range(i + 1)], np.int32)
    kj = np.array([j for i in range(nb) for j in range(i + 1)], np.int32)
    qspec = pl.BlockSpec((blk, LANES), lambda h, s, qi, kj: (qi[s], h))
    kspec = pl.BlockSpec((blk, LANES), lambda h, s, qi, kj: (kj[s], h))
    krspec = pl.BlockSpec((blk, LANES), lambda h, s, qi, kj: (kj[s], 0))
    grid_spec = pltpu.PrefetchScalarGridSpec(
        num_scalar_prefetch=2,
        grid=(MLA_HEADS, len(qi)),
        in_specs=[qspec, qspec, kspec, krspec, kspec],
        out_specs=qspec,
        scratch_shapes=[pltpu.VMEM((blk, 1), F32), pltpu.VMEM((blk, 1), F32), pltpu.VMEM((blk, MLA_V), F32)],
    )
    return pl.pallas_call(
        functools.partial(_attn_kernel, blk=blk),
        grid_spec=grid_spec,
        out_shape=jax.ShapeDtypeStruct((n, MLA_HEADS * MLA_V), BF16),
        compiler_params=_params("parallel", "arbitrary"),
        name="mla_attention",
    )(jnp.asarray(qi), jnp.asarray(kj), qn, qr, kn, kr, v)


def _merge_kernel(oa_ref, ob_ref, ga_ref, gb_ref, x_ref, wa_ref, wb_ref, wo_ref, g_ref, x1_ref, hn_ref):
    ya = jnp.dot(oa_ref[...], wa_ref[...], preferred_element_type=F32)
    yb = jnp.dot(ob_ref[...], wb_ref[...], preferred_element_type=F32)
    y = jax.nn.sigmoid(ga_ref[...]) * ya + jax.nn.sigmoid(gb_ref[...]) * yb
    x1 = x_ref[...] + jnp.dot(y.astype(BF16), wo_ref[...], preferred_element_type=F32)
    x1_ref[...] = x1
    hn_ref[...] = _rms(x1, g_ref[...]).astype(hn_ref.dtype)


def _merge(oa, ob, proj, x, wa, wb, wo, gain, tm=256):
    n, d = x.shape
    tok = lambda w, c=0: pl.BlockSpec((tm, w), lambda i, c=c: (i, c))
    res = lambda shape: pl.BlockSpec(shape, lambda i: (0, 0), pipeline_mode=pl.Buffered(1))
    return pl.pallas_call(
        _merge_kernel,
        grid=(n // tm,),
        in_specs=[tok(oa.shape[1]), tok(ob.shape[1]), tok(d, 2), tok(d, 3), tok(d),
                  res(wa.shape), res(wb.shape), res(wo.shape), res((1, d))],
        out_specs=[tok(d), tok(d)],
        out_shape=[jax.ShapeDtypeStruct((n, d), F32), jax.ShapeDtypeStruct((n, d), BF16)],
        compiler_params=_params("parallel"),
        name="merge_outproj",
    )(oa, ob, proj, proj, x, wa, wb, wo, gain.reshape(1, d))


def _cand_tables():
    k = PEER_TOPK
    big = float(4 * k * k)
    ci = []
    pieces = [("b", 0, 16), ("b", 1, 8), ("b", 2, 8), ("b", 3, 8), ("a", 0, 16), ("a", 1, 8), ("a", 2, 8)]
    for kind, fixed, rows in pieces:
        for r in range(rows):
            a, b = (r, fixed) if kind == "b" else (fixed, r)
            ok = (a + 1) * (b + 1) <= k and (kind == "b" or b >= 4)
            ci.append(float(a * k + b) if ok else big)
    return pieces, np.asarray(ci, np.float32), big


def _top_rows(x, val_ref, idx_ref, k):
    rows = x.shape[0]
    iota = lax.broadcasted_iota(jnp.int32, x.shape, 0).astype(F32)
    for j in range(k):
        m = jnp.max(x, axis=0, keepdims=True)
        am = jnp.min(jnp.where(x == m, iota, float(rows)), axis=0, keepdims=True)
        val_ref[j:j + 1, :] = m
        idx_ref[j:j + 1, :] = am
        x = jnp.where(iota == am, -jnp.inf, x)


def _route_kernel(h_ref, wq_ref, k1_ref, k2_ref, ci_ref, e_ref, g_ref,
                  v1_ref, i1_ref, v2_ref, i2_ref, cv_ref, ce_ref, *, pieces, big):
    k = PEER_TOPK
    half = PEER_QDIM // 2
    qt = lax.dot_general(wq_ref[...], h_ref[...], _NT, preferred_element_type=F32)
    hp = lax.Precision.HIGHEST
    s1 = jnp.dot(k1_ref[...], qt[:half], preferred_element_type=F32, precision=hp)
    s2 = jnp.dot(k2_ref[...], qt[half:], preferred_element_type=F32, precision=hp)
    _top_rows(s1, v1_ref, i1_ref, k)
    _top_rows(s2, v2_ref, i2_ref, k)
    cand, ec = [], []
    for kind, fixed, rows in pieces:
        if kind == "b":
            cand.append(v1_ref[0:rows, :] + v2_ref[fixed:fixed + 1, :])
            ec.append(i1_ref[0:rows, :] * float(PEER_NKEYS) + i2_ref[fixed:fixed + 1, :])
        else:
            cand.append(v1_ref[fixed:fixed + 1, :] + v2_ref[0:rows, :])
            ec.append(i1_ref[fixed:fixed + 1, :] * float(PEER_NKEYS) + i2_ref[0:rows, :])
    ci = ci_ref[...]
    cand = jnp.where(ci < big, jnp.concatenate(cand, axis=0), -jnp.inf)
    ec = jnp.concatenate(ec, axis=0)
    for j in range(k):
        m = jnp.max(cand, axis=0, keepdims=True)
        sel = jnp.min(jnp.where(cand == m, ci, big), axis=0, keepdims=True)
        hit = ci == sel
        cv_ref[j:j + 1, :] = m
        ce_ref[j:j + 1, :] = jnp.max(jnp.where(hit, ec, -1.0), axis=0, keepdims=True)
        cand = jnp.where(hit, -jnp.inf, cand)
    cv = cv_ref[...]
    ex = jnp.exp(cv - cv[0:1, :])
    g_ref[...] = ex / jnp.sum(ex, axis=0, keepdims=True)
    e_ref[...] = ce_ref[...].astype(jnp.int32)


def _route(hn, wq_t, k1, k2, tm=256):
    n, d = hn.shape
    k = PEER_TOPK
    pieces, ci_col, big = _cand_tables()
    ci = jnp.asarray(np.broadcast_to(ci_col[:, None], (ci_col.shape[0], tm)).copy())
    rows = ci_col.shape[0]
    full = lambda shape: pl.BlockSpec(shape, lambda i, h: (0, 0))
    out = pl.BlockSpec((k, tm), lambda i, h: (h, i))
    sc = lambda: pltpu.VMEM((k, tm), F32)
    return pl.pallas_call(
        functools.partial(_route_kernel, pieces=pieces, big=big),
        grid=(n // tm, PEER_HEADS),
        in_specs=[pl.BlockSpec((tm, d), lambda i, h: (i, 0)),
                  pl.BlockSpec((PEER_QDIM, d), lambda i, h: (h, 0)),
                  full(k1.shape), full(k2.shape), full((rows, tm))],
        out_specs=[out, out],
        out_shape=[jax.ShapeDtypeStruct((PEER_HEADS * k, n), jnp.int32),
                   jax.ShapeDtypeStruct((PEER_HEADS * k, n), F32)],
        scratch_shapes=[sc(), sc(), sc(), sc(), sc(), sc()],
        compiler_params=_params("parallel", "arbitrary"),
        name="peer_route",
    )(hn, wq_t, k1, k2, ci)


def _wbuild_kernel(e_ref, g_ref, w_ref, *, tw):
    nk = PEER_NKEYS
    iota = lax.broadcasted_iota(jnp.int32, (nk, LANES), 0)

    def body(t, carry):
        e = e_ref[pl.ds(t, 1), :]
        g = g_ref[pl.ds(t, 1), :]
        p1 = jnp.where(iota == (e >> 7), g, 0.0).astype(BF16)
        p2 = jnp.where(iota == (e & (nk - 1)), 1.0, 0.0).astype(BF16)
        wt = lax.dot_general(p1, p2, _NT, preferred_element_type=F32)
        w_ref[:, pl.ds(t, 1), :] = wt.reshape(nk, 1, nk)
        return carry

    lax.fori_loop(0, tw, body, 0)


def _wbuild(e, g, tw=128):
    n, pairs = e.shape
    nk = PEER_NKEYS
    return pl.pallas_call(
        functools.partial(_wbuild_kernel, tw=tw),
        grid=(n // tw,),
        in_specs=[pl.BlockSpec((tw, pairs), lambda i: (i, 0)), pl.BlockSpec((tw, pairs), lambda i: (i, 0))],
        out_specs=pl.BlockSpec((nk, tw, nk), lambda i: (0, i, 0)),
        out_shape=jax.ShapeDtypeStruct((nk, n, nk), F32),
        compiler_params=_params("parallel"),
        name="peer_wbuild",
    )(e, g)


def _peer_kernel(h_ref, u_ref, v_ref, w_ref, o_ref, *, groups):
    @pl.when(pl.program_id(1) == 0)
    def _():
        o_ref[...] = jnp.zeros_like(o_ref)

    z = lax.dot_general(h_ref[...], u_ref[...], _NT, preferred_element_type=F32)
    cols = []
    for i in range(groups):
        zi = z[:, i * LANES:(i + 1) * LANES]
        gelu = 0.5 * zi * (1.0 + lax.erf(zi * np.float32(np.sqrt(0.5))))
        cols.append((w_ref[i] * gelu).astype(BF16))
    a = jnp.concatenate(cols, axis=1)
    o_ref[...] += jnp.dot(a, v_ref[...], preferred_element_type=F32)


def _peer(hn, u, v, w, tm=512, te=512):
    n, d = hn.shape
    ne = u.shape[0]
    groups = te // PEER_NKEYS
    return pl.pallas_call(
        functools.partial(_peer_kernel, groups=groups),
        grid=(n // tm, ne // te),
        in_specs=[pl.BlockSpec((tm, d), lambda i, c: (i, 0)),
                  pl.BlockSpec((te, d), lambda i, c: (c, 0)),
                  pl.BlockSpec((te, d), lambda i, c: (c, 0)),
                  pl.BlockSpec((groups, tm, PEER_NKEYS), lambda i, c: (c, i, 0))],
        out_specs=pl.BlockSpec((tm, d), lambda i, c: (i, 0)),
        out_shape=jax.ShapeDtypeStruct((n, d), F32),
        compiler_params=_params("parallel", "arbitrary"),
        name="peer_experts",
    )(hn, u, v, w)


def _ple_kernel(x1_ref, po_ref, p_ref, wpg_ref, wpe_ref, gp_ref, gf_ref, o_ref):
    x2 = x1_ref[...] + po_ref[...]
    hp = _rms(x2, gp_ref[...]).astype(BF16)
    gate = jax.nn.sigmoid(jnp.dot(hp, wpg_ref[...], preferred_element_type=F32))
    pe = jnp.dot(p_ref[...], wpe_ref[...], preferred_element_type=F32)
    x3 = x2 + gate * pe
    o_ref[...] = _rms(x3, gf_ref[...]).astype(o_ref.dtype)


def _ple(x1, po, p, wpg, wpe, g_ple, g_final, tm=256):
    n, d = x1.shape
    tok = lambda w: pl.BlockSpec((tm, w), lambda i: (i, 0))
    res = lambda shape: pl.BlockSpec(shape, lambda i: (0, 0), pipeline_mode=pl.Buffered(1))
    return pl.pallas_call(
        _ple_kernel,
        grid=(n // tm,),
        in_specs=[tok(d), tok(d), tok(p.shape[1]), res(wpg.shape), res(wpe.shape), res((1, d)), res((1, d))],
        out_specs=tok(d),
        out_shape=jax.ShapeDtypeStruct((n, d), F32),
        compiler_params=_params("parallel"),
        name="ple_final",
    )(x1, po, p, wpg, wpe, g_ple.reshape(1, d), g_final.reshape(1, d))


def _layer(x, p, positions, layer, norm_mix, w_in, lb_logits, hg_norm, mla_q_norm, mla_kv_norm,
           w_uq, w_ukv, w_a, w_b, w_o, norm_ffn, peer_wq, peer_k1, peer_k2, peer_u, peer_v,
           norm_ple, w_pg, w_pe, norm_out):
    n, d = x.shape
    hgw = 4 * HG_WIDTH
    mla0 = hgw
    kr0 = mla0 + MLA_Q_LORA + MLA_KV_LORA
    g0 = kr0 + MLA_ROPE
    half = MLA_ROPE // 2

    w_main = jnp.concatenate([w_in[:, :hgw], w_in[:, g0:]], axis=1).astype(BF16)
    w_kr = w_in[:, kr0:g0]
    w_kr_rot = jnp.concatenate([-w_kr[:, half:], w_kr[:, :half]], axis=1)
    w_mla = jnp.concatenate([w_in[:, mla0:kr0], w_kr, w_kr_rot], axis=1).astype(BF16)
    uq = w_uq.reshape(MLA_Q_LORA, MLA_HEADS, MLA_QK)
    uq_r = uq[:, :, MLA_NOPE:]
    uq_rot = jnp.concatenate([-uq_r[:, :, half:], uq_r[:, :, :half]], axis=2)
    wq = jnp.concatenate([uq[:, :, :MLA_NOPE].reshape(MLA_Q_LORA, -1),
                          jnp.concatenate([uq_r, uq_rot], axis=2).reshape(MLA_Q_LORA, -1)],
                         axis=1).astype(BF16)
    ukv = w_ukv.reshape(MLA_KV_LORA, MLA_HEADS, MLA_NOPE + MLA_V)
    wkv = jnp.concatenate([ukv[:, :, :MLA_NOPE].reshape(MLA_KV_LORA, -1),
                           ukv[:, :, MLA_NOPE:].reshape(MLA_KV_LORA, -1)], axis=1).astype(BF16)

    hn = _rmsnorm(x, norm_mix)
    proj = _matmul(hn, w_main, F32, 512, 1024, "in_proj_main")
    proj2 = _matmul(hn, w_mla, F32, 512, w_mla.shape[1], "in_proj_mla")

    o_a = _hgrn(proj, lb_logits, hg_norm, layer)
    qn, qr, kn, v, kr = _mla_prep(proj2, positions, mla_q_norm, mla_kv_norm, wq, wkv)
    o_b = _attention(qn, qr, kn, kr, v)
    x1, hn2 = _merge(o_a, o_b, proj, x, w_a.astype(BF16), w_b.astype(BF16), w_o.astype(BF16), norm_ffn)

    e_t, g_t = _route(hn2, peer_wq.T.astype(BF16), peer_k1, peer_k2)
    w = _wbuild(e_t.T, g_t.T)
    po = _peer(hn2, peer_u.astype(BF16), peer_v.astype(BF16), w)
    return _ple(x1, po, p.astype(BF16), w_pg.astype(BF16), w_pe.astype(BF16), norm_ple, norm_out)


def kernel(x, p, positions, norm_mix, w_in, lb_logits, hg_norm, mla_q_norm, mla_kv_norm, w_uq, w_ukv,
           w_a, w_b, w_o, norm_ffn, peer_wq, peer_k1, peer_k2, peer_u, peer_v, norm_ple, w_pg, w_pe,
           norm_final):
    b, t, d = x.shape
    depth = w_in.shape[0]
    assert b == 1 and depth == 1, "kernel supports one sequence and one layer"
    out = _layer(x.reshape(b * t, d), p[0].reshape(b * t, -1), positions.reshape(-1), 0,
                 norm_mix[0], w_in[0], lb_logits, hg_norm[0], mla_q_norm[0], mla_kv_norm[0],
                 w_uq[0], w_ukv[0], w_a[0], w_b[0], w_o[0], norm_ffn[0], peer_wq[0], peer_k1[0],
                 peer_k2[0], peer_u[0], peer_v[0], norm_ple[0], w_pg[0], w_pe[0], norm_final)
    return out.reshape(b, t, d)
```

```python
import functools

import jax
import jax.numpy as jnp
import numpy as np
from jax import lax
from jax.experimental import pallas as pl
from jax.experimental.pallas import tpu as pltpu

D_MODEL = 2048
HG_HEADS = 8
HG_DK = 128
HG_DV = 128
HG_WIDTH = HG_HEADS * HG_DK
HG_CHUNK = 64
MLA_HEADS = 8
MLA_NOPE = 128
MLA_ROPE = 64
MLA_V = 128
MLA_Q_LORA = 512
MLA_KV_LORA = 512
MLA_QK = MLA_NOPE + MLA_ROPE
ROPE_THETA = 10000.0
PEER_HEADS = 8
PEER_NKEYS = 128
PEER_QDIM = 256
PEER_TOPK = 16
NORM_EPS = 1e-6
MASK_VALUE = -1e30

LANES = 128
VMEM_LIMIT = 52 * 1024 * 1024

_NT = (((1,), (1,)), ((), ()))
_TN = (((0,), (0,)), ((), ()))

F32 = jnp.float32
BF16 = jnp.bfloat16


def _params(*sem):
    return pltpu.CompilerParams(dimension_semantics=sem, vmem_limit_bytes=VMEM_LIMIT)


def _rms(x, gain):
    return x * lax.rsqrt(jnp.mean(x * x, axis=-1, keepdims=True) + NORM_EPS) * gain


def _rmsnorm_kernel(x_ref, g_ref, o_ref):
    o_ref[...] = _rms(x_ref[...], g_ref[...]).astype(o_ref.dtype)


def _rmsnorm(x, gain, tm=512):
    n, d = x.shape
    return pl.pallas_call(
        _rmsnorm_kernel,
        grid=(n // tm,),
        in_specs=[pl.BlockSpec((tm, d), lambda i: (i, 0)), pl.BlockSpec((1, d), lambda i: (0, 0))],
        out_specs=pl.BlockSpec((tm, d), lambda i: (i, 0)),
        out_shape=jax.ShapeDtypeStruct((n, d), BF16),
        compiler_params=_params("parallel"),
        name="rmsnorm",
    )(x, gain.reshape(1, d))


def _matmul_kernel(a_ref, b_ref, o_ref):
    o_ref[...] = jnp.dot(a_ref[...], b_ref[...], preferred_element_type=F32).astype(o_ref.dtype)


def _matmul(a, b, out_dtype, tm, tn, name):
    m, k = a.shape
    _, n = b.shape
    return pl.pallas_call(
        _matmul_kernel,
        grid=(m // tm, n // tn),
        in_specs=[pl.BlockSpec((tm, k), lambda i, j: (i, 0)), pl.BlockSpec((k, tn), lambda i, j: (0, j))],
        out_specs=pl.BlockSpec((tm, tn), lambda i, j: (i, j)),
        out_shape=jax.ShapeDtypeStruct((m, n), out_dtype),
        compiler_params=_params("parallel", "arbitrary"),
        name=name,
    )(a, b)


def _hgrn_kernel(q_ref, f_ref, i_ref, og_ref, lbl_ref, gain_ref, o_ref, st_ref, *, layer, chunks):
    @pl.when(pl.program_id(1) == 0)
    def _():
        st_ref[...] = jnp.zeros_like(st_ref)

    lbl = lbl_ref[...]
    e = jnp.exp(lbl - jnp.max(lbl, axis=0, keepdims=True))
    sm = e / jnp.sum(e, axis=0, keepdims=True)
    lb = jnp.sum(sm[: layer + 1], axis=0, keepdims=True)
    gain = gain_ref[...]
    c = HG_CHUNK
    row = lax.broadcasted_iota(jnp.int32, (c, c), 0)
    col = lax.broadcasted_iota(jnp.int32, (c, c), 1)
    causal = row >= col
    tri = causal.astype(F32)
    for n in range(chunks):
        sl = pl.ds(n * c, c)
        q = q_ref[sl, :]
        v = i_ref[sl, :]
        og = og_ref[sl, :]
        f = lb + (1.0 - lb) * jax.nn.sigmoid(f_ref[sl, :])
        g = jnp.log(f)
        k = 1.0 - f
        b = jnp.dot(tri, g, preferred_element_type=F32, precision=lax.Precision.HIGHEST)
        b_mid = b[c // 2 - 1:c // 2, :]
        b_last = b[c - 1:c, :]
        q_in = (q * jnp.exp(b - b_mid)).astype(BF16)
        k_in = (k * jnp.exp(b_mid - b)).astype(BF16)
        a = lax.dot_general(q_in, k_in, _NT, preferred_element_type=F32)
        a = jnp.where(causal, a, 0.0).astype(BF16)
        vb = v.astype(BF16)
        o = jnp.dot(a, vb, preferred_element_type=F32)
        st = st_ref[...]
        qe = (q * jnp.exp(b)).astype(BF16)
        o = o + lax.dot_general(qe, st.astype(BF16), _NT, preferred_element_type=F32)
        k_dec = (k * jnp.exp(b_last - b)).astype(BF16)
        dst = lax.dot_general(vb, k_dec, _TN, preferred_element_type=F32)
        st_ref[...] = jnp.exp(b_last) * st + dst
        o = o * lax.rsqrt(jnp.mean(o * o, axis=-1, keepdims=True) + NORM_EPS) * gain
        o = o * (og * jax.nn.sigmoid(og))
        o_ref[sl, :] = o.astype(o_ref.dtype)


def _hgrn(proj, lb_logits, gain, layer, tt=512):
    n = proj.shape[0]
    h = HG_HEADS
    nl = lb_logits.shape[0]
    blk = lambda g: pl.BlockSpec((tt, HG_DK), lambda hh, t, g=g: (t, g * h + hh))
    return pl.pallas_call(
        functools.partial(_hgrn_kernel, layer=layer, chunks=tt // HG_CHUNK),
        grid=(h, n // tt),
        in_specs=[blk(0), blk(1), blk(2), blk(3),
                  pl.BlockSpec((nl, HG_DK), lambda hh, t: (0, hh)),
                  pl.BlockSpec((1, HG_DV), lambda hh, t: (0, hh))],
        out_specs=pl.BlockSpec((tt, HG_DV), lambda hh, t: (t, hh)),
        out_shape=jax.ShapeDtypeStruct((n, h * HG_DV), BF16),
        scratch_shapes=[pltpu.VMEM((HG_DV, HG_DK), F32)],
        compiler_params=_params("parallel", "arbitrary"),
        name="hgrn2",
    )(proj, proj, proj, proj, lb_logits, gain.reshape(1, -1))


def _mla_prep_kernel(p_ref, pos_ref, fr_ref, qn_ref_g, kvn_ref_g, wq_ref, wk_ref, wvt_ref,
                     qn_ref, qr_ref, kn_ref, vt_ref, kr_ref, *, scale):
    hw = MLA_HEADS * MLA_NOPE
    ang = pos_ref[...].astype(F32) * fr_ref[...]
    lane = lax.broadcasted_iota(jnp.int32, ang.shape, 1)
    cosv = jnp.where(lane < MLA_ROPE, jnp.cos(ang), 0.0)
    sinv = jnp.where(lane < MLA_ROPE, jnp.sin(ang), 0.0)

    def rope(y):
        return y * cosv + pltpu.roll(y, MLA_ROPE, 1) * sinv

    cq = _rms(p_ref[:, :MLA_Q_LORA], qn_ref_g[...]).astype(BF16)
    qres = jnp.dot(cq, wq_ref[...], preferred_element_type=F32)
    qn_ref[...] = (qres[:, :hw] * scale).astype(qn_ref.dtype)
    for h in range(MLA_HEADS):
        y = qres[:, hw + h * LANES: hw + (h + 1) * LANES]
        qr_ref[:, h * LANES:(h + 1) * LANES] = (rope(y) * scale).astype(qr_ref.dtype)
    ckv = _rms(p_ref[:, MLA_Q_LORA:MLA_Q_LORA + MLA_KV_LORA], kvn_ref_g[...]).astype(BF16)
    kn_ref[...] = jnp.dot(ckv, wk_ref[...], preferred_element_type=F32).astype(kn_ref.dtype)
    vt_ref[...] = lax.dot_general(wvt_ref[...], ckv, _NT, preferred_element_type=F32).astype(vt_ref.dtype)
    y = p_ref[:, MLA_Q_LORA + MLA_KV_LORA:]
    kr_ref[...] = rope(y).astype(kr_ref.dtype)


def _mla_prep(proj2, positions, q_norm, kv_norm, wq, wk, wvt, tm=256):
    n, pw = proj2.shape
    hw = MLA_HEADS * MLA_NOPE
    half = MLA_ROPE // 2
    freqs = ROPE_THETA ** (-jnp.arange(half, dtype=F32) / half)
    fr = jnp.concatenate([freqs, freqs, jnp.zeros((LANES - MLA_ROPE,), F32)]).reshape(1, LANES)
    scale = float(1.0 / np.sqrt(MLA_QK).astype(np.float32)) * float(np.log2(np.e))
    full = lambda shape: pl.BlockSpec(shape, lambda i: (0, 0))
    tok = lambda w: pl.BlockSpec((tm, w), lambda i: (i, 0))
    return pl.pallas_call(
        functools.partial(_mla_prep_kernel, scale=scale),
        grid=(n // tm,),
        in_specs=[tok(pw), tok(1), full((1, LANES)), full((1, MLA_Q_LORA)), full((1, MLA_KV_LORA)),
                  full(wq.shape), full(wk.shape), full(wvt.shape)],
        out_specs=[tok(hw), tok(MLA_HEADS * LANES), tok(hw),
                   pl.BlockSpec((MLA_HEADS * MLA_V, tm), lambda i: (0, i)), tok(LANES)],
        out_shape=[jax.ShapeDtypeStruct((n, hw), BF16),
                   jax.ShapeDtypeStruct((n, MLA_HEADS * LANES), BF16),
                   jax.ShapeDtypeStruct((n, hw), BF16),
                   jax.ShapeDtypeStruct((MLA_HEADS * MLA_V, n), BF16),
                   jax.ShapeDtypeStruct((n, LANES), BF16)],
        compiler_params=_params("parallel"),
        name="mla_prep",
    )(proj2, positions.reshape(n, 1), fr, q_norm.reshape(1, -1), kv_norm.reshape(1, -1), wq, wk, wvt)


QSUB = 256


def _attn_kernel(qi_ref, kj_ref, qn_ref, qr_ref, kn_ref, kr_ref, vt_ref, o_ref, m_ref, l_ref, acc_ref, *, blk):
    s_id = pl.program_id(1)
    qi = qi_ref[s_id]
    kj = kj_ref[s_id]

    @pl.when(kj == 0)
    def _():
        m_ref[...] = jnp.full_like(m_ref, -jnp.inf)
        l_ref[...] = jnp.zeros_like(l_ref)
        acc_ref[...] = jnp.zeros_like(acc_ref)

    def update(masked):
        k = jnp.concatenate([kn_ref[...], kr_ref[...]], axis=1)
        vt = vt_ref[...]
        sts = []
        for c in range(blk // QSUB):
            sl = pl.ds(c * QSUB, QSUB)
            q = jnp.concatenate([qn_ref[sl, :], qr_ref[sl, :]], axis=1)
            sts.append(lax.dot_general(k, q, _NT, preferred_element_type=F32))
        for c in range(blk // QSUB):
            sl = pl.ds(c * QSUB, QSUB)
            st = sts[c]
            if masked:
                key = lax.broadcasted_iota(jnp.int32, (blk, QSUB), 0)
                qry = lax.broadcasted_iota(jnp.int32, (blk, QSUB), 1) + c * QSUB
                st = jnp.where(key <= qry, st, MASK_VALUE)
            m_prev = m_ref[:, sl]
            m_new = jnp.maximum(m_prev, jnp.max(st, axis=0, keepdims=True))
            alpha = jnp.exp2(m_prev - m_new)
            p = jnp.exp2(st - m_new)
            l_ref[:, sl] = alpha * l_ref[:, sl] + jnp.sum(p, axis=0, keepdims=True)
            acc_ref[:, sl] = alpha * acc_ref[:, sl] + jnp.dot(vt, p.astype(BF16), preferred_element_type=F32)
            m_ref[:, sl] = m_new

    @pl.when(kj < qi)
    def _():
        update(False)

    @pl.when(kj == qi)
    def _():
        update(True)
        o_ref[...] = (acc_ref[...] / l_ref[...]).T.astype(o_ref.dtype)


def _attention(qn, qr, kn, kr, vt, blk=512):
    n = qn.shape[0]
    nb = n // blk
    qi = np.array([i for i in range(nb) for j in range(i + 1)], np.int32)
    kj = np.array([j for i in range(nb) for j in range(i + 1)], np.int32)
    qspec = pl.BlockSpec((blk, LANES), lambda h, s, qi, kj: (qi[s], h))
    kspec = pl.BlockSpec((blk, LANES), lambda h, s, qi, kj: (kj[s], h))
    krspec = pl.BlockSpec((blk, LANES), lambda h, s, qi, kj: (kj[s], 0))
    vtspec = pl.BlockSpec((MLA_V, blk), lambda h, s, qi, kj: (h, kj[s]))
    grid_spec = pltpu.PrefetchScalarGridSpec(
        num_scalar_prefetch=2,
        grid=(MLA_HEADS, len(qi)),
        in_specs=[qspec, qspec, kspec, krspec, vtspec],
        out_specs=qspec,
        scratch_shapes=[pltpu.VMEM((1, blk), F32), pltpu.VMEM((1, blk), F32), pltpu.VMEM((MLA_V, blk), F32)],
    )
    return pl.pallas_call(
        functools.partial(_attn_kernel, blk=blk),
        grid_spec=grid_spec,
        out_shape=jax.ShapeDtypeStruct((n, MLA_HEADS * MLA_V), BF16),
        compiler_params=_params("parallel", "arbitrary"),
        name="mla_attention",
    )(jnp.asarray(qi), jnp.asarray(kj), qn, qr, kn, kr, vt)


def _merge_kernel(oa_ref, ob_ref, ga_ref, gb_ref, x_ref, wa_ref, wb_ref, wo_ref, g_ref, x1_ref, hn_ref):
    ya = jnp.dot(oa_ref[...], wa_ref[...], preferred_element_type=F32)
    yb = jnp.dot(ob_ref[...], wb_ref[...], preferred_element_type=F32)
    y = jax.nn.sigmoid(ga_ref[...]) * ya + jax.nn.sigmoid(gb_ref[...]) * yb
    x1 = x_ref[...] + jnp.dot(y.astype(BF16), wo_ref[...], preferred_element_type=F32)
    x1_ref[...] = x1
    hn_ref[...] = _rms(x1, g_ref[...]).astype(hn_ref.dtype)


def _merge(oa, ob, proj, x, wa, wb, wo, gain, tm=256):
    n, d = x.shape
    tok = lambda w, c=0: pl.BlockSpec((tm, w), lambda i, c=c: (i, c))
    res = lambda shape: pl.BlockSpec(shape, lambda i: (0, 0), pipeline_mode=pl.Buffered(1))
    return pl.pallas_call(
        _merge_kernel,
        grid=(n // tm,),
        in_specs=[tok(oa.shape[1]), tok(ob.shape[1]), tok(d, 2), tok(d, 3), tok(d),
                  res(wa.shape), res(wb.shape), res(wo.shape), res((1, d))],
        out_specs=[tok(d), tok(d)],
        out_shape=[jax.ShapeDtypeStruct((n, d), F32), jax.ShapeDtypeStruct((n, d), BF16)],
        compiler_params=_params("parallel"),
        name="merge_outproj",
    )(oa, ob, proj, proj, x, wa, wb, wo, gain.reshape(1, d))


def _cand_tables():
    k = PEER_TOPK
    big = float(4 * k * k)
    ci = []
    pieces = [("b", 0, 16), ("b", 1, 8), ("b", 2, 8), ("b", 3, 8), ("a", 0, 16), ("a", 1, 8), ("a", 2, 8)]
    for kind, fixed, rows in pieces:
        for r in range(rows):
            a, b = (r, fixed) if kind == "b" else (fixed, r)
            ok = (a + 1) * (b + 1) <= k and (kind == "b" or b >= 4)
            ci.append(float(a * k + b) if ok else big)
    return pieces, np.asarray(ci, np.float32), big


def _top_rows(x, val_ref, idx_ref, k):
    rows = x.shape[0]
    iota = lax.broadcasted_iota(jnp.int32, x.shape, 0).astype(F32)
    for j in range(k):
        m = jnp.max(x, axis=0, keepdims=True)
        am = jnp.min(jnp.where(x == m, iota, float(rows)), axis=0, keepdims=True)
        val_ref[j:j + 1, :] = m
        idx_ref[j:j + 1, :] = am
        x = jnp.where(iota == am, -jnp.inf, x)


def _route_kernel(h_ref, wq_ref, k1_ref, k2_ref, ci_ref, e_ref, g_ref,
                  v1_ref, i1_ref, v2_ref, i2_ref, cv_ref, ce_ref, *, pieces, big):
    k = PEER_TOPK
    half = PEER_QDIM // 2
    qt = lax.dot_general(wq_ref[...], h_ref[...], _NT, preferred_element_type=F32)
    hp = lax.Precision.HIGHEST
    s1 = jnp.dot(k1_ref[...], qt[:half], preferred_element_type=F32, precision=hp)
    s2 = jnp.dot(k2_ref[...], qt[half:], preferred_element_type=F32, precision=hp)
    _top_rows(s1, v1_ref, i1_ref, k)
    _top_rows(s2, v2_ref, i2_ref, k)
    cand, ec = [], []
    for kind, fixed, rows in pieces:
        if kind == "b":
            cand.append(v1_ref[0:rows, :] + v2_ref[fixed:fixed + 1, :])
            ec.append(i1_ref[0:rows, :] * float(PEER_NKEYS) + i2_ref[fixed:fixed + 1, :])
        else:
            cand.append(v1_ref[fixed:fixed + 1, :] + v2_ref[0:rows, :])
            ec.append(i1_ref[fixed:fixed + 1, :] * float(PEER_NKEYS) + i2_ref[0:rows, :])
    ci = ci_ref[...]
    cand = jnp.where(ci < big, jnp.concatenate(cand, axis=0), -jnp.inf)
    ec = jnp.concatenate(ec, axis=0)
    for j in range(k):
        m = jnp.max(cand, axis=0, keepdims=True)
        sel = jnp.min(jnp.where(cand == m, ci, big), axis=0, keepdims=True)
        hit = ci == sel
        cv_ref[j:j + 1, :] = m
        ce_ref[j:j + 1, :] = jnp.max(jnp.where(hit, ec, -1.0), axis=0, keepdims=True)
        cand = jnp.where(hit, -jnp.inf, cand)
    cv = cv_ref[...]
    ex = jnp.exp(cv - cv[0:1, :])
    g_ref[...] = ex / jnp.sum(ex, axis=0, keepdims=True)
    e_ref[...] = ce_ref[...].astype(jnp.int32)


def _route(hn, wq_t, k1, k2, tm=256):
    n, d = hn.shape
    k = PEER_TOPK
    pieces, ci_col, big = _cand_tables()
    ci = jnp.asarray(np.broadcast_to(ci_col[:, None], (ci_col.shape[0], tm)).copy())
    rows = ci_col.shape[0]
    full = lambda shape: pl.BlockSpec(shape, lambda i, h: (0, 0))
    out = pl.BlockSpec((k, tm), lambda i, h: (h, i))
    sc = lambda: pltpu.VMEM((k, tm), F32)
    return pl.pallas_call(
        functools.partial(_route_kernel, pieces=pieces, big=big),
        grid=(n // tm, PEER_HEADS),
        in_specs=[pl.BlockSpec((tm, d), lambda i, h: (i, 0)),
                  pl.BlockSpec((PEER_QDIM, d), lambda i, h: (h, 0)),
                  full(k1.shape), full(k2.shape), full((rows, tm))],
        out_specs=[out, out],
        out_shape=[jax.ShapeDtypeStruct((PEER_HEADS * k, n), jnp.int32),
                   jax.ShapeDtypeStruct((PEER_HEADS * k, n), F32)],
        scratch_shapes=[sc(), sc(), sc(), sc(), sc(), sc()],
        compiler_params=_params("parallel", "arbitrary"),
        name="peer_route",
    )(hn, wq_t, k1, k2, ci)


W_TILE = 128
W_PITCH = W_TILE + 8


def _wbuild_kernel(e_ref, g_ref, w_ref):
    nk = PEER_NKEYS
    iota = lax.broadcasted_iota(jnp.int32, (nk, LANES), 0)
    for i in range(nk):
        w_ref[i * W_PITCH + W_TILE:(i + 1) * W_PITCH, :] = jnp.zeros((W_PITCH - W_TILE, nk), F32)

    def body(t, carry):
        e = e_ref[pl.ds(t, 1), :]
        g = g_ref[pl.ds(t, 1), :]
        p1 = jnp.where(iota == (e >> 7), g, 0.0).astype(BF16)
        p2 = jnp.where(iota == (e & (nk - 1)), 1.0, 0.0).astype(BF16)
        wt = lax.dot_general(p1, p2, _NT, preferred_element_type=F32)
        w_ref[pl.ds(t, nk, stride=W_PITCH), :] = wt
        return carry

    lax.fori_loop(0, W_TILE, body, 0, unroll=8)


def _wbuild(e, g):
    n, pairs = e.shape
    nk = PEER_NKEYS
    tiles = n // W_TILE
    w = pl.pallas_call(
        _wbuild_kernel,
        grid=(tiles,),
        in_specs=[pl.BlockSpec((W_TILE, pairs), lambda i: (i, 0)), pl.BlockSpec((W_TILE, pairs), lambda i: (i, 0))],
        out_specs=pl.BlockSpec((nk * W_PITCH, nk), lambda i: (i, 0)),
        out_shape=jax.ShapeDtypeStruct((tiles * nk * W_PITCH, nk), F32),
        compiler_params=_params("parallel"),
        name="peer_wbuild",
    )(e, g)
    return w.reshape(tiles, nk, W_PITCH, nk)


def _peer_kernel(h_ref, u_ref, v_ref, w_ref, o_ref, *, groups, tiles):
    @pl.when(pl.program_id(1) == 0)
    def _():
        o_ref[...] = jnp.zeros_like(o_ref)

    z = lax.dot_general(h_ref[...], u_ref[...], _NT, preferred_element_type=F32)
    rows = []
    for k in range(tiles):
        cols = []
        for i in range(groups):
            zi = z[k * W_TILE:(k + 1) * W_TILE, i * LANES:(i + 1) * LANES]
            gelu = 0.5 * zi * (1.0 + lax.erf(zi * np.float32(np.sqrt(0.5))))
            cols.append((w_ref[k, i, 0:W_TILE, :] * gelu).astype(BF16))
        rows.append(jnp.concatenate(cols, axis=1))
    a = jnp.concatenate(rows, axis=0)
    o_ref[...] += jnp.dot(a, v_ref[...], preferred_element_type=F32)


def _peer(hn, u, v, w, tm=1024, te=512):
    n, d = hn.shape
    ne = u.shape[0]
    groups = te // PEER_NKEYS
    tiles = tm // W_TILE
    return pl.pallas_call(
        functools.partial(_peer_kernel, groups=groups, tiles=tiles),
        grid=(n // tm, ne // te),
        in_specs=[pl.BlockSpec((tm, d), lambda i, c: (i, 0)),
                  pl.BlockSpec((te, d), lambda i, c: (c, 0)),
                  pl.BlockSpec((te, d), lambda i, c: (c, 0)),
                  pl.BlockSpec((tiles, groups, W_PITCH, PEER_NKEYS), lambda i, c: (i, c, 0, 0))],
        out_specs=pl.BlockSpec((tm, d), lambda i, c: (i, 0)),
        out_shape=jax.ShapeDtypeStruct((n, d), F32),
        compiler_params=_params("parallel", "arbitrary"),
        name="peer_experts",
    )(hn, u, v, w)


def _ple_kernel(x1_ref, po_ref, p_ref, wpg_ref, wpe_ref, gp_ref, gf_ref, o_ref):
    x2 = x1_ref[...] + po_ref[...]
    hp = _rms(x2, gp_ref[...]).astype(BF16)
    gate = jax.nn.sigmoid(jnp.dot(hp, wpg_ref[...], preferred_element_type=F32))
    pe = jnp.dot(p_ref[...], wpe_ref[...], preferred_element_type=F32)
    x3 = x2 + gate * pe
    o_ref[...] = _rms(x3, gf_ref[...]).astype(o_ref.dtype)


def _ple(x1, po, p, wpg, wpe, g_ple, g_final, tm=256):
    n, d = x1.shape
    tok = lambda w: pl.BlockSpec((tm, w), lambda i: (i, 0))
    res = lambda shape: pl.BlockSpec(shape, lambda i: (0, 0), pipeline_mode=pl.Buffered(1))
    return pl.pallas_call(
        _ple_kernel,
        grid=(n // tm,),
        in_specs=[tok(d), tok(d), tok(p.shape[1]), res(wpg.shape), res(wpe.shape), res((1, d)), res((1, d))],
        out_specs=tok(d),
        out_shape=jax.ShapeDtypeStruct((n, d), F32),
        compiler_params=_params("parallel"),
        name="ple_final",
    )(x1, po, p, wpg, wpe, g_ple.reshape(1, d), g_final.reshape(1, d))


def _layer(x, p, positions, layer, norm_mix, w_in, lb_logits, hg_norm, mla_q_norm, mla_kv_norm,
           w_uq, w_ukv, w_a, w_b, w_o, norm_ffn, peer_wq, peer_k1, peer_k2, peer_u, peer_v,
           norm_ple, w_pg, w_pe, norm_out):
    n, d = x.shape
    hgw = 4 * HG_WIDTH
    mla0 = hgw
    kr0 = mla0 + MLA_Q_LORA + MLA_KV_LORA
    g0 = kr0 + MLA_ROPE
    half = MLA_ROPE // 2

    w_main = jnp.concatenate([w_in[:, :hgw], w_in[:, g0:]], axis=1).astype(BF16)
    w_kr = w_in[:, kr0:g0]
    w_kr_rot = jnp.concatenate([-w_kr[:, half:], w_kr[:, :half]], axis=1)
    w_mla = jnp.concatenate([w_in[:, mla0:kr0], w_kr, w_kr_rot], axis=1).astype(BF16)
    uq = w_uq.reshape(MLA_Q_LORA, MLA_HEADS, MLA_QK)
    uq_r = uq[:, :, MLA_NOPE:]
    uq_rot = jnp.concatenate([-uq_r[:, :, half:], uq_r[:, :, :half]], axis=2)
    wq = jnp.concatenate([uq[:, :, :MLA_NOPE].reshape(MLA_Q_LORA, -1),
                          jnp.concatenate([uq_r, uq_rot], axis=2).reshape(MLA_Q_LORA, -1)],
                         axis=1).astype(BF16)
    ukv = w_ukv.reshape(MLA_KV_LORA, MLA_HEADS, MLA_NOPE + MLA_V)
    wk = ukv[:, :, :MLA_NOPE].reshape(MLA_KV_LORA, -1).astype(BF16)
    wvt = ukv[:, :, MLA_NOPE:].reshape(MLA_KV_LORA, -1).T.astype(BF16)

    hn = _rmsnorm(x, norm_mix)
    proj = _matmul(hn, w_main, F32, 512, 1024, "in_proj_main")
    proj2 = _matmul(hn, w_mla, F32, 512, w_mla.shape[1], "in_proj_mla")

    o_a = _hgrn(proj, lb_logits, hg_norm, layer)
    qn, qr, kn, vt, kr = _mla_prep(proj2, positions, mla_q_norm, mla_kv_norm, wq, wk, wvt)
    o_b = _attention(qn, qr, kn, kr, vt)
    x1, hn2 = _merge(o_a, o_b, proj, x, w_a.astype(BF16), w_b.astype(BF16), w_o.astype(BF16), norm_ffn)

    e_t, g_t = _route(hn2, peer_wq.T.astype(BF16), peer_k1, peer_k2)
    w = _wbuild(e_t.T, g_t.T)
    po = _peer(hn2, peer_u.astype(BF16), peer_v.astype(BF16), w)
    return _ple(x1, po, p.astype(BF16), w_pg.astype(BF16), w_pe.astype(BF16), norm_ple, norm_out)


def kernel(x, p, positions, norm_mix, w_in, lb_logits, hg_norm, mla_q_norm, mla_kv_norm, w_uq, w_ukv,
           w_a, w_b, w_o, norm_ffn, peer_wq, peer_k1, peer_k2, peer_u, peer_v, norm_ple, w_pg, w_pe,
           norm_final):
    b, t, d = x.shape
    depth = w_in.shape[0]
    assert b == 1 and depth == 1, "kernel supports one sequence and one layer"
    out = _layer(x.reshape(b * t, d), p[0].reshape(b * t, -1), positions.reshape(-1), 0,
                 norm_mix[0], w_in[0], lb_logits, hg_norm[0], mla_q_norm[0], mla_kv_norm[0],
                 w_uq[0], w_ukv[0], w_a[0], w_b[0], w_o[0], norm_ffn[0], peer_wq[0], peer_k1[0],
                 peer_k2[0], peer_u[0], peer_v[0], norm_ple[0], w_pg[0], w_pe[0], norm_final)
    return out.reshape(b, t, d)
```

```python
import functools

import jax
import jax.numpy as jnp
import numpy as np
from jax import lax
from jax.experimental import pallas as pl
from jax.experimental.pallas import tpu as pltpu

D_MODEL = 2048
HG_HEADS = 8
HG_DK = 128
HG_DV = 128
HG_WIDTH = HG_HEADS * HG_DK
HG_CHUNK = 64
MLA_HEADS = 8
MLA_NOPE = 128
MLA_ROPE = 64
MLA_V = 128
MLA_Q_LORA = 512
MLA_KV_LORA = 512
MLA_QK = MLA_NOPE + MLA_ROPE
ROPE_THETA = 10000.0
PEER_HEADS = 8
PEER_NKEYS = 128
PEER_QDIM = 256
PEER_TOPK = 16
NORM_EPS = 1e-6
MASK_VALUE = -1e30

LANES = 128
VMEM_LIMIT = 52 * 1024 * 1024

_NT = (((1,), (1,)), ((), ()))
_TN = (((0,), (0,)), ((), ()))

F32 = jnp.float32
BF16 = jnp.bfloat16


def _params(*sem):
    return pltpu.CompilerParams(dimension_semantics=sem, vmem_limit_bytes=VMEM_LIMIT)


def _rms(x, gain):
    return x * lax.rsqrt(jnp.mean(x * x, axis=-1, keepdims=True) + NORM_EPS) * gain


def _rmsnorm_kernel(x_ref, g_ref, o_ref):
    o_ref[...] = _rms(x_ref[...], g_ref[...]).astype(o_ref.dtype)


def _rmsnorm(x, gain, tm=512):
    n, d = x.shape
    return pl.pallas_call(
        _rmsnorm_kernel,
        grid=(n // tm,),
        in_specs=[pl.BlockSpec((tm, d), lambda i: (i, 0)), pl.BlockSpec((1, d), lambda i: (0, 0))],
        out_specs=pl.BlockSpec((tm, d), lambda i: (i, 0)),
        out_shape=jax.ShapeDtypeStruct((n, d), BF16),
        compiler_params=_params("parallel"),
        name="rmsnorm",
    )(x, gain.reshape(1, d))


def _matmul_kernel(a_ref, b_ref, o_ref):
    o_ref[...] = jnp.dot(a_ref[...], b_ref[...], preferred_element_type=F32).astype(o_ref.dtype)


def _matmul(a, b, out_dtype, tm, tn, name):
    m, k = a.shape
    _, n = b.shape
    return pl.pallas_call(
        _matmul_kernel,
        grid=(m // tm, n // tn),
        in_specs=[pl.BlockSpec((tm, k), lambda i, j: (i, 0)), pl.BlockSpec((k, tn), lambda i, j: (0, j))],
        out_specs=pl.BlockSpec((tm, tn), lambda i, j: (i, j)),
        out_shape=jax.ShapeDtypeStruct((m, n), out_dtype),
        compiler_params=_params("parallel", "arbitrary"),
        name=name,
    )(a, b)


def _hgrn_kernel(q_ref, f_ref, i_ref, og_ref, lbl_ref, gain_ref, o_ref, st_ref, *, layer, chunks):
    @pl.when(pl.program_id(1) == 0)
    def _():
        st_ref[...] = jnp.zeros_like(st_ref)

    lbl = lbl_ref[...]
    e = jnp.exp(lbl - jnp.max(lbl, axis=0, keepdims=True))
    sm = e / jnp.sum(e, axis=0, keepdims=True)
    lb = jnp.sum(sm[: layer + 1], axis=0, keepdims=True)
    gain = gain_ref[...]
    c = HG_CHUNK
    row = lax.broadcasted_iota(jnp.int32, (c, c), 0)
    col = lax.broadcasted_iota(jnp.int32, (c, c), 1)
    causal = row >= col
    tri = causal.astype(F32)
    for n in range(chunks):
        sl = pl.ds(n * c, c)
        q = q_ref[sl, :]
        v = i_ref[sl, :]
        og = og_ref[sl, :]
        f = lb + (1.0 - lb) * jax.nn.sigmoid(f_ref[sl, :])
        g = jnp.log(f)
        k = 1.0 - f
        b = jnp.dot(tri, g, preferred_element_type=F32, precision=lax.Precision.HIGHEST)
        b_mid = b[c // 2 - 1:c // 2, :]
        b_last = b[c - 1:c, :]
        q_in = (q * jnp.exp(b - b_mid)).astype(BF16)
        k_in = (k * jnp.exp(b_mid - b)).astype(BF16)
        a = lax.dot_general(q_in, k_in, _NT, preferred_element_type=F32)
        a = jnp.where(causal, a, 0.0).astype(BF16)
        vb = v.astype(BF16)
        o = jnp.dot(a, vb, preferred_element_type=F32)
        st = st_ref[...]
        qe = (q * jnp.exp(b)).astype(BF16)
        o = o + lax.dot_general(qe, st.astype(BF16), _NT, preferred_element_type=F32)
        k_dec = (k * jnp.exp(b_last - b)).astype(BF16)
        dst = lax.dot_general(vb, k_dec, _TN, preferred_element_type=F32)
        st_ref[...] = jnp.exp(b_last) * st + dst
        o = o * lax.rsqrt(jnp.mean(o * o, axis=-1, keepdims=True) + NORM_EPS) * gain
        o = o * (og * jax.nn.sigmoid(og))
        o_ref[sl, :] = o.astype(o_ref.dtype)


def _hgrn(proj, lb_logits, gain, layer, tt=512):
    n = proj.shape[0]
    h = HG_HEADS
    nl = lb_logits.shape[0]
    blk = lambda g: pl.BlockSpec((tt, HG_DK), lambda hh, t, g=g: (t, g * h + hh))
    return pl.pallas_call(
        functools.partial(_hgrn_kernel, layer=layer, chunks=tt // HG_CHUNK),
        grid=(h, n // tt),
        in_specs=[blk(0), blk(1), blk(2), blk(3),
                  pl.BlockSpec((nl, HG_DK), lambda hh, t: (0, hh)),
                  pl.BlockSpec((1, HG_DV), lambda hh, t: (0, hh))],
        out_specs=pl.BlockSpec((tt, HG_DV), lambda hh, t: (t, hh)),
        out_shape=jax.ShapeDtypeStruct((n, h * HG_DV), BF16),
        scratch_shapes=[pltpu.VMEM((HG_DV, HG_DK), F32)],
        compiler_params=_params("parallel", "arbitrary"),
        name="hgrn2",
    )(proj, proj, proj, proj, lb_logits, gain.reshape(1, -1))


def _mla_prep_kernel(p_ref, pos_ref, fr_ref, qn_ref_g, kvn_ref_g, wq_ref, wk_ref, wvt_ref,
                     qn_ref, qr_ref, kn_ref, vt_ref, kr_ref, *, scale):
    hw = MLA_HEADS * MLA_NOPE
    ang = pos_ref[...].astype(F32) * fr_ref[...]
    lane = lax.broadcasted_iota(jnp.int32, ang.shape, 1)
    cosv = jnp.where(lane < MLA_ROPE, jnp.cos(ang), 0.0)
    sinv = jnp.where(lane < MLA_ROPE, jnp.sin(ang), 0.0)

    def rope(y):
        return y * cosv + pltpu.roll(y, MLA_ROPE, 1) * sinv

    cq = _rms(p_ref[:, :MLA_Q_LORA], qn_ref_g[...]).astype(BF16)
    qres = jnp.dot(cq, wq_ref[...], preferred_element_type=F32)
    qn_ref[...] = (qres[:, :hw] * scale).astype(qn_ref.dtype)
    for h in range(MLA_HEADS):
        y = qres[:, hw + h * LANES: hw + (h + 1) * LANES]
        qr_ref[:, h * LANES:(h + 1) * LANES] = (rope(y) * scale).astype(qr_ref.dtype)
    ckv = _rms(p_ref[:, MLA_Q_LORA:MLA_Q_LORA + MLA_KV_LORA], kvn_ref_g[...]).astype(BF16)
    kn_ref[...] = jnp.dot(ckv, wk_ref[...], preferred_element_type=F32).astype(kn_ref.dtype)
    vt_ref[...] = lax.dot_general(wvt_ref[...], ckv, _NT, preferred_element_type=F32).astype(vt_ref.dtype)
    y = p_ref[:, MLA_Q_LORA + MLA_KV_LORA:]
    kr_ref[...] = rope(y).astype(kr_ref.dtype)


def _mla_prep(proj2, positions, q_norm, kv_norm, wq, wk, wvt, tm=256):
    n, pw = proj2.shape
    hw = MLA_HEADS * MLA_NOPE
    half = MLA_ROPE // 2
    freqs = ROPE_THETA ** (-jnp.arange(half, dtype=F32) / half)
    fr = jnp.concatenate([freqs, freqs, jnp.zeros((LANES - MLA_ROPE,), F32)]).reshape(1, LANES)
    scale = float(1.0 / np.sqrt(MLA_QK).astype(np.float32)) * float(np.log2(np.e))
    full = lambda shape: pl.BlockSpec(shape, lambda i: (0, 0))
    tok = lambda w: pl.BlockSpec((tm, w), lambda i: (i, 0))
    return pl.pallas_call(
        functools.partial(_mla_prep_kernel, scale=scale),
        grid=(n // tm,),
        in_specs=[tok(pw), tok(1), full((1, LANES)), full((1, MLA_Q_LORA)), full((1, MLA_KV_LORA)),
                  full(wq.shape), full(wk.shape), full(wvt.shape)],
        out_specs=[tok(hw), tok(MLA_HEADS * LANES), tok(hw),
                   pl.BlockSpec((MLA_HEADS * MLA_V, tm), lambda i: (0, i)), tok(LANES)],
        out_shape=[jax.ShapeDtypeStruct((n, hw), BF16),
                   jax.ShapeDtypeStruct((n, MLA_HEADS * LANES), BF16),
                   jax.ShapeDtypeStruct((n, hw), BF16),
                   jax.ShapeDtypeStruct((MLA_HEADS * MLA_V, n), BF16),
                   jax.ShapeDtypeStruct((n, LANES), BF16)],
        compiler_params=_params("parallel"),
        name="mla_prep",
    )(proj2, positions.reshape(n, 1), fr, q_norm.reshape(1, -1), kv_norm.reshape(1, -1), wq, wk, wvt)


QSUB = 256


def _attn_kernel(qi_ref, kj_ref, qn_ref, qr_ref, kn_ref, kr_ref, vt_ref, o_ref, m_ref, l_ref, acc_ref, *, blk):
    s_id = pl.program_id(1)
    qi = qi_ref[s_id]
    kj = kj_ref[s_id]

    @pl.when(kj == 0)
    def _():
        m_ref[...] = jnp.full_like(m_ref, -jnp.inf)
        l_ref[...] = jnp.zeros_like(l_ref)
        acc_ref[...] = jnp.zeros_like(acc_ref)

    def update(masked):
        nsub = blk // QSUB

        def nkeys(c):
            return (c + 1) * QSUB if masked else blk

        def scores(c):
            sl = pl.ds(c * QSUB, QSUB)
            k = jnp.concatenate([kn_ref[0:nkeys(c), :], kr_ref[0:nkeys(c), :]], axis=1)
            q = jnp.concatenate([qn_ref[sl, :], qr_ref[sl, :]], axis=1)
            return lax.dot_general(k, q, _NT, preferred_element_type=F32)

        sts = [scores(c) for c in range(min(2, nsub))]
        for c in range(nsub):
            sl = pl.ds(c * QSUB, QSUB)
            st = sts[c]
            if c + 2 < nsub:
                sts.append(scores(c + 2))
            if masked:
                key = lax.broadcasted_iota(jnp.int32, st.shape, 0)
                qry = lax.broadcasted_iota(jnp.int32, st.shape, 1) + c * QSUB
                st = jnp.where(key <= qry, st, MASK_VALUE)
            m_prev = m_ref[:, sl]
            m_new = jnp.maximum(m_prev, jnp.max(st, axis=0, keepdims=True))
            alpha = jnp.exp2(m_prev - m_new)
            p = jnp.exp2(st - m_new)
            l_ref[:, sl] = alpha * l_ref[:, sl] + jnp.sum(p, axis=0, keepdims=True)
            pv = jnp.dot(vt_ref[:, 0:nkeys(c)], p.astype(BF16), preferred_element_type=F32)
            acc_ref[:, sl] = alpha * acc_ref[:, sl] + pv
            m_ref[:, sl] = m_new

    @pl.when(kj < qi)
    def _():
        update(False)

    @pl.when(kj == qi)
    def _():
        update(True)
        o_ref[...] = (acc_ref[...] / l_ref[...]).T.astype(o_ref.dtype)


def _attention(qn, qr, kn, kr, vt, blk=1024):
    n = qn.shape[0]
    nb = n // blk
    qi = np.array([i for i in range(nb) for j in range(i + 1)], np.int32)
    kj = np.array([j for i in range(nb) for j in range(i + 1)], np.int32)
    qspec = pl.BlockSpec((blk, LANES), lambda h, s, qi, kj: (qi[s], h))
    kspec = pl.BlockSpec((blk, LANES), lambda h, s, qi, kj: (kj[s], h))
    krspec = pl.BlockSpec((blk, LANES), lambda h, s, qi, kj: (kj[s], 0))
    vtspec = pl.BlockSpec((MLA_V, blk), lambda h, s, qi, kj: (h, kj[s]))
    grid_spec = pltpu.PrefetchScalarGridSpec(
        num_scalar_prefetch=2,
        grid=(MLA_HEADS, len(qi)),
        in_specs=[qspec, qspec, kspec, krspec, vtspec],
        out_specs=qspec,
        scratch_shapes=[pltpu.VMEM((1, blk), F32), pltpu.VMEM((1, blk), F32), pltpu.VMEM((MLA_V, blk), F32)],
    )
    return pl.pallas_call(
        functools.partial(_attn_kernel, blk=blk),
        grid_spec=grid_spec,
        out_shape=jax.ShapeDtypeStruct((n, MLA_HEADS * MLA_V), BF16),
        compiler_params=_params("parallel", "arbitrary"),
        name="mla_attention",
    )(jnp.asarray(qi), jnp.asarray(kj), qn, qr, kn, kr, vt)


def _merge_kernel(oa_ref, ob_ref, ga_ref, gb_ref, x_ref, wa_ref, wb_ref, wo_ref, g_ref, x1_ref, hn_ref):
    ya = jnp.dot(oa_ref[...], wa_ref[...], preferred_element_type=F32)
    yb = jnp.dot(ob_ref[...], wb_ref[...], preferred_element_type=F32)
    y = jax.nn.sigmoid(ga_ref[...]) * ya + jax.nn.sigmoid(gb_ref[...]) * yb
    x1 = x_ref[...] + jnp.dot(y.astype(BF16), wo_ref[...], preferred_element_type=F32)
    x1_ref[...] = x1
    hn_ref[...] = _rms(x1, g_ref[...]).astype(hn_ref.dtype)


def _merge(oa, ob, proj, x, wa, wb, wo, gain, tm=256):
    n, d = x.shape
    tok = lambda w, c=0: pl.BlockSpec((tm, w), lambda i, c=c: (i, c))
    res = lambda shape: pl.BlockSpec(shape, lambda i: (0, 0), pipeline_mode=pl.Buffered(1))
    return pl.pallas_call(
        _merge_kernel,
        grid=(n // tm,),
        in_specs=[tok(oa.shape[1]), tok(ob.shape[1]), tok(d, 2), tok(d, 3), tok(d),
                  res(wa.shape), res(wb.shape), res(wo.shape), res((1, d))],
        out_specs=[tok(d), tok(d)],
        out_shape=[jax.ShapeDtypeStruct((n, d), F32), jax.ShapeDtypeStruct((n, d), BF16)],
        compiler_params=_params("parallel"),
        name="merge_outproj",
    )(oa, ob, proj, proj, x, wa, wb, wo, gain.reshape(1, d))


def _cand_tables():
    k = PEER_TOPK
    big = float(4 * k * k)
    ci = []
    pieces = [("b", 0, 16), ("b", 1, 8), ("b", 2, 8), ("b", 3, 8), ("a", 0, 16), ("a", 1, 8), ("a", 2, 8)]
    for kind, fixed, rows in pieces:
        for r in range(rows):
            a, b = (r, fixed) if kind == "b" else (fixed, r)
            ok = (a + 1) * (b + 1) <= k and (kind == "b" or b >= 4)
            ci.append(float(a * k + b) if ok else big)
    return pieces, np.asarray(ci, np.float32), big


def _top_rows(x, val_ref, idx_ref, lanes, k):
    rows = x.shape[0]
    iota = lax.broadcasted_iota(jnp.int32, x.shape, 0).astype(F32)
    for j in range(k):
        m = jnp.max(x, axis=0, keepdims=True)
        am = jnp.min(jnp.where(x == m, iota, float(rows)), axis=0, keepdims=True)
        val_ref[j:j + 1, lanes] = m
        idx_ref[j:j + 1, lanes] = am
        x = jnp.where(iota == am, -jnp.inf, x)


def _split_bf16(x):
    hi = x.astype(BF16)
    return hi, (x - hi.astype(F32)).astype(BF16)


def _route_kernel(h_ref, wq_ref, k1_ref, k2_ref, ci_ref, e_ref, g_ref,
                  v1_ref, i1_ref, v2_ref, i2_ref, cv_ref, ce_ref, *, pieces, big, hps):
    k = PEER_TOPK
    half = PEER_QDIM // 2
    tm = h_ref.shape[0]
    h = h_ref[...]
    ci = ci_ref[...]

    def scores(hh):
        qt = lax.dot_general(wq_ref[hh * PEER_QDIM:(hh + 1) * PEER_QDIM, :], h, _NT, preferred_element_type=F32)
        out = []
        for part, k_ref in ((0, k1_ref), (1, k2_ref)):
            q_hi, q_lo = _split_bf16(qt[part * half:(part + 1) * half])
            qs = jnp.concatenate([q_hi, q_lo, q_hi], axis=0)
            out.append(jnp.dot(k_ref[...], qs, preferred_element_type=F32))
        return out

    nxt = scores(0)
    for hh in range(hps):
        s1, s2 = nxt
        if hh + 1 < hps:
            nxt = scores(hh + 1)
        for c in range(tm // LANES):
            lanes = pl.ds(c * LANES, LANES)
            _top_rows(s1[:, c * LANES:(c + 1) * LANES], v1_ref, i1_ref, lanes, k)
            _top_rows(s2[:, c * LANES:(c + 1) * LANES], v2_ref, i2_ref, lanes, k)
            cand, ec = [], []
            for kind, fixed, rows in pieces:
                if kind == "b":
                    cand.append(v1_ref[0:rows, lanes] + v2_ref[fixed:fixed + 1, lanes])
                    ec.append(i1_ref[0:rows, lanes] * float(PEER_NKEYS) + i2_ref[fixed:fixed + 1, lanes])
                else:
                    cand.append(v1_ref[fixed:fixed + 1, lanes] + v2_ref[0:rows, lanes])
                    ec.append(i1_ref[fixed:fixed + 1, lanes] * float(PEER_NKEYS) + i2_ref[0:rows, lanes])
            cand = jnp.where(ci < big, jnp.concatenate(cand, axis=0), -jnp.inf)
            ec = jnp.concatenate(ec, axis=0)
            for j in range(k):
                m = jnp.max(cand, axis=0, keepdims=True)
                sel = jnp.min(jnp.where(cand == m, ci, big), axis=0, keepdims=True)
                hit = ci == sel
                cv_ref[j:j + 1, lanes] = m
                ce_ref[j:j + 1, lanes] = jnp.max(jnp.where(hit, ec, -1.0), axis=0, keepdims=True)
                cand = jnp.where(hit, -jnp.inf, cand)
        cv = cv_ref[...]
        ex = jnp.exp(cv - cv[0:1, :])
        g_ref[hh * k:(hh + 1) * k, :] = ex / jnp.sum(ex, axis=0, keepdims=True)
        e_ref[hh * k:(hh + 1) * k, :] = ce_ref[...].astype(jnp.int32)


def _route(hn, wq_t, k1, k2, tm=256, hps=2):
    n, d = hn.shape
    k = PEER_TOPK
    pieces, ci_col, big = _cand_tables()
    rows = ci_col.shape[0]
    ci = jnp.asarray(np.broadcast_to(ci_col[:, None], (rows, LANES)).copy())

    def stack(kk):
        hi, lo = _split_bf16(kk)
        return jnp.concatenate([hi, hi, lo], axis=1)

    full = lambda shape: pl.BlockSpec(shape, lambda i, g: (0, 0))
    out = pl.BlockSpec((hps * k, tm), lambda i, g: (g, i))
    sc = lambda: pltpu.VMEM((k, tm), F32)
    return pl.pallas_call(
        functools.partial(_route_kernel, pieces=pieces, big=big, hps=hps),
        grid=(n // tm, PEER_HEADS // hps),
        in_specs=[pl.BlockSpec((tm, d), lambda i, g: (i, 0)),
                  pl.BlockSpec((hps * PEER_QDIM, d), lambda i, g: (g, 0)),
                  full((PEER_NKEYS, 3 * (PEER_QDIM // 2))), full((PEER_NKEYS, 3 * (PEER_QDIM // 2))),
                  full((rows, LANES))],
        out_specs=[out, out],
        out_shape=[jax.ShapeDtypeStruct((PEER_HEADS * k, n), jnp.int32),
                   jax.ShapeDtypeStruct((PEER_HEADS * k, n), F32)],
        scratch_shapes=[sc(), sc(), sc(), sc(), sc(), sc()],
        compiler_params=_params("parallel", "arbitrary"),
        name="peer_route",
    )(hn, wq_t, stack(k1), stack(k2), ci)


W_TILE = 128
W_PITCH = W_TILE + 8


def _wbuild_kernel(e_ref, g_ref, w_ref):
    nk = PEER_NKEYS
    iota = lax.broadcasted_iota(jnp.int32, (nk, LANES), 0)
    for i in range(nk):
        w_ref[i * W_PITCH + W_TILE:(i + 1) * W_PITCH, :] = jnp.zeros((W_PITCH - W_TILE, nk), F32)

    def body(t, carry):
        e = e_ref[pl.ds(t, 1), :]
        g = g_ref[pl.ds(t, 1), :]
        p1 = jnp.where(iota == (e >> 7), g, 0.0).astype(BF16)
        p2 = jnp.where(iota == (e & (nk - 1)), 1.0, 0.0).astype(BF16)
        wt = lax.dot_general(p1, p2, _NT, preferred_element_type=F32)
        w_ref[pl.ds(t, nk, stride=W_PITCH), :] = wt
        return carry

    lax.fori_loop(0, W_TILE, body, 0, unroll=8)


def _wbuild(e, g):
    n, pairs = e.shape
    nk = PEER_NKEYS
    tiles = n // W_TILE
    w = pl.pallas_call(
        _wbuild_kernel,
        grid=(tiles,),
        in_specs=[pl.BlockSpec((W_TILE, pairs), lambda i: (i, 0)), pl.BlockSpec((W_TILE, pairs), lambda i: (i, 0))],
        out_specs=pl.BlockSpec((nk * W_PITCH, nk), lambda i: (i, 0)),
        out_shape=jax.ShapeDtypeStruct((tiles * nk * W_PITCH, nk), F32),
        compiler_params=_params("parallel"),
        name="peer_wbuild",
    )(e, g)
    return w.reshape(tiles, nk, W_PITCH, nk)


def _peer_kernel(h_ref, u_ref, v_ref, w_ref, o_ref, *, groups, tiles):
    @pl.when(pl.program_id(1) == 0)
    def _():
        o_ref[...] = jnp.zeros_like(o_ref)

    z = lax.dot_general(h_ref[...], u_ref[...], _NT, preferred_element_type=F32)
    rows = []
    for k in range(tiles):
        cols = []
        for i in range(groups):
            zi = z[k * W_TILE:(k + 1) * W_TILE, i * LANES:(i + 1) * LANES]
            gelu = 0.5 * zi * (1.0 + lax.erf(zi * np.float32(np.sqrt(0.5))))
            cols.append((w_ref[k, i, 0:W_TILE, :] * gelu).astype(BF16))
        rows.append(jnp.concatenate(cols, axis=1))
    a = jnp.concatenate(rows, axis=0)
    o_ref[...] += jnp.dot(a, v_ref[...], preferred_element_type=F32)


def _peer(hn, u, v, w, tm=1024, te=512):
    n, d = hn.shape
    ne = u.shape[0]
    groups = te // PEER_NKEYS
    tiles = tm // W_TILE
    return pl.pallas_call(
        functools.partial(_peer_kernel, groups=groups, tiles=tiles),
        grid=(n // tm, ne // te),
        in_specs=[pl.BlockSpec((tm, d), lambda i, c: (i, 0)),
                  pl.BlockSpec((te, d), lambda i, c: (c, 0)),
                  pl.BlockSpec((te, d), lambda i, c: (c, 0)),
                  pl.BlockSpec((tiles, groups, W_PITCH, PEER_NKEYS), lambda i, c: (i, c, 0, 0))],
        out_specs=pl.BlockSpec((tm, d), lambda i, c: (i, 0)),
        out_shape=jax.ShapeDtypeStruct((n, d), F32),
        compiler_params=_params("parallel", "arbitrary"),
        name="peer_experts",
    )(hn, u, v, w)


def _ple_kernel(x1_ref, po_ref, p_ref, wpg_ref, wpe_ref, gp_ref, gf_ref, o_ref):
    x2 = x1_ref[...] + po_ref[...]
    hp = _rms(x2, gp_ref[...]).astype(BF16)
    gate = jax.nn.sigmoid(jnp.dot(hp, wpg_ref[...], preferred_element_type=F32))
    pe = jnp.dot(p_ref[...], wpe_ref[...], preferred_element_type=F32)
    x3 = x2 + gate * pe
    o_ref[...] = _rms(x3, gf_ref[...]).astype(o_ref.dtype)


def _ple(x1, po, p, wpg, wpe, g_ple, g_final, tm=256):
    n, d = x1.shape
    tok = lambda w: pl.BlockSpec((tm, w), lambda i: (i, 0))
    res = lambda shape: pl.BlockSpec(shape, lambda i: (0, 0), pipeline_mode=pl.Buffered(1))
    return pl.pallas_call(
        _ple_kernel,
        grid=(n // tm,),
        in_specs=[tok(d), tok(d), tok(p.shape[1]), res(wpg.shape), res(wpe.shape), res((1, d)), res((1, d))],
        out_specs=tok(d),
        out_shape=jax.ShapeDtypeStruct((n, d), F32),
        compiler_params=_params("parallel"),
        name="ple_final",
    )(x1, po, p, wpg, wpe, g_ple.reshape(1, d), g_final.reshape(1, d))


def _layer(x, p, positions, layer, norm_mix, w_in, lb_logits, hg_norm, mla_q_norm, mla_kv_norm,
           w_uq, w_ukv, w_a, w_b, w_o, norm_ffn, peer_wq, peer_k1, peer_k2, peer_u, peer_v,
           norm_ple, w_pg, w_pe, norm_out):
    n, d = x.shape
    hgw = 4 * HG_WIDTH
    mla0 = hgw
    kr0 = mla0 + MLA_Q_LORA + MLA_KV_LORA
    g0 = kr0 + MLA_ROPE
    half = MLA_ROPE // 2

    w_main = jnp.concatenate([w_in[:, :hgw], w_in[:, g0:]], axis=1).astype(BF16)
    w_kr = w_in[:, kr0:g0]
    w_kr_rot = jnp.concatenate([-w_kr[:, half:], w_kr[:, :half]], axis=1)
    w_mla = jnp.concatenate([w_in[:, mla0:kr0], w_kr, w_kr_rot], axis=1).astype(BF16)
    uq = w_uq.reshape(MLA_Q_LORA, MLA_HEADS, MLA_QK)
    uq_r = uq[:, :, MLA_NOPE:]
    uq_rot = jnp.concatenate([-uq_r[:, :, half:], uq_r[:, :, :half]], axis=2)
    wq = jnp.concatenate([uq[:, :, :MLA_NOPE].reshape(MLA_Q_LORA, -1),
                          jnp.concatenate([uq_r, uq_rot], axis=2).reshape(MLA_Q_LORA, -1)],
                         axis=1).astype(BF16)
    ukv = w_ukv.reshape(MLA_KV_LORA, MLA_HEADS, MLA_NOPE + MLA_V)
    wk = ukv[:, :, :MLA_NOPE].reshape(MLA_KV_LORA, -1).astype(BF16)
    wvt = ukv[:, :, MLA_NOPE:].reshape(MLA_KV_LORA, -1).T.astype(BF16)

    hn = _rmsnorm(x, norm_mix)
    proj = _matmul(hn, w_main, F32, 512, 1024, "in_proj_main")
    proj2 = _matmul(hn, w_mla, F32, 512, w_mla.shape[1], "in_proj_mla")

    o_a = _hgrn(proj, lb_logits, hg_norm, layer)
    qn, qr, kn, vt, kr = _mla_prep(proj2, positions, mla_q_norm, mla_kv_norm, wq, wk, wvt)
    o_b = _attention(qn, qr, kn, kr, vt)
    x1, hn2 = _merge(o_a, o_b, proj, x, w_a.astype(BF16), w_b.astype(BF16), w_o.astype(BF16), norm_ffn)

    e_t, g_t = _route(hn2, peer_wq.T.astype(BF16), peer_k1, peer_k2)
    w = _wbuild(e_t.T, g_t.T)
    po = _peer(hn2, peer_u.astype(BF16), peer_v.astype(BF16), w)
    return _ple(x1, po, p.astype(BF16), w_pg.astype(BF16), w_pe.astype(BF16), norm_ple, norm_out)


def kernel(x, p, positions, norm_mix, w_in, lb_logits, hg_norm, mla_q_norm, mla_kv_norm, w_uq, w_ukv,
           w_a, w_b, w_o, norm_ffn, peer_wq, peer_k1, peer_k2, peer_u, peer_v, norm_ple, w_pg, w_pe,
           norm_final):
    b, t, d = x.shape
    depth = w_in.shape[0]
    assert b == 1 and depth == 1, "kernel supports one sequence and one layer"
    out = _layer(x.reshape(b * t, d), p[0].reshape(b * t, -1), positions.reshape(-1), 0,
                 norm_mix[0], w_in[0], lb_logits, hg_norm[0], mla_q_norm[0], mla_kv_norm[0],
                 w_uq[0], w_ukv[0], w_a[0], w_b[0], w_o[0], norm_ffn[0], peer_wq[0], peer_k1[0],
                 peer_k2[0], peer_u[0], peer_v[0], norm_ple[0], w_pg[0], w_pe[0], norm_final)
    return out.reshape(b, t, d)
```

```python
import functools

import jax
import jax.numpy as jnp
import numpy as np
from jax import lax
from jax.experimental import pallas as pl
from jax.experimental.pallas import tpu as pltpu

D_MODEL = 2048
HG_HEADS = 8
HG_DK = 128
HG_DV = 128
HG_WIDTH = HG_HEADS * HG_DK
HG_CHUNK = 64
MLA_HEADS = 8
MLA_NOPE = 128
MLA_ROPE = 64
MLA_V = 128
MLA_Q_LORA = 512
MLA_KV_LORA = 512
MLA_QK = MLA_NOPE + MLA_ROPE
ROPE_THETA = 10000.0
PEER_HEADS = 8
PEER_NKEYS = 128
PEER_QDIM = 256
PEER_TOPK = 16
NORM_EPS = 1e-6
MASK_VALUE = -1e30

LANES = 128
VMEM_LIMIT = 56 * 1024 * 1024

_NT = (((1,), (1,)), ((), ()))
_TN = (((0,), (0,)), ((), ()))

F32 = jnp.float32
BF16 = jnp.bfloat16


def _params(*sem):
    return pltpu.CompilerParams(dimension_semantics=sem, vmem_limit_bytes=VMEM_LIMIT)


def _rms(x, gain):
    return x * lax.rsqrt(jnp.mean(x * x, axis=-1, keepdims=True) + NORM_EPS) * gain


def _rmsnorm_kernel(x_ref, g_ref, o_ref):
    o_ref[...] = _rms(x_ref[...], g_ref[...]).astype(o_ref.dtype)


def _rmsnorm(x, gain, tm=512):
    n, d = x.shape
    return pl.pallas_call(
        _rmsnorm_kernel,
        grid=(n // tm,),
        in_specs=[pl.BlockSpec((tm, d), lambda i: (i, 0)), pl.BlockSpec((1, d), lambda i: (0, 0))],
        out_specs=pl.BlockSpec((tm, d), lambda i: (i, 0)),
        out_shape=jax.ShapeDtypeStruct((n, d), BF16),
        compiler_params=_params("parallel"),
        name="rmsnorm",
    )(x, gain.reshape(1, d))


def _matmul_kernel(a_ref, b_ref, o_ref):
    o_ref[...] = jnp.dot(a_ref[...], b_ref[...], preferred_element_type=F32).astype(o_ref.dtype)


def _matmul(a, b, out_dtype, tm, tn, name):
    m, k = a.shape
    _, n = b.shape
    return pl.pallas_call(
        _matmul_kernel,
        grid=(m // tm, n // tn),
        in_specs=[pl.BlockSpec((tm, k), lambda i, j: (i, 0)), pl.BlockSpec((k, tn), lambda i, j: (0, j))],
        out_specs=pl.BlockSpec((tm, tn), lambda i, j: (i, j)),
        out_shape=jax.ShapeDtypeStruct((m, n), out_dtype),
        compiler_params=_params("parallel", "arbitrary"),
        name=name,
    )(a, b)


def _matmul_ws_kernel(a_ref, w_ref, o_ref, wb_ref):
    @pl.when(pl.program_id(1) == 0)
    def _():
        wb_ref[...] = w_ref[...].astype(BF16)

    o_ref[...] = jnp.dot(a_ref[...], wb_ref[...], preferred_element_type=F32).astype(o_ref.dtype)


def _matmul_ws(a, w, cols, out_dtype, tm, tn, name):
    m, k = a.shape
    return pl.pallas_call(
        _matmul_ws_kernel,
        grid=(cols // tn, m // tm),
        in_specs=[pl.BlockSpec((tm, k), lambda j, i: (i, 0)), pl.BlockSpec((k, tn), lambda j, i: (0, j))],
        out_specs=pl.BlockSpec((tm, tn), lambda j, i: (i, j)),
        out_shape=jax.ShapeDtypeStruct((m, cols), out_dtype),
        scratch_shapes=[pltpu.VMEM((k, tn), BF16)],
        compiler_params=_params("arbitrary", "arbitrary"),
        name=name,
    )(a, w)


def _hgrn_kernel(q_ref, f_ref, i_ref, og_ref, lbl_ref, gain_ref, o_ref, st_ref, *, layer, chunks):
    @pl.when(pl.program_id(1) == 0)
    def _():
        st_ref[...] = jnp.zeros_like(st_ref)

    lbl = lbl_ref[...]
    e = jnp.exp(lbl - jnp.max(lbl, axis=0, keepdims=True))
    sm = e / jnp.sum(e, axis=0, keepdims=True)
    lb = jnp.sum(sm[: layer + 1], axis=0, keepdims=True)
    gain = gain_ref[...]
    c = HG_CHUNK
    row = lax.broadcasted_iota(jnp.int32, (c, c), 0)
    col = lax.broadcasted_iota(jnp.int32, (c, c), 1)
    causal = row >= col
    tri = causal.astype(F32)
    for n in range(chunks):
        sl = pl.ds(n * c, c)
        q = q_ref[sl, :]
        v = i_ref[sl, :]
        og = og_ref[sl, :]
        f = lb + (1.0 - lb) * jax.nn.sigmoid(f_ref[sl, :])
        g = jnp.log(f)
        k = 1.0 - f
        b = jnp.dot(tri, g, preferred_element_type=F32, precision=lax.Precision.HIGHEST)
        b_mid = b[c // 2 - 1:c // 2, :]
        b_last = b[c - 1:c, :]
        q_in = (q * jnp.exp(b - b_mid)).astype(BF16)
        k_in = (k * jnp.exp(b_mid - b)).astype(BF16)
        a = lax.dot_general(q_in, k_in, _NT, preferred_element_type=F32)
        a = jnp.where(causal, a, 0.0).astype(BF16)
        vb = v.astype(BF16)
        o = jnp.dot(a, vb, preferred_element_type=F32)
        st = st_ref[...]
        qe = (q * jnp.exp(b)).astype(BF16)
        o = o + lax.dot_general(qe, st.astype(BF16), _NT, preferred_element_type=F32)
        k_dec = (k * jnp.exp(b_last - b)).astype(BF16)
        dst = lax.dot_general(vb, k_dec, _TN, preferred_element_type=F32)
        st_ref[...] = jnp.exp(b_last) * st + dst
        o = o * lax.rsqrt(jnp.mean(o * o, axis=-1, keepdims=True) + NORM_EPS) * gain
        o = o * (og * jax.nn.sigmoid(og))
        o_ref[sl, :] = o.astype(o_ref.dtype)


def _hgrn(proj, lb_logits, gain, layer, tt=512):
    n = proj.shape[0]
    h = HG_HEADS
    nl = lb_logits.shape[0]
    blk = lambda g: pl.BlockSpec((tt, HG_DK), lambda hh, t, g=g: (t, g * h + hh))
    return pl.pallas_call(
        functools.partial(_hgrn_kernel, layer=layer, chunks=tt // HG_CHUNK),
        grid=(h, n // tt),
        in_specs=[blk(0), blk(1), blk(2), blk(3),
                  pl.BlockSpec((nl, HG_DK), lambda hh, t: (0, hh)),
                  pl.BlockSpec((1, HG_DV), lambda hh, t: (0, hh))],
        out_specs=pl.BlockSpec((tt, HG_DV), lambda hh, t: (t, hh)),
        out_shape=jax.ShapeDtypeStruct((n, h * HG_DV), BF16),
        scratch_shapes=[pltpu.VMEM((HG_DV, HG_DK), F32)],
        compiler_params=_params("parallel", "arbitrary"),
        name="hgrn2",
    )(proj, proj, proj, proj, lb_logits, gain.reshape(1, -1))


def _mla_prep_kernel(p_ref, pk_ref, pos_ref, fr_ref, qn_ref_g, kvn_ref_g, wq_ref, wk_ref, wvt_ref,
                     qn_ref, qr_ref, kn_ref, vt_ref, kr_ref, *, scale):
    hw = MLA_HEADS * MLA_NOPE
    half = MLA_ROPE // 2
    ang = pos_ref[...].astype(F32) * fr_ref[...]
    lane = lax.broadcasted_iota(jnp.int32, ang.shape, 1)
    cosv = jnp.where(lane < MLA_ROPE, jnp.cos(ang), 0.0)
    sinv = jnp.where(lane < MLA_ROPE, jnp.sin(ang), 0.0)

    def rope(y):
        return y * cosv + pltpu.roll(y, MLA_ROPE, 1) * sinv

    cq = _rms(p_ref[:, :MLA_Q_LORA], qn_ref_g[...]).astype(BF16)
    qres = jnp.dot(cq, wq_ref[...], preferred_element_type=F32)
    qn_ref[...] = (qres[:, :hw] * scale).astype(qn_ref.dtype)
    for h in range(MLA_HEADS):
        y = qres[:, hw + h * LANES: hw + (h + 1) * LANES]
        qr_ref[:, h * LANES:(h + 1) * LANES] = (rope(y) * scale).astype(qr_ref.dtype)
    ckv = _rms(p_ref[:, MLA_Q_LORA:MLA_Q_LORA + MLA_KV_LORA], kvn_ref_g[...]).astype(BF16)
    kn_ref[...] = jnp.dot(ckv, wk_ref[...], preferred_element_type=F32).astype(kn_ref.dtype)
    vt_ref[...] = lax.dot_general(wvt_ref[...], ckv, _NT, preferred_element_type=F32).astype(vt_ref.dtype)
    y = pk_ref[...]
    rot = jnp.where(lane < half, -pltpu.roll(y, LANES - half, 1), pltpu.roll(y, half, 1))
    kr_ref[...] = (y * cosv + rot * sinv).astype(kr_ref.dtype)


def _mla_prep(proj, positions, q_norm, kv_norm, wq, wk, wvt, tm=256):
    n = proj.shape[0]
    pw = MLA_Q_LORA + MLA_KV_LORA
    lat_blk = 4 * HG_WIDTH // pw
    kr_blk = (4 * HG_WIDTH + pw) // LANES
    hw = MLA_HEADS * MLA_NOPE
    half = MLA_ROPE // 2
    freqs = ROPE_THETA ** (-jnp.arange(half, dtype=F32) / half)
    fr = jnp.concatenate([freqs, freqs, jnp.zeros((LANES - MLA_ROPE,), F32)]).reshape(1, LANES)
    scale = float(1.0 / np.sqrt(MLA_QK).astype(np.float32)) * float(np.log2(np.e))
    full = lambda shape: pl.BlockSpec(shape, lambda i: (0, 0))
    tok = lambda w: pl.BlockSpec((tm, w), lambda i: (i, 0))
    return pl.pallas_call(
        functools.partial(_mla_prep_kernel, scale=scale),
        grid=(n // tm,),
        in_specs=[pl.BlockSpec((tm, pw), lambda i: (i, lat_blk)), pl.BlockSpec((tm, LANES), lambda i: (i, kr_blk)),
                  tok(1), full((1, LANES)), full((1, MLA_Q_LORA)), full((1, MLA_KV_LORA)),
                  full(wq.shape), full(wk.shape), full(wvt.shape)],
        out_specs=[tok(hw), tok(MLA_HEADS * LANES), tok(hw),
                   pl.BlockSpec((MLA_HEADS * MLA_V, tm), lambda i: (0, i)), tok(LANES)],
        out_shape=[jax.ShapeDtypeStruct((n, hw), BF16),
                   jax.ShapeDtypeStruct((n, MLA_HEADS * LANES), BF16),
                   jax.ShapeDtypeStruct((n, hw), BF16),
                   jax.ShapeDtypeStruct((MLA_HEADS * MLA_V, n), BF16),
                   jax.ShapeDtypeStruct((n, LANES), BF16)],
        compiler_params=_params("parallel"),
        name="mla_prep",
    )(proj, proj, positions.reshape(n, 1), fr, q_norm.reshape(1, -1), kv_norm.reshape(1, -1), wq, wk, wvt)


QSUB = 256


def _attn_kernel(qi_ref, kj_ref, qn_ref, qr_ref, kn_ref, kr_ref, vt_ref, o_ref, m_ref, l_ref, acc_ref, *, blk):
    s_id = pl.program_id(1)
    qi = qi_ref[s_id]
    kj = kj_ref[s_id]

    @pl.when(kj == 0)
    def _():
        m_ref[...] = jnp.full_like(m_ref, -jnp.inf)
        l_ref[...] = jnp.zeros_like(l_ref)
        acc_ref[...] = jnp.zeros_like(acc_ref)

    def update(masked):
        nsub = blk // QSUB

        def nkeys(c):
            return (c + 1) * QSUB if masked else blk

        def scores(c):
            sl = pl.ds(c * QSUB, QSUB)
            k = jnp.concatenate([kn_ref[0:nkeys(c), :], kr_ref[0:nkeys(c), :]], axis=1)
            q = jnp.concatenate([qn_ref[sl, :], qr_ref[sl, :]], axis=1)
            return lax.dot_general(k, q, _NT, preferred_element_type=F32)

        sts = [scores(c) for c in range(min(2, nsub))]
        for c in range(nsub):
            sl = pl.ds(c * QSUB, QSUB)
            st = sts[c]
            if c + 2 < nsub:
                sts.append(scores(c + 2))
            if masked:
                key = lax.broadcasted_iota(jnp.int32, st.shape, 0)
                qry = lax.broadcasted_iota(jnp.int32, st.shape, 1) + c * QSUB
                st = jnp.where(key <= qry, st, MASK_VALUE)
            m_prev = m_ref[:, sl]
            m_new = jnp.maximum(m_prev, jnp.max(st, axis=0, keepdims=True))
            alpha = jnp.exp2(m_prev - m_new)
            p = jnp.exp2(st - m_new)
            l_ref[:, sl] = alpha * l_ref[:, sl] + jnp.sum(p, axis=0, keepdims=True)
            pv = jnp.dot(vt_ref[:, 0:nkeys(c)], p.astype(BF16), preferred_element_type=F32)
            acc_ref[:, sl] = alpha * acc_ref[:, sl] + pv
            m_ref[:, sl] = m_new

    @pl.when(kj < qi)
    def _():
        update(False)

    @pl.when(kj == qi)
    def _():
        update(True)
        o_ref[...] = (acc_ref[...] / l_ref[...]).T.astype(o_ref.dtype)


def _attention(qn, qr, kn, kr, vt, blk=1024):
    n = qn.shape[0]
    nb = n // blk
    qi = np.array([i for i in range(nb) for j in range(i + 1)], np.int32)
    kj = np.array([j for i in range(nb) for j in range(i + 1)], np.int32)
    qspec = pl.BlockSpec((blk, LANES), lambda h, s, qi, kj: (qi[s], h))
    kspec = pl.BlockSpec((blk, LANES), lambda h, s, qi, kj: (kj[s], h))
    krspec = pl.BlockSpec((blk, LANES), lambda h, s, qi, kj: (kj[s], 0))
    vtspec = pl.BlockSpec((MLA_V, blk), lambda h, s, qi, kj: (h, kj[s]))
    grid_spec = pltpu.PrefetchScalarGridSpec(
        num_scalar_prefetch=2,
        grid=(MLA_HEADS, len(qi)),
        in_specs=[qspec, qspec, kspec, krspec, vtspec],
        out_specs=qspec,
        scratch_shapes=[pltpu.VMEM((1, blk), F32), pltpu.VMEM((1, blk), F32), pltpu.VMEM((MLA_V, blk), F32)],
    )
    return pl.pallas_call(
        functools.partial(_attn_kernel, blk=blk),
        grid_spec=grid_spec,
        out_shape=jax.ShapeDtypeStruct((n, MLA_HEADS * MLA_V), BF16),
        compiler_params=_params("parallel", "arbitrary"),
        name="mla_attention",
    )(jnp.asarray(qi), jnp.asarray(kj), qn, qr, kn, kr, vt)


def _merge_kernel(oa_ref, ob_ref, g0_ref, g1_ref, g2_ref, g3_ref, gt_ref, x_ref, wa_ref, wb_ref, wo_ref,
                  g_ref, x1_ref, hn_ref, *, off):
    d = x_ref.shape[1]
    ya = jnp.dot(oa_ref[...], wa_ref[...], preferred_element_type=F32)
    yb = jnp.dot(ob_ref[...], wb_ref[...], preferred_element_type=F32)
    gates = jnp.concatenate([g0_ref[...], g1_ref[...], g2_ref[...], g3_ref[...], gt_ref[...]], axis=1)
    ga = gates[:, off:off + d]
    gb = gates[:, off + d:off + 2 * d]
    y = jax.nn.sigmoid(ga) * ya + jax.nn.sigmoid(gb) * yb
    x1 = x_ref[...] + jnp.dot(y.astype(BF16), wo_ref[...], preferred_element_type=F32)
    x1_ref[...] = x1
    hn_ref[...] = _rms(x1, g_ref[...]).astype(hn_ref.dtype)


def _merge(oa, ob, proj, tail, g0, x, wa, wb, wo, gain, tm=256):
    n, d = x.shape
    gw = d // 2
    base = (g0 // gw) * gw
    assert base + 4 * gw == proj.shape[1] and g0 + 2 * d <= base + 4 * gw + LANES
    tok = lambda w, c=0: pl.BlockSpec((tm, w), lambda i, c=c: (i, c))
    res = lambda shape: pl.BlockSpec(shape, lambda i: (0, 0), pipeline_mode=pl.Buffered(1))
    return pl.pallas_call(
        functools.partial(_merge_kernel, off=g0 - base),
        grid=(n // tm,),
        in_specs=[tok(oa.shape[1]), tok(ob.shape[1])] + [tok(gw, base // gw + c) for c in range(4)]
                 + [tok(LANES), tok(d), res(wa.shape), res(wb.shape), res(wo.shape), res((1, d))],
        out_specs=[tok(d), tok(d)],
        out_shape=[jax.ShapeDtypeStruct((n, d), F32), jax.ShapeDtypeStruct((n, d), BF16)],
        compiler_params=_params("parallel"),
        name="merge_outproj",
    )(oa, ob, proj, proj, proj, proj, tail, x, wa, wb, wo, gain.reshape(1, d))


def _cand_tables():
    k = PEER_TOPK
    big = float(4 * k * k)
    ci = []
    pieces = [("b", 0, 16), ("b", 1, 8), ("b", 2, 8), ("b", 3, 8), ("a", 0, 16), ("a", 1, 8), ("a", 2, 8)]
    for kind, fixed, rows in pieces:
        for r in range(rows):
            a, b = (r, fixed) if kind == "b" else (fixed, r)
            ok = (a + 1) * (b + 1) <= k and (kind == "b" or b >= 4)
            ci.append(float(a * k + b) if ok else big)
    return pieces, np.asarray(ci, np.float32), big


def _top_rows(x, val_ref, idx_ref, lanes, k):
    rows = x.shape[0]
    iota = lax.broadcasted_iota(jnp.int32, x.shape, 0).astype(F32)
    for j in range(k):
        m = jnp.max(x, axis=0, keepdims=True)
        am = jnp.min(jnp.where(x == m, iota, float(rows)), axis=0, keepdims=True)
        val_ref[j:j + 1, lanes] = m
        idx_ref[j:j + 1, lanes] = am
        x = jnp.where(iota == am, -jnp.inf, x)


def _split_bf16(x):
    hi = x.astype(BF16)
    return hi, (x - hi.astype(F32)).astype(BF16)


def _route_kernel(h_ref, wq_ref, k1_ref, k2_ref, ci_ref, e_ref, g_ref,
                  v1_ref, i1_ref, v2_ref, i2_ref, cv_ref, ce_ref, *, pieces, big, hps):
    k = PEER_TOPK
    half = PEER_QDIM // 2
    tm = h_ref.shape[0]
    h = h_ref[...]
    ci = ci_ref[...]

    def scores(hh):
        qt = lax.dot_general(wq_ref[hh * PEER_QDIM:(hh + 1) * PEER_QDIM, :], h, _NT, preferred_element_type=F32)
        out = []
        for part, k_ref in ((0, k1_ref), (1, k2_ref)):
            q_hi, q_lo = _split_bf16(qt[part * half:(part + 1) * half])
            qs = jnp.concatenate([q_hi, q_lo, q_hi], axis=0)
            out.append(jnp.dot(k_ref[...], qs, preferred_element_type=F32))
        return out

    nxt = scores(0)
    for hh in range(hps):
        s1, s2 = nxt
        if hh + 1 < hps:
            nxt = scores(hh + 1)
        for c in range(tm // LANES):
            lanes = pl.ds(c * LANES, LANES)
            _top_rows(s1[:, c * LANES:(c + 1) * LANES], v1_ref, i1_ref, lanes, k)
            _top_rows(s2[:, c * LANES:(c + 1) * LANES], v2_ref, i2_ref, lanes, k)
            cand, ec = [], []
            for kind, fixed, rows in pieces:
                if kind == "b":
                    cand.append(v1_ref[0:rows, lanes] + v2_ref[fixed:fixed + 1, lanes])
                    ec.append(i1_ref[0:rows, lanes] * float(PEER_NKEYS) + i2_ref[fixed:fixed + 1, lanes])
                else:
                    cand.append(v1_ref[fixed:fixed + 1, lanes] + v2_ref[0:rows, lanes])
                    ec.append(i1_ref[fixed:fixed + 1, lanes] * float(PEER_NKEYS) + i2_ref[0:rows, lanes])
            cand = jnp.where(ci < big, jnp.concatenate(cand, axis=0), -jnp.inf)
            ec = jnp.concatenate(ec, axis=0)
            for j in range(k):
                m = jnp.max(cand, axis=0, keepdims=True)
                sel = jnp.min(jnp.where(cand == m, ci, big), axis=0, keepdims=True)
                hit = ci == sel
                cv_ref[j:j + 1, lanes] = m
                ce_ref[j:j + 1, lanes] = jnp.max(jnp.where(hit, ec, -1.0), axis=0, keepdims=True)
                cand = jnp.where(hit, -jnp.inf, cand)
        cv = cv_ref[...]
        ex = jnp.exp(cv - cv[0:1, :])
        g_ref[hh * k:(hh + 1) * k, :] = ex / jnp.sum(ex, axis=0, keepdims=True)
        e_ref[hh * k:(hh + 1) * k, :] = ce_ref[...].astype(jnp.int32)


def _route(hn, wq_t, k1, k2, tm=256, hps=2):
    n, d = hn.shape
    k = PEER_TOPK
    pieces, ci_col, big = _cand_tables()
    rows = ci_col.shape[0]
    ci = jnp.asarray(np.broadcast_to(ci_col[:, None], (rows, LANES)).copy())

    def stack(kk):
        hi, lo = _split_bf16(kk)
        return jnp.concatenate([hi, hi, lo], axis=1)

    full = lambda shape: pl.BlockSpec(shape, lambda i, g: (0, 0))
    out = pl.BlockSpec((hps * k, tm), lambda i, g: (g, i))
    sc = lambda: pltpu.VMEM((k, tm), F32)
    return pl.pallas_call(
        functools.partial(_route_kernel, pieces=pieces, big=big, hps=hps),
        grid=(n // tm, PEER_HEADS // hps),
        in_specs=[pl.BlockSpec((tm, d), lambda i, g: (i, 0)),
                  pl.BlockSpec((hps * PEER_QDIM, d), lambda i, g: (g, 0)),
                  full((PEER_NKEYS, 3 * (PEER_QDIM // 2))), full((PEER_NKEYS, 3 * (PEER_QDIM // 2))),
                  full((rows, LANES))],
        out_specs=[out, out],
        out_shape=[jax.ShapeDtypeStruct((PEER_HEADS * k, n), jnp.int32),
                   jax.ShapeDtypeStruct((PEER_HEADS * k, n), F32)],
        scratch_shapes=[sc(), sc(), sc(), sc(), sc(), sc()],
        compiler_params=_params("parallel", "arbitrary"),
        name="peer_route",
    )(hn, wq_t, stack(k1), stack(k2), ci)


W_TILE = 128
W_PITCH = W_TILE + 8


def _wbuild_kernel(e_ref, g_ref, w_ref):
    nk = PEER_NKEYS
    iota = lax.broadcasted_iota(jnp.int32, (nk, LANES), 0)
    for i in range(nk):
        w_ref[i * W_PITCH + W_TILE:(i + 1) * W_PITCH, :] = jnp.zeros((W_PITCH - W_TILE, nk), F32)

    def body(t, carry):
        e = e_ref[pl.ds(t, 1), :]
        g = g_ref[pl.ds(t, 1), :]
        p1 = jnp.where(iota == (e >> 7), g, 0.0).astype(BF16)
        p2 = jnp.where(iota == (e & (nk - 1)), 1.0, 0.0).astype(BF16)
        wt = lax.dot_general(p1, p2, _NT, preferred_element_type=F32)
        w_ref[pl.ds(t, nk, stride=W_PITCH), :] = wt
        return carry

    lax.fori_loop(0, W_TILE, body, 0, unroll=8)


def _wbuild(e, g):
    n, pairs = e.shape
    nk = PEER_NKEYS
    tiles = n // W_TILE
    w = pl.pallas_call(
        _wbuild_kernel,
        grid=(tiles,),
        in_specs=[pl.BlockSpec((W_TILE, pairs), lambda i: (i, 0)), pl.BlockSpec((W_TILE, pairs), lambda i: (i, 0))],
        out_specs=pl.BlockSpec((nk * W_PITCH, nk), lambda i: (i, 0)),
        out_shape=jax.ShapeDtypeStruct((tiles * nk * W_PITCH, nk), F32),
        compiler_params=_params("parallel"),
        name="peer_wbuild",
    )(e, g)
    return w.reshape(tiles, nk, W_PITCH, nk)


def _peer_kernel(h_ref, u_ref, v_ref, w_ref, o_ref, *, groups, tiles):
    @pl.when(pl.program_id(1) == 0)
    def _():
        o_ref[...] = jnp.zeros_like(o_ref)

    z = lax.dot_general(h_ref[...], u_ref[...].astype(BF16), _NT, preferred_element_type=F32)
    rows = []
    for k in range(tiles):
        cols = []
        for i in range(groups):
            zi = z[k * W_TILE:(k + 1) * W_TILE, i * LANES:(i + 1) * LANES]
            gelu = 0.5 * zi * (1.0 + lax.erf(zi * np.float32(np.sqrt(0.5))))
            cols.append((w_ref[k, i, 0:W_TILE, :] * gelu).astype(BF16))
        rows.append(jnp.concatenate(cols, axis=1))
    a = jnp.concatenate(rows, axis=0)
    o_ref[...] += jnp.dot(a, v_ref[...].astype(BF16), preferred_element_type=F32)


def _peer(hn, u, v, w, tm=1024, te=512):
    n, d = hn.shape
    ne = u.shape[0]
    groups = te // PEER_NKEYS
    tiles = tm // W_TILE
    return pl.pallas_call(
        functools.partial(_peer_kernel, groups=groups, tiles=tiles),
        grid=(n // tm, ne // te),
        in_specs=[pl.BlockSpec((tm, d), lambda i, c: (i, 0)),
                  pl.BlockSpec((te, d), lambda i, c: (c, 0)),
                  pl.BlockSpec((te, d), lambda i, c: (c, 0)),
                  pl.BlockSpec((tiles, groups, W_PITCH, PEER_NKEYS), lambda i, c: (i, c, 0, 0))],
        out_specs=pl.BlockSpec((tm, d), lambda i, c: (i, 0)),
        out_shape=jax.ShapeDtypeStruct((n, d), F32),
        compiler_params=_params("parallel", "arbitrary"),
        name="peer_experts",
    )(hn, u, v, w)


def _ple_kernel(x1_ref, po_ref, p_ref, wpg_ref, wpe_ref, gp_ref, gf_ref, o_ref):
    x2 = x1_ref[...] + po_ref[...]
    hp = _rms(x2, gp_ref[...]).astype(BF16)
    gate = jax.nn.sigmoid(jnp.dot(hp, wpg_ref[...], preferred_element_type=F32))
    pe = jnp.dot(p_ref[...], wpe_ref[...], preferred_element_type=F32)
    x3 = x2 + gate * pe
    o_ref[...] = _rms(x3, gf_ref[...]).astype(o_ref.dtype)


def _ple(x1, po, p, wpg, wpe, g_ple, g_final, tm=256):
    n, d = x1.shape
    tok = lambda w: pl.BlockSpec((tm, w), lambda i: (i, 0))
    res = lambda shape: pl.BlockSpec(shape, lambda i: (0, 0), pipeline_mode=pl.Buffered(1))
    return pl.pallas_call(
        _ple_kernel,
        grid=(n // tm,),
        in_specs=[tok(d), tok(d), tok(p.shape[1]), res(wpg.shape), res(wpe.shape), res((1, d)), res((1, d))],
        out_specs=tok(d),
        out_shape=jax.ShapeDtypeStruct((n, d), F32),
        compiler_params=_params("parallel"),
        name="ple_final",
    )(x1, po, p, wpg, wpe, g_ple.reshape(1, d), g_final.reshape(1, d))


def _layer(x, p, positions, layer, norm_mix, w_in, lb_logits, hg_norm, mla_q_norm, mla_kv_norm,
           w_uq, w_ukv, w_a, w_b, w_o, norm_ffn, peer_wq, peer_k1, peer_k2, peer_u, peer_v,
           norm_ple, w_pg, w_pe, norm_out):
    n, d = x.shape
    hgw = 4 * HG_WIDTH
    mla0 = hgw
    kr0 = mla0 + MLA_Q_LORA + MLA_KV_LORA
    g0 = kr0 + MLA_ROPE
    half = MLA_ROPE // 2

    uq = w_uq.reshape(MLA_Q_LORA, MLA_HEADS, MLA_QK)
    uq_r = uq[:, :, MLA_NOPE:]
    uq_rot = jnp.concatenate([-uq_r[:, :, half:], uq_r[:, :, :half]], axis=2)
    wq = jnp.concatenate([uq[:, :, :MLA_NOPE].reshape(MLA_Q_LORA, -1),
                          jnp.concatenate([uq_r, uq_rot], axis=2).reshape(MLA_Q_LORA, -1)],
                         axis=1).astype(BF16)
    ukv = w_ukv.reshape(MLA_KV_LORA, MLA_HEADS, MLA_NOPE + MLA_V)
    wk = ukv[:, :, :MLA_NOPE].reshape(MLA_KV_LORA, -1).astype(BF16)
    wvt = ukv[:, :, MLA_NOPE:].reshape(MLA_KV_LORA, -1).T.astype(BF16)

    tn = 512
    main_cols = (w_in.shape[1] // tn) * tn
    w_tail = w_in[:, main_cols:]
    w_tail = jnp.pad(w_tail, ((0, 0), (0, LANES - w_tail.shape[1]))).astype(BF16)
    hn = _rmsnorm(x, norm_mix)
    proj = _matmul_ws(hn, w_in, main_cols, F32, 1024, tn, "in_proj_main")
    tail = _matmul(hn, w_tail, F32, 1024, LANES, "in_proj_tail")

    o_a = _hgrn(proj, lb_logits, hg_norm, layer)
    qn, qr, kn, vt, kr = _mla_prep(proj, positions, mla_q_norm, mla_kv_norm, wq, wk, wvt)
    o_b = _attention(qn, qr, kn, kr, vt)
    x1, hn2 = _merge(o_a, o_b, proj, tail, g0, x, w_a.astype(BF16), w_b.astype(BF16), w_o.astype(BF16),
                     norm_ffn)

    e_t, g_t = _route(hn2, peer_wq.T.astype(BF16), peer_k1, peer_k2)
    w = _wbuild(e_t.T, g_t.T)
    po = _peer(hn2, peer_u, peer_v, w)
    return _ple(x1, po, p.astype(BF16), w_pg.astype(BF16), w_pe.astype(BF16), norm_ple, norm_out)


def kernel(x, p, positions, norm_mix, w_in, lb_logits, hg_norm, mla_q_norm, mla_kv_norm, w_uq, w_ukv,
           w_a, w_b, w_o, norm_ffn, peer_wq, peer_k1, peer_k2, peer_u, peer_v, norm_ple, w_pg, w_pe,
           norm_final):
    b, t, d = x.shape
    depth = w_in.shape[0]
    assert b == 1 and depth == 1, "kernel supports one sequence and one layer"
    out = _layer(x.reshape(b * t, d), p[0].reshape(b * t, -1), positions.reshape(-1), 0,
                 norm_mix[0], w_in[0], lb_logits, hg_norm[0], mla_q_norm[0], mla_kv_norm[0],
                 w_uq[0], w_ukv[0], w_a[0], w_b[0], w_o[0], norm_ffn[0], peer_wq[0], peer_k1[0],
                 peer_k2[0], peer_u[0], peer_v[0], norm_ple[0], w_pg[0], w_pe[0], norm_final)
    return out.reshape(b, t, d)
```

```python
import functools

import jax
import jax.numpy as jnp
import numpy as np
from jax import lax
from jax.experimental import pallas as pl
from jax.experimental.pallas import tpu as pltpu

D_MODEL = 2048
HG_HEADS = 8
HG_DK = 128
HG_DV = 128
HG_WIDTH = HG_HEADS * HG_DK
HG_CHUNK = 64
MLA_HEADS = 8
MLA_NOPE = 128
MLA_ROPE = 64
MLA_V = 128
MLA_Q_LORA = 512
MLA_KV_LORA = 512
MLA_QK = MLA_NOPE + MLA_ROPE
ROPE_THETA = 10000.0
PEER_HEADS = 8
PEER_NKEYS = 128
PEER_QDIM = 256
PEER_TOPK = 16
NORM_EPS = 1e-6
MASK_VALUE = -1e30

LANES = 128
VMEM_LIMIT = 56 * 1024 * 1024

_NT = (((1,), (1,)), ((), ()))
_TN = (((0,), (0,)), ((), ()))

F32 = jnp.float32
BF16 = jnp.bfloat16


def _params(*sem):
    return pltpu.CompilerParams(dimension_semantics=sem, vmem_limit_bytes=VMEM_LIMIT)


def _rms(x, gain):
    return x * lax.rsqrt(jnp.mean(x * x, axis=-1, keepdims=True) + NORM_EPS) * gain


def _rmsnorm_kernel(x_ref, g_ref, o_ref):
    o_ref[...] = _rms(x_ref[...], g_ref[...]).astype(o_ref.dtype)


def _rmsnorm(x, gain, tm=512):
    n, d = x.shape
    return pl.pallas_call(
        _rmsnorm_kernel,
        grid=(n // tm,),
        in_specs=[pl.BlockSpec((tm, d), lambda i: (i, 0)), pl.BlockSpec((1, d), lambda i: (0, 0))],
        out_specs=pl.BlockSpec((tm, d), lambda i: (i, 0)),
        out_shape=jax.ShapeDtypeStruct((n, d), BF16),
        compiler_params=_params("parallel"),
        name="rmsnorm",
    )(x, gain.reshape(1, d))


def _matmul_kernel(a_ref, b_ref, o_ref):
    o_ref[...] = jnp.dot(a_ref[...], b_ref[...], preferred_element_type=F32).astype(o_ref.dtype)


def _matmul(a, b, out_dtype, tm, tn, name):
    m, k = a.shape
    _, n = b.shape
    return pl.pallas_call(
        _matmul_kernel,
        grid=(m // tm, n // tn),
        in_specs=[pl.BlockSpec((tm, k), lambda i, j: (i, 0)), pl.BlockSpec((k, tn), lambda i, j: (0, j))],
        out_specs=pl.BlockSpec((tm, tn), lambda i, j: (i, j)),
        out_shape=jax.ShapeDtypeStruct((m, n), out_dtype),
        compiler_params=_params("parallel", "arbitrary"),
        name=name,
    )(a, b)


def _matmul_ws_kernel(a_ref, w_ref, o_ref, wb_ref):
    @pl.when(pl.program_id(1) == 0)
    def _():
        wb_ref[...] = w_ref[...].astype(BF16)

    o_ref[...] = lax.dot_general(a_ref[...], wb_ref[...], _NT, preferred_element_type=F32).astype(o_ref.dtype)


def _matmul_ws(a, w_t, cols, out_dtype, tm, tn, name):
    m, k = a.shape
    return pl.pallas_call(
        _matmul_ws_kernel,
        grid=(cols // tn, m // tm),
        in_specs=[pl.BlockSpec((tm, k), lambda j, i: (i, 0)), pl.BlockSpec((tn, k), lambda j, i: (j, 0))],
        out_specs=pl.BlockSpec((tm, tn), lambda j, i: (i, j)),
        out_shape=jax.ShapeDtypeStruct((m, cols), out_dtype),
        scratch_shapes=[pltpu.VMEM((tn, k), BF16)],
        compiler_params=_params("arbitrary", "arbitrary"),
        name=name,
    )(a, w_t)


HG_GROUP = 4


def _hgrn_kernel(q_ref, f_ref, i_ref, og_ref, lbl_ref, gain_ref, tri_ref, o_ref, st_ref, *, layer, chunks):
    @pl.when(pl.program_id(1) == 0)
    def _():
        st_ref[...] = jnp.zeros_like(st_ref)

    lbl = lbl_ref[...]
    e = jnp.exp(lbl - jnp.max(lbl, axis=0, keepdims=True))
    sm = e / jnp.sum(e, axis=0, keepdims=True)
    lb = jnp.sum(sm[: layer + 1], axis=0, keepdims=True)
    c = HG_CHUNK
    row = lax.broadcasted_iota(jnp.int32, (c, c), 0)
    col = lax.broadcasted_iota(jnp.int32, (c, c), 1)
    causal = row >= col

    q = q_ref[...]
    vb = i_ref[...].astype(BF16)
    f = lb + (1.0 - lb) * jax.nn.sigmoid(f_ref[...])
    g = jnp.log(f)
    k = 1.0 - f
    g_hi = g.astype(BF16)
    r1 = g - g_hi.astype(F32)
    g_mid = r1.astype(BF16)
    g_lo = (r1 - g_mid.astype(F32)).astype(BF16)
    gs = jnp.concatenate([g_hi, g_mid, g_lo], axis=1)
    gr = HG_GROUP * c
    parts = []
    for n in range(chunks // HG_GROUP):
        b3 = jnp.dot(tri_ref[...], gs[n * gr:(n + 1) * gr, :], preferred_element_type=F32)
        parts.append(b3[:, :HG_DK] + b3[:, HG_DK:2 * HG_DK] + b3[:, 2 * HG_DK:])
    b = jnp.concatenate(parts, axis=0)

    def rows_of(r):
        return jnp.concatenate([jnp.broadcast_to(b[n * c + r:n * c + r + 1, :], (c, HG_DK))
                                for n in range(chunks)], axis=0)

    b_mid = rows_of(c // 2 - 1)
    b_last = rows_of(c - 1)
    q_in = (q * jnp.exp(b - b_mid)).astype(BF16)
    k_in = (k * jnp.exp(b_mid - b)).astype(BF16)
    k_dec = (k * jnp.exp(b_last - b)).astype(BF16)
    qe = (q * jnp.exp(b)).astype(BF16)

    o_intra, dst = [], []
    for n in range(chunks):
        sl = slice(n * c, (n + 1) * c)
        a = lax.dot_general(q_in[sl], k_in[sl], _NT, preferred_element_type=F32)
        a = jnp.where(causal, a, 0.0).astype(BF16)
        o_intra.append(jnp.dot(a, vb[sl], preferred_element_type=F32))
        dst.append(lax.dot_general(vb[sl], k_dec[sl], _TN, preferred_element_type=F32))

    st = st_ref[...]
    outs = []
    for n in range(chunks):
        sl = slice(n * c, (n + 1) * c)
        outs.append(o_intra[n] + lax.dot_general(qe[sl], st.astype(BF16), _NT, preferred_element_type=F32))
        st = jnp.exp(b[(n + 1) * c - 1:(n + 1) * c, :]) * st + dst[n]
    st_ref[...] = st

    o = jnp.concatenate(outs, axis=0)
    og = og_ref[...]
    o = o * lax.rsqrt(jnp.mean(o * o, axis=-1, keepdims=True) + NORM_EPS) * gain_ref[...]
    o_ref[...] = (o * (og * jax.nn.sigmoid(og))).astype(o_ref.dtype)


def _hgrn(proj, lb_logits, gain, layer, tt=512):
    n = proj.shape[0]
    h = HG_HEADS
    nl = lb_logits.shape[0]
    gr = HG_GROUP * HG_CHUNK
    idx = np.arange(gr)
    tri = ((idx[:, None] >= idx[None, :]) & (idx[:, None] // HG_CHUNK == idx[None, :] // HG_CHUNK))
    blk = lambda g: pl.BlockSpec((tt, HG_DK), lambda hh, t, g=g: (t, g * h + hh))
    return pl.pallas_call(
        functools.partial(_hgrn_kernel, layer=layer, chunks=tt // HG_CHUNK),
        grid=(h, n // tt),
        in_specs=[blk(0), blk(1), blk(2), blk(3),
                  pl.BlockSpec((nl, HG_DK), lambda hh, t: (0, hh)),
                  pl.BlockSpec((1, HG_DV), lambda hh, t: (0, hh)),
                  pl.BlockSpec((gr, gr), lambda hh, t: (0, 0))],
        out_specs=pl.BlockSpec((tt, HG_DV), lambda hh, t: (t, hh)),
        out_shape=jax.ShapeDtypeStruct((n, h * HG_DV), BF16),
        scratch_shapes=[pltpu.VMEM((HG_DV, HG_DK), F32)],
        compiler_params=_params("parallel", "arbitrary"),
        name="hgrn2",
    )(proj, proj, proj, proj, lb_logits, gain.reshape(1, -1), jnp.asarray(tri, BF16))


def _mla_prep_kernel(p_ref, pk_ref, pos_ref, fr_ref, qn_ref_g, kvn_ref_g, wq_ref, wk_ref, wvt_ref,
                     qn_ref, qr_ref, kn_ref, vt_ref, kr_ref, *, scale):
    hw = MLA_HEADS * MLA_NOPE
    half = MLA_ROPE // 2
    ang = pos_ref[...].astype(F32) * fr_ref[...]
    lane = lax.broadcasted_iota(jnp.int32, ang.shape, 1)
    cosv = jnp.where(lane < MLA_ROPE, jnp.cos(ang), 0.0)
    sinv = jnp.where(lane < MLA_ROPE, jnp.sin(ang), 0.0)

    def rope(y):
        return y * cosv + pltpu.roll(y, MLA_ROPE, 1) * sinv

    cq = _rms(p_ref[:, :MLA_Q_LORA], qn_ref_g[...]).astype(BF16)
    qres = jnp.dot(cq, wq_ref[...], preferred_element_type=F32)
    qn_ref[...] = (qres[:, :hw] * scale).astype(qn_ref.dtype)
    for h in range(MLA_HEADS):
        y = qres[:, hw + h * LANES: hw + (h + 1) * LANES]
        qr_ref[:, h * LANES:(h + 1) * LANES] = (rope(y) * scale).astype(qr_ref.dtype)
    ckv = _rms(p_ref[:, MLA_Q_LORA:MLA_Q_LORA + MLA_KV_LORA], kvn_ref_g[...]).astype(BF16)
    kn_ref[...] = jnp.dot(ckv, wk_ref[...], preferred_element_type=F32).astype(kn_ref.dtype)
    vt_ref[...] = lax.dot_general(wvt_ref[...], ckv, _NT, preferred_element_type=F32).astype(vt_ref.dtype)
    y = pk_ref[...]
    rot = jnp.where(lane < half, -pltpu.roll(y, LANES - half, 1), pltpu.roll(y, half, 1))
    kr_ref[...] = (y * cosv + rot * sinv).astype(kr_ref.dtype)


def _mla_prep(proj, positions, q_norm, kv_norm, wq, wk, wvt, tm=256):
    n = proj.shape[0]
    pw = MLA_Q_LORA + MLA_KV_LORA
    lat_blk = 4 * HG_WIDTH // pw
    kr_blk = (4 * HG_WIDTH + pw) // LANES
    hw = MLA_HEADS * MLA_NOPE
    half = MLA_ROPE // 2
    freqs = ROPE_THETA ** (-jnp.arange(half, dtype=F32) / half)
    fr = jnp.concatenate([freqs, freqs, jnp.zeros((LANES - MLA_ROPE,), F32)]).reshape(1, LANES)
    scale = float(1.0 / np.sqrt(MLA_QK).astype(np.float32)) * float(np.log2(np.e))
    full = lambda shape: pl.BlockSpec(shape, lambda i: (0, 0))
    tok = lambda w: pl.BlockSpec((tm, w), lambda i: (i, 0))
    return pl.pallas_call(
        functools.partial(_mla_prep_kernel, scale=scale),
        grid=(n // tm,),
        in_specs=[pl.BlockSpec((tm, pw), lambda i: (i, lat_blk)), pl.BlockSpec((tm, LANES), lambda i: (i, kr_blk)),
                  tok(1), full((1, LANES)), full((1, MLA_Q_LORA)), full((1, MLA_KV_LORA)),
                  full(wq.shape), full(wk.shape), full(wvt.shape)],
        out_specs=[tok(hw), tok(MLA_HEADS * LANES), tok(hw),
                   pl.BlockSpec((MLA_HEADS * MLA_V, tm), lambda i: (0, i)), tok(LANES)],
        out_shape=[jax.ShapeDtypeStruct((n, hw), BF16),
                   jax.ShapeDtypeStruct((n, MLA_HEADS * LANES), BF16),
                   jax.ShapeDtypeStruct((n, hw), BF16),
                   jax.ShapeDtypeStruct((MLA_HEADS * MLA_V, n), BF16),
                   jax.ShapeDtypeStruct((n, LANES), BF16)],
        compiler_params=_params("parallel"),
        name="mla_prep",
    )(proj, proj, positions.reshape(n, 1), fr, q_norm.reshape(1, -1), kv_norm.reshape(1, -1), wq, wk, wvt)


QSUB = 256


def _attn_kernel(qi_ref, kj_ref, qn_ref, qr_ref, kn_ref, kr_ref, vt_ref, o_ref, m_ref, l_ref, acc_ref, *, blk):
    s_id = pl.program_id(1)
    qi = qi_ref[s_id]
    kj = kj_ref[s_id]

    @pl.when(kj == 0)
    def _():
        m_ref[...] = jnp.full_like(m_ref, -jnp.inf)
        l_ref[...] = jnp.zeros_like(l_ref)
        acc_ref[...] = jnp.zeros_like(acc_ref)

    def update(masked):
        nsub = blk // QSUB

        def nkeys(c):
            return (c + 1) * QSUB if masked else blk

        def scores(c):
            sl = pl.ds(c * QSUB, QSUB)
            k = jnp.concatenate([kn_ref[0:nkeys(c), :], kr_ref[0:nkeys(c), :]], axis=1)
            q = jnp.concatenate([qn_ref[sl, :], qr_ref[sl, :]], axis=1)
            return lax.dot_general(k, q, _NT, preferred_element_type=F32)

        sts = [scores(c) for c in range(min(2, nsub))]
        for c in range(nsub):
            sl = pl.ds(c * QSUB, QSUB)
            st = sts[c]
            if c + 2 < nsub:
                sts.append(scores(c + 2))
            if masked:
                key = lax.broadcasted_iota(jnp.int32, st.shape, 0)
                qry = lax.broadcasted_iota(jnp.int32, st.shape, 1) + c * QSUB
                st = jnp.where(key <= qry, st, MASK_VALUE)
            m_prev = m_ref[:, sl]
            m_new = jnp.maximum(m_prev, jnp.max(st, axis=0, keepdims=True))
            alpha = jnp.exp2(m_prev - m_new)
            p = jnp.exp2(st - m_new)
            l_ref[:, sl] = alpha * l_ref[:, sl] + jnp.sum(p, axis=0, keepdims=True)
            pv = jnp.dot(vt_ref[:, 0:nkeys(c)], p.astype(BF16), preferred_element_type=F32)
            acc_ref[:, sl] = alpha * acc_ref[:, sl] + pv
            m_ref[:, sl] = m_new

    @pl.when(kj < qi)
    def _():
        update(False)

    @pl.when(kj == qi)
    def _():
        update(True)
        o_ref[...] = (acc_ref[...] / l_ref[...]).T.astype(o_ref.dtype)


def _attention(qn, qr, kn, kr, vt, blk=1024):
    n = qn.shape[0]
    nb = n // blk
    qi = np.array([i for i in range(nb) for j in range(i + 1)], np.int32)
    kj = np.array([j for i in range(nb) for j in range(i + 1)], np.int32)
    qspec = pl.BlockSpec((blk, LANES), lambda h, s, qi, kj: (qi[s], h))
    kspec = pl.BlockSpec((blk, LANES), lambda h, s, qi, kj: (kj[s], h))
    krspec = pl.BlockSpec((blk, LANES), lambda h, s, qi, kj: (kj[s], 0))
    vtspec = pl.BlockSpec((MLA_V, blk), lambda h, s, qi, kj: (h, kj[s]))
    grid_spec = pltpu.PrefetchScalarGridSpec(
        num_scalar_prefetch=2,
        grid=(MLA_HEADS, len(qi)),
        in_specs=[qspec, qspec, kspec, krspec, vtspec],
        out_specs=qspec,
        scratch_shapes=[pltpu.VMEM((1, blk), F32), pltpu.VMEM((1, blk), F32), pltpu.VMEM((MLA_V, blk), F32)],
    )
    return pl.pallas_call(
        functools.partial(_attn_kernel, blk=blk),
        grid_spec=grid_spec,
        out_shape=jax.ShapeDtypeStruct((n, MLA_HEADS * MLA_V), BF16),
        compiler_params=_params("parallel", "arbitrary"),
        name="mla_attention",
    )(jnp.asarray(qi), jnp.asarray(kj), qn, qr, kn, kr, vt)


def _merge_kernel(oa_ref, ob_ref, g0_ref, g1_ref, g2_ref, g3_ref, gt_ref, x_ref, wa_ref, wb_ref, wo_ref,
                  g_ref, x1_ref, hn_ref, *, off):
    d = x_ref.shape[1]
    ya = jnp.dot(oa_ref[...], wa_ref[...], preferred_element_type=F32)
    yb = jnp.dot(ob_ref[...], wb_ref[...], preferred_element_type=F32)
    gates = jnp.concatenate([g0_ref[...], g1_ref[...], g2_ref[...], g3_ref[...], gt_ref[...]], axis=1)
    ga = gates[:, off:off + d]
    gb = gates[:, off + d:off + 2 * d]
    y = jax.nn.sigmoid(ga) * ya + jax.nn.sigmoid(gb) * yb
    x1 = x_ref[...] + jnp.dot(y.astype(BF16), wo_ref[...], preferred_element_type=F32)
    x1_ref[...] = x1
    hn_ref[...] = _rms(x1, g_ref[...]).astype(hn_ref.dtype)


def _merge(oa, ob, proj, tail, g0, x, wa, wb, wo, gain, tm=256):
    n, d = x.shape
    gw = d // 2
    base = (g0 // gw) * gw
    assert base + 4 * gw == proj.shape[1] and g0 + 2 * d <= base + 4 * gw + LANES
    tok = lambda w, c=0: pl.BlockSpec((tm, w), lambda i, c=c: (i, c))
    res = lambda shape: pl.BlockSpec(shape, lambda i: (0, 0), pipeline_mode=pl.Buffered(1))
    return pl.pallas_call(
        functools.partial(_merge_kernel, off=g0 - base),
        grid=(n // tm,),
        in_specs=[tok(oa.shape[1]), tok(ob.shape[1])] + [tok(gw, base // gw + c) for c in range(4)]
                 + [tok(LANES), tok(d), res(wa.shape), res(wb.shape), res(wo.shape), res((1, d))],
        out_specs=[tok(d), tok(d)],
        out_shape=[jax.ShapeDtypeStruct((n, d), F32), jax.ShapeDtypeStruct((n, d), BF16)],
        compiler_params=_params("parallel"),
        name="merge_outproj",
    )(oa, ob, proj, proj, proj, proj, tail, x, wa, wb, wo, gain.reshape(1, d))


def _cand_tables():
    k = PEER_TOPK
    big = float(4 * k * k)
    ci = []
    pieces = [("b", 0, 16), ("b", 1, 8), ("b", 2, 8), ("b", 3, 8), ("a", 0, 16), ("a", 1, 8), ("a", 2, 8)]
    for kind, fixed, rows in pieces:
        for r in range(rows):
            a, b = (r, fixed) if kind == "b" else (fixed, r)
            ok = (a + 1) * (b + 1) <= k and (kind == "b" or b >= 4)
            ci.append(float(a * k + b) if ok else big)
    return pieces, np.asarray(ci, np.float32), big


def _top_rows(x, val_ref, idx_ref, lanes, k):
    rows = x.shape[0]
    iota = lax.broadcasted_iota(jnp.int32, x.shape, 0).astype(F32)
    for j in range(k):
        m = jnp.max(x, axis=0, keepdims=True)
        am = jnp.min(jnp.where(x == m, iota, float(rows)), axis=0, keepdims=True)
        val_ref[j:j + 1, lanes] = m
        idx_ref[j:j + 1, lanes] = am
        x = jnp.where(iota == am, -jnp.inf, x)


def _split_bf16(x):
    hi = x.astype(BF16)
    return hi, (x - hi.astype(F32)).astype(BF16)


def _route_kernel(h_ref, wq_ref, k1_ref, k2_ref, ci_ref, e_ref, g_ref,
                  v1_ref, i1_ref, v2_ref, i2_ref, cv_ref, ce_ref, *, pieces, big, hps):
    k = PEER_TOPK
    half = PEER_QDIM // 2
    tm = h_ref.shape[0]
    h = h_ref[...]
    ci = ci_ref[...]

    def scores(hh):
        qt = lax.dot_general(wq_ref[hh * PEER_QDIM:(hh + 1) * PEER_QDIM, :], h, _NT, preferred_element_type=F32)
        out = []
        for part, k_ref in ((0, k1_ref), (1, k2_ref)):
            q_hi, q_lo = _split_bf16(qt[part * half:(part + 1) * half])
            qs = jnp.concatenate([q_hi, q_lo, q_hi], axis=0)
            out.append(jnp.dot(k_ref[...], qs, preferred_element_type=F32))
        return out

    nxt = scores(0)
    for hh in range(hps):
        s1, s2 = nxt
        if hh + 1 < hps:
            nxt = scores(hh + 1)
        for c in range(tm // LANES):
            lanes = pl.ds(c * LANES, LANES)
            _top_rows(s1[:, c * LANES:(c + 1) * LANES], v1_ref, i1_ref, lanes, k)
            _top_rows(s2[:, c * LANES:(c + 1) * LANES], v2_ref, i2_ref, lanes, k)
            cand, ec = [], []
            for kind, fixed, rows in pieces:
                if kind == "b":
                    cand.append(v1_ref[0:rows, lanes] + v2_ref[fixed:fixed + 1, lanes])
                    ec.append(i1_ref[0:rows, lanes] * float(PEER_NKEYS) + i2_ref[fixed:fixed + 1, lanes])
                else:
                    cand.append(v1_ref[fixed:fixed + 1, lanes] + v2_ref[0:rows, lanes])
                    ec.append(i1_ref[fixed:fixed + 1, lanes] * float(PEER_NKEYS) + i2_ref[0:rows, lanes])
            cand = jnp.where(ci < big, jnp.concatenate(cand, axis=0), -jnp.inf)
            ec = jnp.concatenate(ec, axis=0)
            for j in range(k):
                m = jnp.max(cand, axis=0, keepdims=True)
                sel = jnp.min(jnp.where(cand == m, ci, big), axis=0, keepdims=True)
                hit = ci == sel
                cv_ref[j:j + 1, lanes] = m
                ce_ref[j:j + 1, lanes] = jnp.max(jnp.where(hit, ec, -1.0), axis=0, keepdims=True)
                cand = jnp.where(hit, -jnp.inf, cand)
        cv = cv_ref[...]
        ex = jnp.exp(cv - cv[0:1, :])
        g_ref[hh * k:(hh + 1) * k, :] = ex / jnp.sum(ex, axis=0, keepdims=True)
        e_ref[hh * k:(hh + 1) * k, :] = ce_ref[...].astype(jnp.int32)


def _route(hn, wq_t, k1, k2, tm=256, hps=2):
    n, d = hn.shape
    k = PEER_TOPK
    pieces, ci_col, big = _cand_tables()
    rows = ci_col.shape[0]
    ci = jnp.asarray(np.broadcast_to(ci_col[:, None], (rows, LANES)).copy())

    def stack(kk):
        hi, lo = _split_bf16(kk)
        return jnp.concatenate([hi, hi, lo], axis=1)

    full = lambda shape: pl.BlockSpec(shape, lambda i, g: (0, 0))
    out = pl.BlockSpec((hps * k, tm), lambda i, g: (g, i))
    sc = lambda: pltpu.VMEM((k, tm), F32)
    return pl.pallas_call(
        functools.partial(_route_kernel, pieces=pieces, big=big, hps=hps),
        grid=(n // tm, PEER_HEADS // hps),
        in_specs=[pl.BlockSpec((tm, d), lambda i, g: (i, 0)),
                  pl.BlockSpec((hps * PEER_QDIM, d), lambda i, g: (g, 0)),
                  full((PEER_NKEYS, 3 * (PEER_QDIM // 2))), full((PEER_NKEYS, 3 * (PEER_QDIM // 2))),
                  full((rows, LANES))],
        out_specs=[out, out],
        out_shape=[jax.ShapeDtypeStruct((PEER_HEADS * k, n), jnp.int32),
                   jax.ShapeDtypeStruct((PEER_HEADS * k, n), F32)],
        scratch_shapes=[sc(), sc(), sc(), sc(), sc(), sc()],
        compiler_params=_params("parallel", "arbitrary"),
        name="peer_route",
    )(hn, wq_t, stack(k1), stack(k2), ci)


W_TILE = 128
W_PITCH = W_TILE + 8


def _wbuild_kernel(e_ref, g_ref, w_ref):
    nk = PEER_NKEYS
    iota = lax.broadcasted_iota(jnp.int32, (nk, LANES), 0)
    for i in range(nk):
        w_ref[i * W_PITCH + W_TILE:(i + 1) * W_PITCH, :] = jnp.zeros((W_PITCH - W_TILE, nk), F32)

    def body(t, carry):
        e = e_ref[pl.ds(t, 1), :]
        g = g_ref[pl.ds(t, 1), :]
        p1 = jnp.where(iota == (e >> 7), g, 0.0).astype(BF16)
        p2 = jnp.where(iota == (e & (nk - 1)), 1.0, 0.0).astype(BF16)
        wt = lax.dot_general(p1, p2, _NT, preferred_element_type=F32)
        w_ref[pl.ds(t, nk, stride=W_PITCH), :] = wt
        return carry

    lax.fori_loop(0, W_TILE, body, 0, unroll=32)


def _wbuild(e, g):
    n, pairs = e.shape
    nk = PEER_NKEYS
    tiles = n // W_TILE
    w = pl.pallas_call(
        _wbuild_kernel,
        grid=(tiles,),
        in_specs=[pl.BlockSpec((W_TILE, pairs), lambda i: (i, 0)), pl.BlockSpec((W_TILE, pairs), lambda i: (i, 0))],
        out_specs=pl.BlockSpec((nk * W_PITCH, nk), lambda i: (i, 0)),
        out_shape=jax.ShapeDtypeStruct((tiles * nk * W_PITCH, nk), F32),
        compiler_params=_params("parallel"),
        name="peer_wbuild",
    )(e, g)
    return w.reshape(tiles, nk, W_PITCH, nk)


def _peer_kernel(h_ref, u_ref, v_ref, w_ref, o_ref, *, groups, tiles):
    @pl.when(pl.program_id(1) == 0)
    def _():
        o_ref[...] = jnp.zeros_like(o_ref)

    z = lax.dot_general(h_ref[...], u_ref[...].astype(BF16), _NT, preferred_element_type=F32)
    rows = []
    for k in range(tiles):
        cols = []
        for i in range(groups):
            zi = z[k * W_TILE:(k + 1) * W_TILE, i * LANES:(i + 1) * LANES]
            gelu = 0.5 * zi * (1.0 + lax.erf(zi * np.float32(np.sqrt(0.5))))
            cols.append((w_ref[k, i, 0:W_TILE, :] * gelu).astype(BF16))
        rows.append(jnp.concatenate(cols, axis=1))
    a = jnp.concatenate(rows, axis=0)
    o_ref[...] += jnp.dot(a, v_ref[...].astype(BF16), preferred_element_type=F32)


def _peer(hn, u, v, w, tm=1024, te=512):
    n, d = hn.shape
    ne = u.shape[0]
    groups = te // PEER_NKEYS
    tiles = tm // W_TILE
    return pl.pallas_call(
        functools.partial(_peer_kernel, groups=groups, tiles=tiles),
        grid=(n // tm, ne // te),
        in_specs=[pl.BlockSpec((tm, d), lambda i, c: (i, 0)),
                  pl.BlockSpec((te, d), lambda i, c: (c, 0)),
                  pl.BlockSpec((te, d), lambda i, c: (c, 0)),
                  pl.BlockSpec((tiles, groups, W_PITCH, PEER_NKEYS), lambda i, c: (i, c, 0, 0))],
        out_specs=pl.BlockSpec((tm, d), lambda i, c: (i, 0)),
        out_shape=jax.ShapeDtypeStruct((n, d), F32),
        compiler_params=_params("parallel", "arbitrary"),
        name="peer_experts",
    )(hn, u, v, w)


def _ple_kernel(x1_ref, po_ref, p_ref, wpg_ref, wpe_ref, gp_ref, gf_ref, o_ref):
    x2 = x1_ref[...] + po_ref[...]
    hp = _rms(x2, gp_ref[...]).astype(BF16)
    gate = jax.nn.sigmoid(jnp.dot(hp, wpg_ref[...], preferred_element_type=F32))
    pe = jnp.dot(p_ref[...], wpe_ref[...], preferred_element_type=F32)
    x3 = x2 + gate * pe
    o_ref[...] = _rms(x3, gf_ref[...]).astype(o_ref.dtype)


def _ple(x1, po, p, wpg, wpe, g_ple, g_final, tm=256):
    n, d = x1.shape
    tok = lambda w: pl.BlockSpec((tm, w), lambda i: (i, 0))
    res = lambda shape: pl.BlockSpec(shape, lambda i: (0, 0), pipeline_mode=pl.Buffered(1))
    return pl.pallas_call(
        _ple_kernel,
        grid=(n // tm,),
        in_specs=[tok(d), tok(d), tok(p.shape[1]), res(wpg.shape), res(wpe.shape), res((1, d)), res((1, d))],
        out_specs=tok(d),
        out_shape=jax.ShapeDtypeStruct((n, d), F32),
        compiler_params=_params("parallel"),
        name="ple_final",
    )(x1, po, p, wpg, wpe, g_ple.reshape(1, d), g_final.reshape(1, d))


def _layer(x, p, positions, layer, norm_mix, w_in, lb_logits, hg_norm, mla_q_norm, mla_kv_norm,
           w_uq, w_ukv, w_a, w_b, w_o, norm_ffn, peer_wq, peer_k1, peer_k2, peer_u, peer_v,
           norm_ple, w_pg, w_pe, norm_out):
    n, d = x.shape
    hgw = 4 * HG_WIDTH
    mla0 = hgw
    kr0 = mla0 + MLA_Q_LORA + MLA_KV_LORA
    g0 = kr0 + MLA_ROPE
    half = MLA_ROPE // 2

    uq = w_uq.reshape(MLA_Q_LORA, MLA_HEADS, MLA_QK)
    uq_r = uq[:, :, MLA_NOPE:]
    uq_rot = jnp.concatenate([-uq_r[:, :, half:], uq_r[:, :, :half]], axis=2)
    wq = jnp.concatenate([uq[:, :, :MLA_NOPE].reshape(MLA_Q_LORA, -1),
                          jnp.concatenate([uq_r, uq_rot], axis=2).reshape(MLA_Q_LORA, -1)],
                         axis=1).astype(BF16)
    ukv = w_ukv.reshape(MLA_KV_LORA, MLA_HEADS, MLA_NOPE + MLA_V)
    wk = ukv[:, :, :MLA_NOPE].reshape(MLA_KV_LORA, -1).astype(BF16)
    wvt = ukv[:, :, MLA_NOPE:].reshape(MLA_KV_LORA, -1).T.astype(BF16)

    tn = 512
    main_cols = (w_in.shape[1] // tn) * tn
    w_tail = w_in[:, main_cols:]
    w_tail = jnp.pad(w_tail, ((0, 0), (0, LANES - w_tail.shape[1]))).astype(BF16)
    hn = _rmsnorm(x, norm_mix)
    proj = _matmul_ws(hn, w_in.T, main_cols, F32, 1024, tn, "in_proj_main")
    tail = _matmul(hn, w_tail, F32, 1024, LANES, "in_proj_tail")

    o_a = _hgrn(proj, lb_logits, hg_norm, layer)
    qn, qr, kn, vt, kr = _mla_prep(proj, positions, mla_q_norm, mla_kv_norm, wq, wk, wvt)
    o_b = _attention(qn, qr, kn, kr, vt)
    x1, hn2 = _merge(o_a, o_b, proj, tail, g0, x, w_a.astype(BF16), w_b.astype(BF16), w_o.astype(BF16),
                     norm_ffn)

    e_t, g_t = _route(hn2, peer_wq.T.astype(BF16), peer_k1, peer_k2)
    w = _wbuild(e_t.T, g_t.T)
    po = _peer(hn2, peer_u, peer_v, w)
    return _ple(x1, po, p.astype(BF16), w_pg.astype(BF16), w_pe.astype(BF16), norm_ple, norm_out)


def kernel(x, p, positions, norm_mix, w_in, lb_logits, hg_norm, mla_q_norm, mla_kv_norm, w_uq, w_ukv,
           w_a, w_b, w_o, norm_ffn, peer_wq, peer_k1, peer_k2, peer_u, peer_v, norm_ple, w_pg, w_pe,
           norm_final):
    b, t, d = x.shape
    depth = w_in.shape[0]
    assert b == 1 and depth == 1, "kernel supports one sequence and one layer"
    out = _layer(x.reshape(b * t, d), p[0].reshape(b * t, -1), positions.reshape(-1), 0,
                 norm_mix[0], w_in[0], lb_logits, hg_norm[0], mla_q_norm[0], mla_kv_norm[0],
                 w_uq[0], w_ukv[0], w_a[0], w_b[0], w_o[0], norm_ffn[0], peer_wq[0], peer_k1[0],
                 peer_k2[0], peer_u[0], peer_v[0], norm_ple[0], w_pg[0], w_pe[0], norm_final)
    return out.reshape(b, t, d)
```

```python
import functools

import jax
import jax.numpy as jnp
import numpy as np
from jax import lax
from jax.experimental import pallas as pl
from jax.experimental.pallas import tpu as pltpu

D_MODEL = 2048
HG_HEADS = 8
HG_DK = 128
HG_DV = 128
HG_WIDTH = HG_HEADS * HG_DK
HG_CHUNK = 64
MLA_HEADS = 8
MLA_NOPE = 128
MLA_ROPE = 64
MLA_V = 128
MLA_Q_LORA = 512
MLA_KV_LORA = 512
MLA_QK = MLA_NOPE + MLA_ROPE
ROPE_THETA = 10000.0
PEER_HEADS = 8
PEER_NKEYS = 128
PEER_QDIM = 256
PEER_TOPK = 16
NORM_EPS = 1e-6
MASK_VALUE = -1e30

LANES = 128
VMEM_LIMIT = 56 * 1024 * 1024

_NT = (((1,), (1,)), ((), ()))
_TN = (((0,), (0,)), ((), ()))

F32 = jnp.float32
BF16 = jnp.bfloat16


def _params(*sem):
    return pltpu.CompilerParams(dimension_semantics=sem, vmem_limit_bytes=VMEM_LIMIT)


def _rms(x, gain):
    return x * lax.rsqrt(jnp.mean(x * x, axis=-1, keepdims=True) + NORM_EPS) * gain


def _rmsnorm_kernel(x_ref, g_ref, o_ref):
    o_ref[...] = _rms(x_ref[...], g_ref[...]).astype(o_ref.dtype)


def _rmsnorm(x, gain, tm=512):
    n, d = x.shape
    return pl.pallas_call(
        _rmsnorm_kernel,
        grid=(n // tm,),
        in_specs=[pl.BlockSpec((tm, d), lambda i: (i, 0)), pl.BlockSpec((1, d), lambda i: (0, 0))],
        out_specs=pl.BlockSpec((tm, d), lambda i: (i, 0)),
        out_shape=jax.ShapeDtypeStruct((n, d), BF16),
        compiler_params=_params("parallel"),
        name="rmsnorm",
    )(x, gain.reshape(1, d))


def _matmul_ws_kernel(a_ref, w_ref, o_ref, wb_ref):
    @pl.when(pl.program_id(1) == 0)
    def _():
        wb_ref[...] = w_ref[...].astype(BF16)

    o_ref[...] = lax.dot_general(a_ref[...], wb_ref[...], _NT, preferred_element_type=F32).astype(o_ref.dtype)


def _matmul_ws(a, w_t, cols, out_dtype, tm, tn, name):
    m, k = a.shape
    return pl.pallas_call(
        _matmul_ws_kernel,
        grid=(cols // tn, m // tm),
        in_specs=[pl.BlockSpec((tm, k), lambda j, i: (i, 0)), pl.BlockSpec((tn, k), lambda j, i: (j, 0))],
        out_specs=pl.BlockSpec((tm, tn), lambda j, i: (i, j)),
        out_shape=jax.ShapeDtypeStruct((m, cols), out_dtype),
        scratch_shapes=[pltpu.VMEM((tn, k), BF16)],
        compiler_params=_params("arbitrary", "arbitrary"),
        name=name,
    )(a, w_t)


HG_GROUP = 4


def _hgrn_kernel(q_ref, f_ref, i_ref, og_ref, lbl_ref, gain_ref, tri_ref, o_ref, st_ref, *, layer, chunks):
    @pl.when(pl.program_id(1) == 0)
    def _():
        st_ref[...] = jnp.zeros_like(st_ref)

    lbl = lbl_ref[...]
    e = jnp.exp(lbl - jnp.max(lbl, axis=0, keepdims=True))
    sm = e / jnp.sum(e, axis=0, keepdims=True)
    lb = jnp.sum(sm[: layer + 1], axis=0, keepdims=True)
    c = HG_CHUNK
    row = lax.broadcasted_iota(jnp.int32, (c, c), 0)
    col = lax.broadcasted_iota(jnp.int32, (c, c), 1)
    causal = row >= col

    q = q_ref[...]
    vb = i_ref[...].astype(BF16)
    f = lb + (1.0 - lb) * jax.nn.sigmoid(f_ref[...])
    g = jnp.log(f)
    k = 1.0 - f
    g_hi = g.astype(BF16)
    r1 = g - g_hi.astype(F32)
    g_mid = r1.astype(BF16)
    g_lo = (r1 - g_mid.astype(F32)).astype(BF16)
    gs = jnp.concatenate([g_hi, g_mid, g_lo], axis=1)
    gr = HG_GROUP * c
    parts = []
    for n in range(chunks // HG_GROUP):
        b3 = jnp.dot(tri_ref[...], gs[n * gr:(n + 1) * gr, :], preferred_element_type=F32)
        parts.append(b3[:, :HG_DK] + b3[:, HG_DK:2 * HG_DK] + b3[:, 2 * HG_DK:])
    b = jnp.concatenate(parts, axis=0)

    def rows_of(r):
        return jnp.concatenate([jnp.broadcast_to(b[n * c + r:n * c + r + 1, :], (c, HG_DK))
                                for n in range(chunks)], axis=0)

    b_mid = rows_of(c // 2 - 1)
    b_last = rows_of(c - 1)
    q_in = (q * jnp.exp(b - b_mid)).astype(BF16)
    k_in = (k * jnp.exp(b_mid - b)).astype(BF16)
    k_dec = (k * jnp.exp(b_last - b)).astype(BF16)
    qe = (q * jnp.exp(b)).astype(BF16)

    o_intra, dst = [], []
    for n in range(chunks):
        sl = slice(n * c, (n + 1) * c)
        a = lax.dot_general(q_in[sl], k_in[sl], _NT, preferred_element_type=F32)
        a = jnp.where(causal, a, 0.0).astype(BF16)
        o_intra.append(jnp.dot(a, vb[sl], preferred_element_type=F32))
        dst.append(lax.dot_general(vb[sl], k_dec[sl], _TN, preferred_element_type=F32))

    st = st_ref[...]
    outs = []
    for n in range(chunks):
        sl = slice(n * c, (n + 1) * c)
        outs.append(o_intra[n] + lax.dot_general(qe[sl], st.astype(BF16), _NT, preferred_element_type=F32))
        st = jnp.exp(b[(n + 1) * c - 1:(n + 1) * c, :]) * st + dst[n]
    st_ref[...] = st

    o = jnp.concatenate(outs, axis=0)
    og = og_ref[...]
    o = o * lax.rsqrt(jnp.mean(o * o, axis=-1, keepdims=True) + NORM_EPS) * gain_ref[...]
    o_ref[...] = (o * (og * jax.nn.sigmoid(og))).astype(o_ref.dtype)


def _hgrn(proj, lb_logits, gain, layer, tt=512):
    n = proj.shape[0]
    h = HG_HEADS
    nl = lb_logits.shape[0]
    gr = HG_GROUP * HG_CHUNK
    idx = np.arange(gr)
    tri = ((idx[:, None] >= idx[None, :]) & (idx[:, None] // HG_CHUNK == idx[None, :] // HG_CHUNK))
    blk = lambda g: pl.BlockSpec((tt, HG_DK), lambda hh, t, g=g: (t, g * h + hh))
    return pl.pallas_call(
        functools.partial(_hgrn_kernel, layer=layer, chunks=tt // HG_CHUNK),
        grid=(h, n // tt),
        in_specs=[blk(0), blk(1), blk(2), blk(3),
                  pl.BlockSpec((nl, HG_DK), lambda hh, t: (0, hh)),
                  pl.BlockSpec((1, HG_DV), lambda hh, t: (0, hh)),
                  pl.BlockSpec((gr, gr), lambda hh, t: (0, 0))],
        out_specs=pl.BlockSpec((tt, HG_DV), lambda hh, t: (t, hh)),
        out_shape=jax.ShapeDtypeStruct((n, h * HG_DV), BF16),
        scratch_shapes=[pltpu.VMEM((HG_DV, HG_DK), F32)],
        compiler_params=_params("parallel", "arbitrary"),
        name="hgrn2",
    )(proj, proj, proj, proj, lb_logits, gain.reshape(1, -1), jnp.asarray(tri, BF16))


def _mla_prep_kernel(p_ref, pk_ref, pos_ref, fr_ref, qn_ref_g, kvn_ref_g, wq_ref, wk_ref, wvt_ref,
                     qn_ref, qr_ref, kn_ref, vt_ref, kr_ref, *, scale):
    hw = MLA_HEADS * MLA_NOPE
    half = MLA_ROPE // 2
    ang = pos_ref[...].astype(F32) * fr_ref[...]
    lane = lax.broadcasted_iota(jnp.int32, ang.shape, 1)
    cosv = jnp.where(lane < MLA_ROPE, jnp.cos(ang), 0.0)
    sinv = jnp.where(lane < MLA_ROPE, jnp.sin(ang), 0.0)

    def rope(y):
        return y * cosv + pltpu.roll(y, MLA_ROPE, 1) * sinv

    cq = _rms(p_ref[:, :MLA_Q_LORA], qn_ref_g[...]).astype(BF16)
    qres = jnp.dot(cq, wq_ref[...], preferred_element_type=F32)
    qn_ref[...] = (qres[:, :hw] * scale).astype(qn_ref.dtype)
    for h in range(MLA_HEADS):
        y = qres[:, hw + h * LANES: hw + (h + 1) * LANES]
        qr_ref[:, h * LANES:(h + 1) * LANES] = (rope(y) * scale).astype(qr_ref.dtype)
    ckv = _rms(p_ref[:, MLA_Q_LORA:MLA_Q_LORA + MLA_KV_LORA], kvn_ref_g[...]).astype(BF16)
    kn_ref[...] = jnp.dot(ckv, wk_ref[...], preferred_element_type=F32).astype(kn_ref.dtype)
    vt_ref[...] = lax.dot_general(wvt_ref[...], ckv, _NT, preferred_element_type=F32).astype(vt_ref.dtype)
    y = pk_ref[...]
    rot = jnp.where(lane < half, -pltpu.roll(y, LANES - half, 1), pltpu.roll(y, half, 1))
    kr_ref[...] = (y * cosv + rot * sinv).astype(kr_ref.dtype)


def _mla_prep(proj, positions, q_norm, kv_norm, wq, wk, wvt, tm=256):
    n = proj.shape[0]
    pw = MLA_Q_LORA + MLA_KV_LORA
    lat_blk = 4 * HG_WIDTH // pw
    kr_blk = (4 * HG_WIDTH + pw) // LANES
    hw = MLA_HEADS * MLA_NOPE
    half = MLA_ROPE // 2
    freqs = ROPE_THETA ** (-jnp.arange(half, dtype=F32) / half)
    fr = jnp.concatenate([freqs, freqs, jnp.zeros((LANES - MLA_ROPE,), F32)]).reshape(1, LANES)
    scale = float(1.0 / np.sqrt(MLA_QK).astype(np.float32)) * float(np.log2(np.e))
    full = lambda shape: pl.BlockSpec(shape, lambda i: (0, 0))
    tok = lambda w: pl.BlockSpec((tm, w), lambda i: (i, 0))
    return pl.pallas_call(
        functools.partial(_mla_prep_kernel, scale=scale),
        grid=(n // tm,),
        in_specs=[pl.BlockSpec((tm, pw), lambda i: (i, lat_blk)), pl.BlockSpec((tm, LANES), lambda i: (i, kr_blk)),
                  tok(1), full((1, LANES)), full((1, MLA_Q_LORA)), full((1, MLA_KV_LORA)),
                  full(wq.shape), full(wk.shape), full(wvt.shape)],
        out_specs=[tok(hw), tok(MLA_HEADS * LANES), tok(hw),
                   pl.BlockSpec((MLA_HEADS * MLA_V, tm), lambda i: (0, i)), tok(LANES)],
        out_shape=[jax.ShapeDtypeStruct((n, hw), BF16),
                   jax.ShapeDtypeStruct((n, MLA_HEADS * LANES), BF16),
                   jax.ShapeDtypeStruct((n, hw), BF16),
                   jax.ShapeDtypeStruct((MLA_HEADS * MLA_V, n), BF16),
                   jax.ShapeDtypeStruct((n, LANES), BF16)],
        compiler_params=_params("parallel"),
        name="mla_prep",
    )(proj, proj, positions.reshape(n, 1), fr, q_norm.reshape(1, -1), kv_norm.reshape(1, -1), wq, wk, wvt)


QSUB = 256


def _attn_kernel(qi_ref, kj_ref, qn_ref, qr_ref, kn_ref, kr_ref, vt_ref, o_ref, m_ref, l_ref, acc_ref, *, blk):
    s_id = pl.program_id(1)
    qi = qi_ref[s_id]
    kj = kj_ref[s_id]

    @pl.when(kj == 0)
    def _():
        m_ref[...] = jnp.full_like(m_ref, -jnp.inf)
        l_ref[...] = jnp.zeros_like(l_ref)
        acc_ref[...] = jnp.zeros_like(acc_ref)

    def update(masked):
        nsub = blk // QSUB

        def nkeys(c):
            return (c + 1) * QSUB if masked else blk

        def scores(c):
            sl = pl.ds(c * QSUB, QSUB)
            k = jnp.concatenate([kn_ref[0:nkeys(c), :], kr_ref[0:nkeys(c), :]], axis=1)
            q = jnp.concatenate([qn_ref[sl, :], qr_ref[sl, :]], axis=1)
            return lax.dot_general(k, q, _NT, preferred_element_type=F32)

        sts = [scores(c) for c in range(min(2, nsub))]
        for c in range(nsub):
            sl = pl.ds(c * QSUB, QSUB)
            st = sts[c]
            if c + 2 < nsub:
                sts.append(scores(c + 2))
            if masked:
                key = lax.broadcasted_iota(jnp.int32, st.shape, 0)
                qry = lax.broadcasted_iota(jnp.int32, st.shape, 1) + c * QSUB
                st = jnp.where(key <= qry, st, MASK_VALUE)
            m_prev = m_ref[:, sl]
            m_new = jnp.maximum(m_prev, jnp.max(st, axis=0, keepdims=True))
            alpha = jnp.exp2(m_prev - m_new)
            p = jnp.exp2(st - m_new)
            l_ref[:, sl] = alpha * l_ref[:, sl] + jnp.sum(p, axis=0, keepdims=True)
            pv = jnp.dot(vt_ref[:, 0:nkeys(c)], p.astype(BF16), preferred_element_type=F32)
            acc_ref[:, sl] = alpha * acc_ref[:, sl] + pv
            m_ref[:, sl] = m_new

    @pl.when(kj < qi)
    def _():
        update(False)

    @pl.when(kj == qi)
    def _():
        update(True)
        o_ref[...] = (acc_ref[...] / l_ref[...]).T.astype(o_ref.dtype)


def _attention(qn, qr, kn, kr, vt, blk=1024):
    n = qn.shape[0]
    nb = n // blk
    qi = np.array([i for i in range(nb) for j in range(i + 1)], np.int32)
    kj = np.array([j for i in range(nb) for j in range(i + 1)], np.int32)
    qspec = pl.BlockSpec((blk, LANES), lambda h, s, qi, kj: (qi[s], h))
    kspec = pl.BlockSpec((blk, LANES), lambda h, s, qi, kj: (kj[s], h))
    krspec = pl.BlockSpec((blk, LANES), lambda h, s, qi, kj: (kj[s], 0))
    vtspec = pl.BlockSpec((MLA_V, blk), lambda h, s, qi, kj: (h, kj[s]))
    grid_spec = pltpu.PrefetchScalarGridSpec(
        num_scalar_prefetch=2,
        grid=(MLA_HEADS, len(qi)),
        in_specs=[qspec, qspec, kspec, krspec, vtspec],
        out_specs=qspec,
        scratch_shapes=[pltpu.VMEM((1, blk), F32), pltpu.VMEM((1, blk), F32), pltpu.VMEM((MLA_V, blk), F32)],
    )
    return pl.pallas_call(
        functools.partial(_attn_kernel, blk=blk),
        grid_spec=grid_spec,
        out_shape=jax.ShapeDtypeStruct((n, MLA_HEADS * MLA_V), BF16),
        compiler_params=_params("parallel", "arbitrary"),
        name="mla_attention",
    )(jnp.asarray(qi), jnp.asarray(kj), qn, qr, kn, kr, vt)


def _merge_kernel(oa_ref, ob_ref, g0_ref, g1_ref, g2_ref, g3_ref, gt_ref, x_ref, wa_ref, wb_ref, wo_ref,
                  g_ref, x1_ref, hn_ref, *, off):
    d = x_ref.shape[1]
    ya = jnp.dot(oa_ref[...], wa_ref[...], preferred_element_type=F32)
    yb = jnp.dot(ob_ref[...], wb_ref[...], preferred_element_type=F32)
    gates = jnp.concatenate([g0_ref[...], g1_ref[...], g2_ref[...], g3_ref[...], gt_ref[...]], axis=1)
    ga = gates[:, off:off + d]
    gb = gates[:, off + d:off + 2 * d]
    y = jax.nn.sigmoid(ga) * ya + jax.nn.sigmoid(gb) * yb
    x1 = x_ref[...] + jnp.dot(y.astype(BF16), wo_ref[...], preferred_element_type=F32)
    x1_ref[...] = x1
    hn_ref[...] = _rms(x1, g_ref[...]).astype(hn_ref.dtype)


def _merge(oa, ob, proj, tail, g0, x, wa, wb, wo, gain, tm=256):
    n, d = x.shape
    gw = d // 2
    base = (g0 // gw) * gw
    assert base + 4 * gw == proj.shape[1] and g0 + 2 * d <= base + 4 * gw + LANES
    tok = lambda w, c=0: pl.BlockSpec((tm, w), lambda i, c=c: (i, c))
    res = lambda shape: pl.BlockSpec(shape, lambda i: (0, 0), pipeline_mode=pl.Buffered(1))
    return pl.pallas_call(
        functools.partial(_merge_kernel, off=g0 - base),
        grid=(n // tm,),
        in_specs=[tok(oa.shape[1]), tok(ob.shape[1])] + [tok(gw, base // gw + c) for c in range(4)]
                 + [tok(LANES), tok(d), res(wa.shape), res(wb.shape), res(wo.shape), res((1, d))],
        out_specs=[tok(d), tok(d)],
        out_shape=[jax.ShapeDtypeStruct((n, d), F32), jax.ShapeDtypeStruct((n, d), BF16)],
        compiler_params=_params("parallel"),
        name="merge_outproj",
    )(oa, ob, proj, proj, proj, proj, tail, x, wa, wb, wo, gain.reshape(1, d))


R_TM = 1024
R_BLOCKS = R_TM // LANES
R_PITCH = PEER_NKEYS + 8


def _cand_pairs():
    k = PEER_TOPK
    return [(a, b) for a in range(k) for b in range(k) if (a + 1) * (b + 1) <= k]


def _first_max(vals, *payloads):
    payloads = [list(p) for p in payloads]
    while len(vals) > 1:
        nv, npl = [], [[] for _ in payloads]
        for i in range(0, len(vals) - 1, 2):
            take = vals[i] >= vals[i + 1]
            nv.append(jnp.maximum(vals[i], vals[i + 1]))
            for dst, p in zip(npl, payloads):
                dst.append(jnp.where(take, p[i], p[i + 1]))
        if len(vals) % 2:
            nv.append(vals[-1])
            for dst, p in zip(npl, payloads):
                dst.append(p[-1])
        vals, payloads = nv, npl
    return (vals[0], *[p[0] for p in payloads])


def _top_keys(x, k):
    x = list(x)
    ids = [float(i) for i in range(len(x))]
    vals, idxs = [], []
    for _ in range(k):
        m, am = _first_max(x, ids)
        vals.append(m)
        idxs.append(am)
        x = [jnp.where(am == i, -jnp.inf, xi) for i, xi in zip(ids, x)]
    return vals, idxs


def _split_bf16(x):
    hi = x.astype(BF16)
    return hi, (x - hi.astype(F32)).astype(BF16)


def _route_kernel(h_ref, wq_ref, k1_ref, k2_ref, e_ref, g_ref, s1_ref, s2_ref, *, hps):
    k = PEER_TOPK
    half = PEER_QDIM // 2
    h = h_ref[...]
    pairs = _cand_pairs()

    def scores(hh):
        qt = lax.dot_general(wq_ref[hh * PEER_QDIM:(hh + 1) * PEER_QDIM, :], h, _NT, preferred_element_type=F32)
        out = []
        for part, k_ref in ((0, k1_ref), (1, k2_ref)):
            q_hi, q_lo = _split_bf16(qt[part * half:(part + 1) * half])
            qs = jnp.concatenate([q_hi, q_lo, q_hi], axis=0)
            out.append(jnp.dot(k_ref[...], qs, preferred_element_type=F32))
        return out

    def by_key(s, s_ref):
        for j in range(R_BLOCKS):
            s_ref[j * R_PITCH:j * R_PITCH + PEER_NKEYS, :] = s[:, j * LANES:(j + 1) * LANES]
        return [s_ref[pl.ds(key, R_BLOCKS, stride=R_PITCH), :] for key in range(PEER_NKEYS)]

    nxt = scores(0)
    for hh in range(hps):
        s1, s2 = nxt
        if hh + 1 < hps:
            nxt = scores(hh + 1)
        v1, i1 = _top_keys(by_key(s1, s1_ref), k)
        v2, i2 = _top_keys(by_key(s2, s2_ref), k)
        cand = [v1[a] + v2[b] for a, b in pairs]
        ec = [i1[a] * float(PEER_NKEYS) + i2[b] for a, b in pairs]
        ci = [float(a * k + b) for a, b in pairs]
        cv, ce = [], []
        for _ in range(k):
            m, sel, e = _first_max(cand, ci, ec)
            cv.append(m)
            ce.append(e)
            cand = [jnp.where(sel == c, -jnp.inf, x) for c, x in zip(ci, cand)]
        ex = [jnp.exp(c - cv[0]) for c in cv]
        den = ex[0]
        for x in ex[1:]:
            den = den + x
        for j in range(k):
            g_ref[0, hh * k + j] = ex[j] / den
            e_ref[0, hh * k + j] = ce[j].astype(jnp.int32)


def _route(hn, wq_t, k1, k2, hps=2):
    n, d = hn.shape
    k = PEER_TOPK
    tiles = n // R_TM

    def stack(kk):
        hi, lo = _split_bf16(kk)
        return jnp.concatenate([hi, hi, lo], axis=1)

    full = lambda shape: pl.BlockSpec(shape, lambda i, g: (0, 0))
    out = pl.BlockSpec((1, hps * k, R_BLOCKS, LANES), lambda i, g: (i, g, 0, 0))
    sc = lambda: pltpu.VMEM((R_BLOCKS * R_PITCH, LANES), F32)
    return pl.pallas_call(
        functools.partial(_route_kernel, hps=hps),
        grid=(tiles, PEER_HEADS // hps),
        in_specs=[pl.BlockSpec((R_TM, d), lambda i, g: (i, 0)),
                  pl.BlockSpec((hps * PEER_QDIM, d), lambda i, g: (g, 0)),
                  full((PEER_NKEYS, 3 * (PEER_QDIM // 2))), full((PEER_NKEYS, 3 * (PEER_QDIM // 2)))],
        out_specs=[out, out],
        out_shape=[jax.ShapeDtypeStruct((tiles, PEER_HEADS * k, R_BLOCKS, LANES), jnp.int32),
                   jax.ShapeDtypeStruct((tiles, PEER_HEADS * k, R_BLOCKS, LANES), F32)],
        scratch_shapes=[sc(), sc()],
        compiler_params=_params("parallel", "arbitrary"),
        name="peer_route",
    )(hn, wq_t, stack(k1), stack(k2))


W_TILE = 128
W_PITCH = W_TILE + 8


def _wbuild_kernel(e_ref, g_ref, w_ref):
    nk = PEER_NKEYS
    iota = lax.broadcasted_iota(jnp.int32, (nk, LANES), 0)
    for i in range(nk):
        w_ref[i * W_PITCH + W_TILE:(i + 1) * W_PITCH, :] = jnp.zeros((W_PITCH - W_TILE, nk), F32)

    def body(t, carry):
        e = e_ref[pl.ds(t, 1), :]
        g = g_ref[pl.ds(t, 1), :]
        p1 = jnp.where(iota == (e >> 7), g, 0.0).astype(BF16)
        p2 = jnp.where(iota == (e & (nk - 1)), 1.0, 0.0).astype(BF16)
        wt = lax.dot_general(p1, p2, _NT, preferred_element_type=F32)
        w_ref[pl.ds(t, nk, stride=W_PITCH), :] = wt
        return carry

    lax.fori_loop(0, W_TILE, body, 0, unroll=32)


def _wbuild(e, g):
    n, pairs = e.shape
    nk = PEER_NKEYS
    tiles = n // W_TILE
    w = pl.pallas_call(
        _wbuild_kernel,
        grid=(tiles,),
        in_specs=[pl.BlockSpec((W_TILE, pairs), lambda i: (i, 0)), pl.BlockSpec((W_TILE, pairs), lambda i: (i, 0))],
        out_specs=pl.BlockSpec((nk * W_PITCH, nk), lambda i: (i, 0)),
        out_shape=jax.ShapeDtypeStruct((tiles * nk * W_PITCH, nk), F32),
        compiler_params=_params("parallel"),
        name="peer_wbuild",
    )(e, g)
    return w.reshape(tiles, nk, W_PITCH, nk)


def _peer_kernel(h_ref, u_ref, v_ref, w_ref, o_ref, *, groups, tiles):
    @pl.when(pl.program_id(1) == 0)
    def _():
        o_ref[...] = jnp.zeros_like(o_ref)

    z = lax.dot_general(h_ref[...], u_ref[...].astype(BF16), _NT, preferred_element_type=F32)
    rows = []
    for k in range(tiles):
        cols = []
        for i in range(groups):
            zi = z[k * W_TILE:(k + 1) * W_TILE, i * LANES:(i + 1) * LANES]
            gelu = 0.5 * zi * (1.0 + lax.erf(zi * np.float32(np.sqrt(0.5))))
            cols.append((w_ref[k, i, 0:W_TILE, :] * gelu).astype(BF16))
        rows.append(jnp.concatenate(cols, axis=1))
    a = jnp.concatenate(rows, axis=0)
    o_ref[...] += jnp.dot(a, v_ref[...].astype(BF16), preferred_element_type=F32)


def _peer(hn, u, v, w, tm=1024, te=512):
    n, d = hn.shape
    ne = u.shape[0]
    groups = te // PEER_NKEYS
    tiles = tm // W_TILE
    return pl.pallas_call(
        functools.partial(_peer_kernel, groups=groups, tiles=tiles),
        grid=(n // tm, ne // te),
        in_specs=[pl.BlockSpec((tm, d), lambda i, c: (i, 0)),
                  pl.BlockSpec((te, d), lambda i, c: (c, 0)),
                  pl.BlockSpec((te, d), lambda i, c: (c, 0)),
                  pl.BlockSpec((tiles, groups, W_PITCH, PEER_NKEYS), lambda i, c: (i, c, 0, 0))],
        out_specs=pl.BlockSpec((tm, d), lambda i, c: (i, 0)),
        out_shape=jax.ShapeDtypeStruct((n, d), F32),
        compiler_params=_params("parallel", "arbitrary"),
        name="peer_experts",
    )(hn, u, v, w)


def _ple_kernel(x1_ref, po_ref, p_ref, wpg_ref, wpe_ref, gp_ref, gf_ref, o_ref):
    x2 = x1_ref[...] + po_ref[...]
    hp = _rms(x2, gp_ref[...]).astype(BF16)
    gate = jax.nn.sigmoid(jnp.dot(hp, wpg_ref[...], preferred_element_type=F32))
    pe = jnp.dot(p_ref[...], wpe_ref[...], preferred_element_type=F32)
    x3 = x2 + gate * pe
    o_ref[...] = _rms(x3, gf_ref[...]).astype(o_ref.dtype)


def _ple(x1, po, p, wpg, wpe, g_ple, g_final, tm=256):
    n, d = x1.shape
    tok = lambda w: pl.BlockSpec((tm, w), lambda i: (i, 0))
    res = lambda shape: pl.BlockSpec(shape, lambda i: (0, 0), pipeline_mode=pl.Buffered(1))
    return pl.pallas_call(
        _ple_kernel,
        grid=(n // tm,),
        in_specs=[tok(d), tok(d), tok(p.shape[1]), res(wpg.shape), res(wpe.shape), res((1, d)), res((1, d))],
        out_specs=tok(d),
        out_shape=jax.ShapeDtypeStruct((n, d), F32),
        compiler_params=_params("parallel"),
        name="ple_final",
    )(x1, po, p, wpg, wpe, g_ple.reshape(1, d), g_final.reshape(1, d))


def _layer(x, p, positions, layer, norm_mix, w_in, lb_logits, hg_norm, mla_q_norm, mla_kv_norm,
           w_uq, w_ukv, w_a, w_b, w_o, norm_ffn, peer_wq, peer_k1, peer_k2, peer_u, peer_v,
           norm_ple, w_pg, w_pe, norm_out):
    n, d = x.shape
    hgw = 4 * HG_WIDTH
    mla0 = hgw
    kr0 = mla0 + MLA_Q_LORA + MLA_KV_LORA
    g0 = kr0 + MLA_ROPE
    half = MLA_ROPE // 2

    uq = w_uq.reshape(MLA_Q_LORA, MLA_HEADS, MLA_QK)
    uq_r = uq[:, :, MLA_NOPE:]
    uq_rot = jnp.concatenate([-uq_r[:, :, half:], uq_r[:, :, :half]], axis=2)
    wq = jnp.concatenate([uq[:, :, :MLA_NOPE].reshape(MLA_Q_LORA, -1),
                          jnp.concatenate([uq_r, uq_rot], axis=2).reshape(MLA_Q_LORA, -1)],
                         axis=1).astype(BF16)
    ukv = w_ukv.reshape(MLA_KV_LORA, MLA_HEADS, MLA_NOPE + MLA_V)
    wk = ukv[:, :, :MLA_NOPE].reshape(MLA_KV_LORA, -1).astype(BF16)
    wvt = ukv[:, :, MLA_NOPE:].reshape(MLA_KV_LORA, -1).T.astype(BF16)

    tn = 1024
    w_in_t = w_in.T
    main_cols = (w_in_t.shape[0] // tn) * tn
    w_tail_t = w_in_t[main_cols:]
    w_tail_t = jnp.pad(w_tail_t, ((0, LANES - w_tail_t.shape[0]), (0, 0)))
    hn = _rmsnorm(x, norm_mix)
    proj = _matmul_ws(hn, w_in_t, main_cols, F32, 1024, tn, "in_proj_main")
    tail = _matmul_ws(hn, w_tail_t, LANES, F32, 1024, LANES, "in_proj_tail")

    o_a = _hgrn(proj, lb_logits, hg_norm, layer)
    qn, qr, kn, vt, kr = _mla_prep(proj, positions, mla_q_norm, mla_kv_norm, wq, wk, wvt)
    o_b = _attention(qn, qr, kn, kr, vt)
    x1, hn2 = _merge(o_a, o_b, proj, tail, g0, x, w_a.astype(BF16), w_b.astype(BF16), w_o.astype(BF16),
                     norm_ffn)

    e4, g4 = _route(hn2, peer_wq.T.astype(BF16), peer_k1, peer_k2)
    by_token = lambda a: a.transpose(0, 2, 3, 1).reshape(n, a.shape[1])
    w = _wbuild(by_token(e4), by_token(g4))
    po = _peer(hn2, peer_u, peer_v, w)
    return _ple(x1, po, p.astype(BF16), w_pg.astype(BF16), w_pe.astype(BF16), norm_ple, norm_out)


def kernel(x, p, positions, norm_mix, w_in, lb_logits, hg_norm, mla_q_norm, mla_kv_norm, w_uq, w_ukv,
           w_a, w_b, w_o, norm_ffn, peer_wq, peer_k1, peer_k2, peer_u, peer_v, norm_ple, w_pg, w_pe,
           norm_final):
    b, t, d = x.shape
    depth = w_in.shape[0]
    assert b == 1 and depth == 1, "kernel supports one sequence and one layer"
    out = _layer(x.reshape(b * t, d), p[0].reshape(b * t, -1), positions.reshape(-1), 0,
                 norm_mix[0], w_in[0], lb_logits, hg_norm[0], mla_q_norm[0], mla_kv_norm[0],
                 w_uq[0], w_ukv[0], w_a[0], w_b[0], w_o[0], norm_ffn[0], peer_wq[0], peer_k1[0],
                 peer_k2[0], peer_u[0], peer_v[0], norm_ple[0], w_pg[0], w_pe[0], norm_final)
    return out.reshape(b, t, d)
```

```python
import functools

import jax
import jax.numpy as jnp
import numpy as np
from jax import lax
from jax.experimental import pallas as pl
from jax.experimental.pallas import tpu as pltpu

D_MODEL = 2048
HG_HEADS = 8
HG_DK = 128
HG_DV = 128
HG_WIDTH = HG_HEADS * HG_DK
HG_CHUNK = 64
MLA_HEADS = 8
MLA_NOPE = 128
MLA_ROPE = 64
MLA_V = 128
MLA_Q_LORA = 512
MLA_KV_LORA = 512
MLA_QK = MLA_NOPE + MLA_ROPE
ROPE_THETA = 10000.0
PEER_HEADS = 8
PEER_NKEYS = 128
PEER_QDIM = 256
PEER_TOPK = 16
NORM_EPS = 1e-6
MASK_VALUE = -1e30

LANES = 128
VMEM_LIMIT = 56 * 1024 * 1024

_NT = (((1,), (1,)), ((), ()))
_TN = (((0,), (0,)), ((), ()))

F32 = jnp.float32
BF16 = jnp.bfloat16


def _params(*sem):
    return pltpu.CompilerParams(dimension_semantics=sem, vmem_limit_bytes=VMEM_LIMIT)


def _rms(x, gain):
    return x * lax.rsqrt(jnp.mean(x * x, axis=-1, keepdims=True) + NORM_EPS) * gain


def _rmsnorm_kernel(x_ref, g_ref, o_ref):
    o_ref[...] = _rms(x_ref[...], g_ref[...]).astype(o_ref.dtype)


def _rmsnorm(x, gain, tm=512):
    n, d = x.shape
    return pl.pallas_call(
        _rmsnorm_kernel,
        grid=(n // tm,),
        in_specs=[pl.BlockSpec((tm, d), lambda i: (i, 0)), pl.BlockSpec((1, d), lambda i: (0, 0))],
        out_specs=pl.BlockSpec((tm, d), lambda i: (i, 0)),
        out_shape=jax.ShapeDtypeStruct((n, d), BF16),
        compiler_params=_params("parallel"),
        name="rmsnorm",
    )(x, gain.reshape(1, d))


def _matmul_ws_kernel(a_ref, w_ref, o_ref, wb_ref):
    @pl.when(pl.program_id(1) == 0)
    def _():
        wb_ref[...] = w_ref[...].astype(BF16)

    o_ref[...] = lax.dot_general(a_ref[...], wb_ref[...], _NT, preferred_element_type=F32).astype(o_ref.dtype)


def _matmul_ws(a, w_t, cols, out_dtype, tm, tn, name):
    m, k = a.shape
    return pl.pallas_call(
        _matmul_ws_kernel,
        grid=(cols // tn, m // tm),
        in_specs=[pl.BlockSpec((tm, k), lambda j, i: (i, 0)), pl.BlockSpec((tn, k), lambda j, i: (j, 0))],
        out_specs=pl.BlockSpec((tm, tn), lambda j, i: (i, j)),
        out_shape=jax.ShapeDtypeStruct((m, cols), out_dtype),
        scratch_shapes=[pltpu.VMEM((tn, k), BF16)],
        compiler_params=_params("arbitrary", "arbitrary"),
        name=name,
    )(a, w_t)


HG_GROUP = 4


def _hgrn_kernel(q_ref, f_ref, i_ref, og_ref, lbl_ref, gain_ref, tri_ref, o_ref, st_ref, *, layer, chunks):
    @pl.when(pl.program_id(1) == 0)
    def _():
        st_ref[...] = jnp.zeros_like(st_ref)

    lbl = lbl_ref[...]
    e = jnp.exp(lbl - jnp.max(lbl, axis=0, keepdims=True))
    sm = e / jnp.sum(e, axis=0, keepdims=True)
    lb = jnp.sum(sm[: layer + 1], axis=0, keepdims=True)
    c = HG_CHUNK
    row = lax.broadcasted_iota(jnp.int32, (c, c), 0)
    col = lax.broadcasted_iota(jnp.int32, (c, c), 1)
    causal = row >= col

    q = q_ref[...]
    vb = i_ref[...].astype(BF16)
    f = lb + (1.0 - lb) * jax.nn.sigmoid(f_ref[...])
    g = jnp.log(f)
    k = 1.0 - f
    g_hi = g.astype(BF16)
    r1 = g - g_hi.astype(F32)
    g_mid = r1.astype(BF16)
    g_lo = (r1 - g_mid.astype(F32)).astype(BF16)
    gs = jnp.concatenate([g_hi, g_mid, g_lo], axis=1)
    gr = HG_GROUP * c
    parts = []
    for n in range(chunks // HG_GROUP):
        b3 = jnp.dot(tri_ref[...], gs[n * gr:(n + 1) * gr, :], preferred_element_type=F32)
        parts.append(b3[:, :HG_DK] + b3[:, HG_DK:2 * HG_DK] + b3[:, 2 * HG_DK:])
    b = jnp.concatenate(parts, axis=0)

    def rows_of(r):
        return jnp.concatenate([jnp.broadcast_to(b[n * c + r:n * c + r + 1, :], (c, HG_DK))
                                for n in range(chunks)], axis=0)

    b_mid = rows_of(c // 2 - 1)
    b_last = rows_of(c - 1)
    q_in = (q * jnp.exp(b - b_mid)).astype(BF16)
    k_in = (k * jnp.exp(b_mid - b)).astype(BF16)
    k_dec = (k * jnp.exp(b_last - b)).astype(BF16)
    qe = (q * jnp.exp(b)).astype(BF16)

    o_intra, dst = [], []
    for n in range(chunks):
        sl = slice(n * c, (n + 1) * c)
        a = lax.dot_general(q_in[sl], k_in[sl], _NT, preferred_element_type=F32)
        a = jnp.where(causal, a, 0.0).astype(BF16)
        o_intra.append(jnp.dot(a, vb[sl], preferred_element_type=F32))
        dst.append(lax.dot_general(vb[sl], k_dec[sl], _TN, preferred_element_type=F32))

    st = st_ref[...]
    outs = []
    for n in range(chunks):
        sl = slice(n * c, (n + 1) * c)
        outs.append(o_intra[n] + lax.dot_general(qe[sl], st.astype(BF16), _NT, preferred_element_type=F32))
        st = jnp.exp(b[(n + 1) * c - 1:(n + 1) * c, :]) * st + dst[n]
    st_ref[...] = st

    o = jnp.concatenate(outs, axis=0)
    og = og_ref[...]
    o = o * lax.rsqrt(jnp.mean(o * o, axis=-1, keepdims=True) + NORM_EPS) * gain_ref[...]
    o_ref[...] = (o * (og * jax.nn.sigmoid(og))).astype(o_ref.dtype)


def _hgrn(proj, lb_logits, gain, layer, tt=512):
    n = proj.shape[0]
    h = HG_HEADS
    nl = lb_logits.shape[0]
    gr = HG_GROUP * HG_CHUNK
    idx = np.arange(gr)
    tri = ((idx[:, None] >= idx[None, :]) & (idx[:, None] // HG_CHUNK == idx[None, :] // HG_CHUNK))
    blk = lambda g: pl.BlockSpec((tt, HG_DK), lambda hh, t, g=g: (t, g * h + hh))
    return pl.pallas_call(
        functools.partial(_hgrn_kernel, layer=layer, chunks=tt // HG_CHUNK),
        grid=(h, n // tt),
        in_specs=[blk(0), blk(1), blk(2), blk(3),
                  pl.BlockSpec((nl, HG_DK), lambda hh, t: (0, hh)),
                  pl.BlockSpec((1, HG_DV), lambda hh, t: (0, hh)),
                  pl.BlockSpec((gr, gr), lambda hh, t: (0, 0))],
        out_specs=pl.BlockSpec((tt, HG_DV), lambda hh, t: (t, hh)),
        out_shape=jax.ShapeDtypeStruct((n, h * HG_DV), BF16),
        scratch_shapes=[pltpu.VMEM((HG_DV, HG_DK), F32)],
        compiler_params=_params("parallel", "arbitrary"),
        name="hgrn2",
    )(proj, proj, proj, proj, lb_logits, gain.reshape(1, -1), jnp.asarray(tri, BF16))


def _mla_prep_kernel(p_ref, pk_ref, pos_ref, fr_ref, qn_ref_g, kvn_ref_g, wq_ref, wk_ref, wvt_ref,
                     qn_ref, qr_ref, kn_ref, vt_ref, kr_ref, *, scale):
    hw = MLA_HEADS * MLA_NOPE
    half = MLA_ROPE // 2
    ang = pos_ref[...].astype(F32) * fr_ref[...]
    lane = lax.broadcasted_iota(jnp.int32, ang.shape, 1)
    cosv = jnp.where(lane < MLA_ROPE, jnp.cos(ang), 0.0)
    sinv = jnp.where(lane < MLA_ROPE, jnp.sin(ang), 0.0)

    def rope(y):
        return y * cosv + pltpu.roll(y, MLA_ROPE, 1) * sinv

    cq = _rms(p_ref[:, :MLA_Q_LORA], qn_ref_g[...]).astype(BF16)
    qres = jnp.dot(cq, wq_ref[...], preferred_element_type=F32)
    qn_ref[...] = (qres[:, :hw] * scale).astype(qn_ref.dtype)
    for h in range(MLA_HEADS):
        y = qres[:, hw + h * LANES: hw + (h + 1) * LANES]
        qr_ref[:, h * LANES:(h + 1) * LANES] = (rope(y) * scale).astype(qr_ref.dtype)
    ckv = _rms(p_ref[:, MLA_Q_LORA:MLA_Q_LORA + MLA_KV_LORA], kvn_ref_g[...]).astype(BF16)
    kn_ref[...] = jnp.dot(ckv, wk_ref[...], preferred_element_type=F32).astype(kn_ref.dtype)
    vt_ref[...] = lax.dot_general(wvt_ref[...], ckv, _NT, preferred_element_type=F32).astype(vt_ref.dtype)
    y = pk_ref[...]
    rot = jnp.where(lane < half, -pltpu.roll(y, LANES - half, 1), pltpu.roll(y, half, 1))
    kr_ref[...] = (y * cosv + rot * sinv).astype(kr_ref.dtype)


def _mla_prep(proj, positions, q_norm, kv_norm, wq, wk, wvt, tm=512):
    n = proj.shape[0]
    pw = MLA_Q_LORA + MLA_KV_LORA
    lat_blk = 4 * HG_WIDTH // pw
    kr_blk = (4 * HG_WIDTH + pw) // LANES
    hw = MLA_HEADS * MLA_NOPE
    half = MLA_ROPE // 2
    freqs = ROPE_THETA ** (-jnp.arange(half, dtype=F32) / half)
    fr = jnp.concatenate([freqs, freqs, jnp.zeros((LANES - MLA_ROPE,), F32)]).reshape(1, LANES)
    scale = float(1.0 / np.sqrt(MLA_QK).astype(np.float32)) * float(np.log2(np.e))
    full = lambda shape: pl.BlockSpec(shape, lambda i: (0, 0))
    tok = lambda w: pl.BlockSpec((tm, w), lambda i: (i, 0))
    return pl.pallas_call(
        functools.partial(_mla_prep_kernel, scale=scale),
        grid=(n // tm,),
        in_specs=[pl.BlockSpec((tm, pw), lambda i: (i, lat_blk)), pl.BlockSpec((tm, LANES), lambda i: (i, kr_blk)),
                  tok(1), full((1, LANES)), full((1, MLA_Q_LORA)), full((1, MLA_KV_LORA)),
                  full(wq.shape), full(wk.shape), full(wvt.shape)],
        out_specs=[tok(hw), tok(MLA_HEADS * LANES), tok(hw),
                   pl.BlockSpec((MLA_HEADS * MLA_V, tm), lambda i: (0, i)), tok(LANES)],
        out_shape=[jax.ShapeDtypeStruct((n, hw), BF16),
                   jax.ShapeDtypeStruct((n, MLA_HEADS * LANES), BF16),
                   jax.ShapeDtypeStruct((n, hw), BF16),
                   jax.ShapeDtypeStruct((MLA_HEADS * MLA_V, n), BF16),
                   jax.ShapeDtypeStruct((n, LANES), BF16)],
        compiler_params=_params("parallel"),
        name="mla_prep",
    )(proj, proj, positions.reshape(n, 1), fr, q_norm.reshape(1, -1), kv_norm.reshape(1, -1), wq, wk, wvt)


QSUB = 256


def _attn_kernel(qi_ref, kj_ref, qn_ref, qr_ref, kn_ref, kr_ref, vt_ref, o_ref, m_ref, l_ref, acc_ref, *, blk, hpb):
    s_id = pl.program_id(1)
    qi = qi_ref[s_id]
    kj = kj_ref[s_id]

    @pl.when(kj == 0)
    def _():
        m_ref[...] = jnp.full_like(m_ref, -jnp.inf)
        l_ref[...] = jnp.zeros_like(l_ref)
        acc_ref[...] = jnp.zeros_like(acc_ref)

    def update(masked):
        nsub = blk // QSUB
        items = [(hh, c) for hh in range(hpb) for c in range(nsub)]

        def nkeys(c):
            return (c + 1) * QSUB if masked else blk

        def scores(item):
            hh, c = item
            hl = slice(hh * LANES, (hh + 1) * LANES)
            sl = pl.ds(c * QSUB, QSUB)
            k = jnp.concatenate([kn_ref[0:nkeys(c), hl], kr_ref[0:nkeys(c), :]], axis=1)
            q = jnp.concatenate([qn_ref[sl, hl], qr_ref[sl, hl]], axis=1)
            return lax.dot_general(k, q, _NT, preferred_element_type=F32)

        sts = [scores(it) for it in items[:2]]
        for n, (hh, c) in enumerate(items):
            sl = pl.ds(c * QSUB, QSUB)
            st = sts[n]
            if n + 2 < len(items):
                sts.append(scores(items[n + 2]))
            if masked:
                key = lax.broadcasted_iota(jnp.int32, st.shape, 0)
                qry = lax.broadcasted_iota(jnp.int32, st.shape, 1) + c * QSUB
                st = jnp.where(key <= qry, st, MASK_VALUE)
            m_prev = m_ref[hh, :, sl]
            m_new = jnp.maximum(m_prev, jnp.max(st, axis=0, keepdims=True))
            alpha = jnp.exp2(m_prev - m_new)
            p = jnp.exp2(st - m_new)
            l_ref[hh, :, sl] = alpha * l_ref[hh, :, sl] + jnp.sum(p, axis=0, keepdims=True)
            pv = jnp.dot(vt_ref[hh * MLA_V:(hh + 1) * MLA_V, 0:nkeys(c)], p.astype(BF16),
                         preferred_element_type=F32)
            acc_ref[hh, :, sl] = alpha * acc_ref[hh, :, sl] + pv
            m_ref[hh, :, sl] = m_new

    @pl.when(kj < qi)
    def _():
        update(False)

    @pl.when(kj == qi)
    def _():
        update(True)
        for hh in range(hpb):
            o_ref[:, hh * MLA_V:(hh + 1) * MLA_V] = (acc_ref[hh] / l_ref[hh]).T.astype(o_ref.dtype)


def _attention(qn, qr, kn, kr, vt, blk=1024, hpb=2):
    n = qn.shape[0]
    nb = n // blk
    qi = np.array([i for i in range(nb) for j in range(i + 1)], np.int32)
    kj = np.array([j for i in range(nb) for j in range(i + 1)], np.int32)
    qspec = pl.BlockSpec((blk, hpb * LANES), lambda h, s, qi, kj: (qi[s], h))
    kspec = pl.BlockSpec((blk, hpb * LANES), lambda h, s, qi, kj: (kj[s], h))
    krspec = pl.BlockSpec((blk, LANES), lambda h, s, qi, kj: (kj[s], 0))
    vtspec = pl.BlockSpec((hpb * MLA_V, blk), lambda h, s, qi, kj: (h, kj[s]))
    grid_spec = pltpu.PrefetchScalarGridSpec(
        num_scalar_prefetch=2,
        grid=(MLA_HEADS // hpb, len(qi)),
        in_specs=[qspec, qspec, kspec, krspec, vtspec],
        out_specs=qspec,
        scratch_shapes=[pltpu.VMEM((hpb, 1, blk), F32), pltpu.VMEM((hpb, 1, blk), F32),
                        pltpu.VMEM((hpb, MLA_V, blk), F32)],
    )
    return pl.pallas_call(
        functools.partial(_attn_kernel, blk=blk, hpb=hpb),
        grid_spec=grid_spec,
        out_shape=jax.ShapeDtypeStruct((n, MLA_HEADS * MLA_V), BF16),
        compiler_params=_params("parallel", "arbitrary"),
        name="mla_attention",
    )(jnp.asarray(qi), jnp.asarray(kj), qn, qr, kn, kr, vt)


def _merge_kernel(oa_ref, ob_ref, g0_ref, g1_ref, g2_ref, g3_ref, gt_ref, x_ref, wa_ref, wb_ref, wo_ref,
                  g_ref, x1_ref, hn_ref, *, off):
    d = x_ref.shape[1]
    ya = jnp.dot(oa_ref[...], wa_ref[...], preferred_element_type=F32)
    yb = jnp.dot(ob_ref[...], wb_ref[...], preferred_element_type=F32)
    gates = jnp.concatenate([g0_ref[...], g1_ref[...], g2_ref[...], g3_ref[...], gt_ref[...]], axis=1)
    ga = gates[:, off:off + d]
    gb = gates[:, off + d:off + 2 * d]
    y = jax.nn.sigmoid(ga) * ya + jax.nn.sigmoid(gb) * yb
    x1 = x_ref[...] + jnp.dot(y.astype(BF16), wo_ref[...], preferred_element_type=F32)
    x1_ref[...] = x1
    hn_ref[...] = _rms(x1, g_ref[...]).astype(hn_ref.dtype)


def _merge(oa, ob, proj, tail, g0, x, wa, wb, wo, gain, tm=256):
    n, d = x.shape
    gw = d // 2
    base = (g0 // gw) * gw
    assert base + 4 * gw == proj.shape[1] and g0 + 2 * d <= base + 4 * gw + LANES
    tok = lambda w, c=0: pl.BlockSpec((tm, w), lambda i, c=c: (i, c))
    res = lambda shape: pl.BlockSpec(shape, lambda i: (0, 0), pipeline_mode=pl.Buffered(1))
    return pl.pallas_call(
        functools.partial(_merge_kernel, off=g0 - base),
        grid=(n // tm,),
        in_specs=[tok(oa.shape[1]), tok(ob.shape[1])] + [tok(gw, base // gw + c) for c in range(4)]
                 + [tok(LANES), tok(d), res(wa.shape), res(wb.shape), res(wo.shape), res((1, d))],
        out_specs=[tok(d), tok(d)],
        out_shape=[jax.ShapeDtypeStruct((n, d), F32), jax.ShapeDtypeStruct((n, d), BF16)],
        compiler_params=_params("parallel"),
        name="merge_outproj",
    )(oa, ob, proj, proj, proj, proj, tail, x, wa, wb, wo, gain.reshape(1, d))


R_TM = 1024
R_BLOCKS = R_TM // LANES
R_PITCH = PEER_NKEYS + 8


def _cand_pairs():
    k = PEER_TOPK
    return [(a, b) for a in range(k) for b in range(k) if (a + 1) * (b + 1) <= k]


def _first_max(vals, *payloads):
    payloads = [list(p) for p in payloads]
    while len(vals) > 1:
        nv, npl = [], [[] for _ in payloads]
        for i in range(0, len(vals) - 1, 2):
            take = vals[i] >= vals[i + 1]
            nv.append(jnp.maximum(vals[i], vals[i + 1]))
            for dst, p in zip(npl, payloads):
                dst.append(jnp.where(take, p[i], p[i + 1]))
        if len(vals) % 2:
            nv.append(vals[-1])
            for dst, p in zip(npl, payloads):
                dst.append(p[-1])
        vals, payloads = nv, npl
    return (vals[0], *[p[0] for p in payloads])


def _top_keys(x, k):
    x = list(x)
    ids = [float(i) for i in range(len(x))]
    vals, idxs = [], []
    for _ in range(k):
        m, am = _first_max(x, ids)
        vals.append(m)
        idxs.append(am)
        x = [jnp.where(am == i, -jnp.inf, xi) for i, xi in zip(ids, x)]
    return vals, idxs


def _split_bf16(x):
    hi = x.astype(BF16)
    return hi, (x - hi.astype(F32)).astype(BF16)


def _route_kernel(h_ref, wq_ref, k1_ref, k2_ref, e_ref, g_ref, s1_ref, s2_ref, *, hps):
    k = PEER_TOPK
    half = PEER_QDIM // 2
    h = h_ref[...]
    pairs = _cand_pairs()

    def scores(hh):
        qt = lax.dot_general(wq_ref[hh * PEER_QDIM:(hh + 1) * PEER_QDIM, :], h, _NT, preferred_element_type=F32)
        out = []
        for part, k_ref in ((0, k1_ref), (1, k2_ref)):
            q_hi, q_lo = _split_bf16(qt[part * half:(part + 1) * half])
            qs = jnp.concatenate([q_hi, q_lo, q_hi], axis=0)
            out.append(jnp.dot(k_ref[...], qs, preferred_element_type=F32))
        return out

    def by_key(s, s_ref):
        for j in range(R_BLOCKS):
            s_ref[j * R_PITCH:j * R_PITCH + PEER_NKEYS, :] = s[:, j * LANES:(j + 1) * LANES]
        return [s_ref[pl.ds(key, R_BLOCKS, stride=R_PITCH), :] for key in range(PEER_NKEYS)]

    nxt = scores(0)
    for hh in range(hps):
        s1, s2 = nxt
        if hh + 1 < hps:
            nxt = scores(hh + 1)
        v1, i1 = _top_keys(by_key(s1, s1_ref), k)
        v2, i2 = _top_keys(by_key(s2, s2_ref), k)
        cand = [v1[a] + v2[b] for a, b in pairs]
        ec = [i1[a] * float(PEER_NKEYS) + i2[b] for a, b in pairs]
        ci = [float(a * k + b) for a, b in pairs]
        cv, ce = [], []
        for _ in range(k):
            m, sel, e = _first_max(cand, ci, ec)
            cv.append(m)
            ce.append(e)
            cand = [jnp.where(sel == c, -jnp.inf, x) for c, x in zip(ci, cand)]
        ex = [jnp.exp(c - cv[0]) for c in cv]
        den = ex[0]
        for x in ex[1:]:
            den = den + x
        for j in range(k):
            g_ref[0, hh * k + j] = ex[j] / den
            e_ref[0, hh * k + j] = ce[j].astype(jnp.int32)


def _route(hn, wq_t, k1, k2, hps=2):
    n, d = hn.shape
    k = PEER_TOPK
    tiles = n // R_TM

    def stack(kk):
        hi, lo = _split_bf16(kk)
        return jnp.concatenate([hi, hi, lo], axis=1)

    full = lambda shape: pl.BlockSpec(shape, lambda i, g: (0, 0))
    out = pl.BlockSpec((1, hps * k, R_BLOCKS, LANES), lambda i, g: (i, g, 0, 0))
    sc = lambda: pltpu.VMEM((R_BLOCKS * R_PITCH, LANES), F32)
    return pl.pallas_call(
        functools.partial(_route_kernel, hps=hps),
        grid=(tiles, PEER_HEADS // hps),
        in_specs=[pl.BlockSpec((R_TM, d), lambda i, g: (i, 0)),
                  pl.BlockSpec((hps * PEER_QDIM, d), lambda i, g: (g, 0)),
                  full((PEER_NKEYS, 3 * (PEER_QDIM // 2))), full((PEER_NKEYS, 3 * (PEER_QDIM // 2)))],
        out_specs=[out, out],
        out_shape=[jax.ShapeDtypeStruct((tiles, PEER_HEADS * k, R_BLOCKS, LANES), jnp.int32),
                   jax.ShapeDtypeStruct((tiles, PEER_HEADS * k, R_BLOCKS, LANES), F32)],
        scratch_shapes=[sc(), sc()],
        compiler_params=_params("parallel", "arbitrary"),
        name="peer_route",
    )(hn, wq_t, stack(k1), stack(k2))


W_TILE = 128
W_PITCH = W_TILE + 8


def _wbuild_kernel(e_ref, g_ref, w_ref):
    nk = PEER_NKEYS
    iota = lax.broadcasted_iota(jnp.int32, (nk, LANES), 0)
    for i in range(nk):
        w_ref[i * W_PITCH + W_TILE:(i + 1) * W_PITCH, :] = jnp.zeros((W_PITCH - W_TILE, nk), F32)

    def body(t, carry):
        e = e_ref[pl.ds(t, 1), :]
        g = g_ref[pl.ds(t, 1), :]
        p1 = jnp.where(iota == (e >> 7), g, 0.0).astype(BF16)
        p2 = jnp.where(iota == (e & (nk - 1)), 1.0, 0.0).astype(BF16)
        wt = lax.dot_general(p1, p2, _NT, preferred_element_type=F32)
        w_ref[pl.ds(t, nk, stride=W_PITCH), :] = wt
        return carry

    lax.fori_loop(0, W_TILE, body, 0, unroll=32)


def _wbuild(e, g):
    n, pairs = e.shape
    nk = PEER_NKEYS
    tiles = n // W_TILE
    w = pl.pallas_call(
        _wbuild_kernel,
        grid=(tiles,),
        in_specs=[pl.BlockSpec((W_TILE, pairs), lambda i: (i, 0)), pl.BlockSpec((W_TILE, pairs), lambda i: (i, 0))],
        out_specs=pl.BlockSpec((nk * W_PITCH, nk), lambda i: (i, 0)),
        out_shape=jax.ShapeDtypeStruct((tiles * nk * W_PITCH, nk), F32),
        compiler_params=_params("parallel"),
        name="peer_wbuild",
    )(e, g)
    return w.reshape(tiles, nk, W_PITCH, nk)


def _peer_kernel(h_ref, u_ref, v_ref, w_ref, o_ref, *, groups, tiles):
    @pl.when(pl.program_id(1) == 0)
    def _():
        o_ref[...] = jnp.zeros_like(o_ref)

    z = lax.dot_general(h_ref[...], u_ref[...].astype(BF16), _NT, preferred_element_type=F32)
    rows = []
    for k in range(tiles):
        cols = []
        for i in range(groups):
            zi = z[k * W_TILE:(k + 1) * W_TILE, i * LANES:(i + 1) * LANES]
            gelu = 0.5 * zi * (1.0 + lax.erf(zi * np.float32(np.sqrt(0.5))))
            cols.append((w_ref[k, i, 0:W_TILE, :] * gelu).astype(BF16))
        rows.append(jnp.concatenate(cols, axis=1))
    a = jnp.concatenate(rows, axis=0)
    o_ref[...] += jnp.dot(a, v_ref[...].astype(BF16), preferred_element_type=F32)


def _peer(hn, u, v, w, tm=1024, te=512):
    n, d = hn.shape
    ne = u.shape[0]
    groups = te // PEER_NKEYS
    tiles = tm // W_TILE
    return pl.pallas_call(
        functools.partial(_peer_kernel, groups=groups, tiles=tiles),
        grid=(n // tm, ne // te),
        in_specs=[pl.BlockSpec((tm, d), lambda i, c: (i, 0)),
                  pl.BlockSpec((te, d), lambda i, c: (c, 0)),
                  pl.BlockSpec((te, d), lambda i, c: (c, 0)),
                  pl.BlockSpec((tiles, groups, W_PITCH, PEER_NKEYS), lambda i, c: (i, c, 0, 0))],
        out_specs=pl.BlockSpec((tm, d), lambda i, c: (i, 0)),
        out_shape=jax.ShapeDtypeStruct((n, d), F32),
        compiler_params=_params("parallel", "arbitrary"),
        name="peer_experts",
    )(hn, u, v, w)


def _ple_kernel(x1_ref, po_ref, p_ref, wpg_ref, wpe_ref, gp_ref, gf_ref, o_ref):
    x2 = x1_ref[...] + po_ref[...]
    hp = _rms(x2, gp_ref[...]).astype(BF16)
    gate = jax.nn.sigmoid(jnp.dot(hp, wpg_ref[...], preferred_element_type=F32))
    pe = jnp.dot(p_ref[...], wpe_ref[...], preferred_element_type=F32)
    x3 = x2 + gate * pe
    o_ref[...] = _rms(x3, gf_ref[...]).astype(o_ref.dtype)


def _ple(x1, po, p, wpg, wpe, g_ple, g_final, tm=256):
    n, d = x1.shape
    tok = lambda w: pl.BlockSpec((tm, w), lambda i: (i, 0))
    res = lambda shape: pl.BlockSpec(shape, lambda i: (0, 0), pipeline_mode=pl.Buffered(1))
    return pl.pallas_call(
        _ple_kernel,
        grid=(n // tm,),
        in_specs=[tok(d), tok(d), tok(p.shape[1]), res(wpg.shape), res(wpe.shape), res((1, d)), res((1, d))],
        out_specs=tok(d),
        out_shape=jax.ShapeDtypeStruct((n, d), F32),
        compiler_params=_params("parallel"),
        name="ple_final",
    )(x1, po, p, wpg, wpe, g_ple.reshape(1, d), g_final.reshape(1, d))


def _layer(x, p, positions, layer, norm_mix, w_in, lb_logits, hg_norm, mla_q_norm, mla_kv_norm,
           w_uq, w_ukv, w_a, w_b, w_o, norm_ffn, peer_wq, peer_k1, peer_k2, peer_u, peer_v,
           norm_ple, w_pg, w_pe, norm_out):
    n, d = x.shape
    hgw = 4 * HG_WIDTH
    mla0 = hgw
    kr0 = mla0 + MLA_Q_LORA + MLA_KV_LORA
    g0 = kr0 + MLA_ROPE
    half = MLA_ROPE // 2

    uq = w_uq.reshape(MLA_Q_LORA, MLA_HEADS, MLA_QK)
    uq_r = uq[:, :, MLA_NOPE:]
    uq_rot = jnp.concatenate([-uq_r[:, :, half:], uq_r[:, :, :half]], axis=2)
    wq = jnp.concatenate([uq[:, :, :MLA_NOPE].reshape(MLA_Q_LORA, -1),
                          jnp.concatenate([uq_r, uq_rot], axis=2).reshape(MLA_Q_LORA, -1)],
                         axis=1).astype(BF16)
    ukv = w_ukv.reshape(MLA_KV_LORA, MLA_HEADS, MLA_NOPE + MLA_V)
    wk = ukv[:, :, :MLA_NOPE].reshape(MLA_KV_LORA, -1).astype(BF16)
    wvt = ukv[:, :, MLA_NOPE:].reshape(MLA_KV_LORA, -1).T.astype(BF16)

    tn = 1024
    w_in_t = w_in.T
    main_cols = (w_in_t.shape[0] // tn) * tn
    w_tail_t = w_in_t[main_cols:]
    w_tail_t = jnp.pad(w_tail_t, ((0, LANES - w_tail_t.shape[0]), (0, 0)))
    hn = _rmsnorm(x, norm_mix)
    proj = _matmul_ws(hn, w_in_t, main_cols, F32, 1024, tn, "in_proj_main")
    tail = _matmul_ws(hn, w_tail_t, LANES, F32, 1024, LANES, "in_proj_tail")

    o_a = _hgrn(proj, lb_logits, hg_norm, layer)
    qn, qr, kn, vt, kr = _mla_prep(proj, positions, mla_q_norm, mla_kv_norm, wq, wk, wvt)
    o_b = _attention(qn, qr, kn, kr, vt)
    x1, hn2 = _merge(o_a, o_b, proj, tail, g0, x, w_a.astype(BF16), w_b.astype(BF16), w_o.astype(BF16),
                     norm_ffn)

    e4, g4 = _route(hn2, peer_wq.T.astype(BF16), peer_k1, peer_k2)
    by_token = lambda a: a.transpose(0, 2, 3, 1).reshape(n, a.shape[1])
    w = _wbuild(by_token(e4), by_token(g4))
    po = _peer(hn2, peer_u, peer_v, w)
    return _ple(x1, po, p.astype(BF16), w_pg.astype(BF16), w_pe.astype(BF16), norm_ple, norm_out)


def kernel(x, p, positions, norm_mix, w_in, lb_logits, hg_norm, mla_q_norm, mla_kv_norm, w_uq, w_ukv,
           w_a, w_b, w_o, norm_ffn, peer_wq, peer_k1, peer_k2, peer_u, peer_v, norm_ple, w_pg, w_pe,
           norm_final):
    b, t, d = x.shape
    depth = w_in.shape[0]
    assert b == 1 and depth == 1, "kernel supports one sequence and one layer"
    out = _layer(x.reshape(b * t, d), p[0].reshape(b * t, -1), positions.reshape(-1), 0,
                 norm_mix[0], w_in[0], lb_logits, hg_norm[0], mla_q_norm[0], mla_kv_norm[0],
                 w_uq[0], w_ukv[0], w_a[0], w_b[0], w_o[0], norm_ffn[0], peer_wq[0], peer_k1[0],
                 peer_k2[0], peer_u[0], peer_v[0], norm_ple[0], w_pg[0], w_pe[0], norm_final)
    return out.reshape(b, t, d)
```

```python
import functools

import jax
import jax.numpy as jnp
import numpy as np
from jax import lax
from jax.experimental import pallas as pl
from jax.experimental.pallas import tpu as pltpu

D_MODEL = 2048
HG_HEADS = 8
HG_DK = 128
HG_DV = 128
HG_WIDTH = HG_HEADS * HG_DK
HG_CHUNK = 64
MLA_HEADS = 8
MLA_NOPE = 128
MLA_ROPE = 64
MLA_V = 128
MLA_Q_LORA = 512
MLA_KV_LORA = 512
MLA_QK = MLA_NOPE + MLA_ROPE
ROPE_THETA = 10000.0
PEER_HEADS = 8
PEER_NKEYS = 128
PEER_QDIM = 256
PEER_TOPK = 16
NORM_EPS = 1e-6
MASK_VALUE = -1e30

LANES = 128
VMEM_LIMIT = 56 * 1024 * 1024

_NT = (((1,), (1,)), ((), ()))
_TN = (((0,), (0,)), ((), ()))

F32 = jnp.float32
BF16 = jnp.bfloat16


def _params(*sem):
    return pltpu.CompilerParams(dimension_semantics=sem, vmem_limit_bytes=VMEM_LIMIT)


def _rms(x, gain):
    return x * lax.rsqrt(jnp.mean(x * x, axis=-1, keepdims=True) + NORM_EPS) * gain


def _rmsnorm_kernel(x_ref, g_ref, o_ref):
    o_ref[...] = _rms(x_ref[...], g_ref[...]).astype(o_ref.dtype)


def _rmsnorm(x, gain, tm=512):
    n, d = x.shape
    return pl.pallas_call(
        _rmsnorm_kernel,
        grid=(n // tm,),
        in_specs=[pl.BlockSpec((tm, d), lambda i: (i, 0)), pl.BlockSpec((1, d), lambda i: (0, 0))],
        out_specs=pl.BlockSpec((tm, d), lambda i: (i, 0)),
        out_shape=jax.ShapeDtypeStruct((n, d), BF16),
        compiler_params=_params("parallel"),
        name="rmsnorm",
    )(x, gain.reshape(1, d))


def _matmul_ws_kernel(a_ref, w_ref, o_ref, wb_ref):
    @pl.when(pl.program_id(1) == 0)
    def _():
        wb_ref[...] = w_ref[...].astype(BF16)

    o_ref[...] = lax.dot_general(a_ref[...], wb_ref[...], _NT, preferred_element_type=F32).astype(o_ref.dtype)


def _matmul_ws(a, w_t, cols, out_dtype, tm, tn, name):
    m, k = a.shape
    return pl.pallas_call(
        _matmul_ws_kernel,
        grid=(cols // tn, m // tm),
        in_specs=[pl.BlockSpec((tm, k), lambda j, i: (i, 0)), pl.BlockSpec((tn, k), lambda j, i: (j, 0))],
        out_specs=pl.BlockSpec((tm, tn), lambda j, i: (i, j)),
        out_shape=jax.ShapeDtypeStruct((m, cols), out_dtype),
        scratch_shapes=[pltpu.VMEM((tn, k), BF16)],
        compiler_params=_params("arbitrary", "arbitrary"),
        name=name,
    )(a, w_t)


HG_GROUP = 4


def _hgrn_kernel(q_ref, f_ref, i_ref, og_ref, lbl_ref, gain_ref, tri_ref, o_ref, st_ref, *, layer, chunks):
    @pl.when(pl.program_id(1) == 0)
    def _():
        st_ref[...] = jnp.zeros_like(st_ref)

    lbl = lbl_ref[...]
    e = jnp.exp(lbl - jnp.max(lbl, axis=0, keepdims=True))
    sm = e / jnp.sum(e, axis=0, keepdims=True)
    lb = jnp.sum(sm[: layer + 1], axis=0, keepdims=True)
    c = HG_CHUNK
    row = lax.broadcasted_iota(jnp.int32, (c, c), 0)
    col = lax.broadcasted_iota(jnp.int32, (c, c), 1)
    causal = row >= col

    q = q_ref[...]
    vb = i_ref[...].astype(BF16)
    f = lb + (1.0 - lb) * jax.nn.sigmoid(f_ref[...])
    g = jnp.log(f)
    k = 1.0 - f
    g_hi = g.astype(BF16)
    r1 = g - g_hi.astype(F32)
    g_mid = r1.astype(BF16)
    g_lo = (r1 - g_mid.astype(F32)).astype(BF16)
    gs = jnp.concatenate([g_hi, g_mid, g_lo], axis=1)
    gr = HG_GROUP * c
    parts = []
    for n in range(chunks // HG_GROUP):
        b3 = jnp.dot(tri_ref[...], gs[n * gr:(n + 1) * gr, :], preferred_element_type=F32)
        parts.append(b3[:, :HG_DK] + b3[:, HG_DK:2 * HG_DK] + b3[:, 2 * HG_DK:])
    b = jnp.concatenate(parts, axis=0)

    def rows_of(r):
        return jnp.concatenate([jnp.broadcast_to(b[n * c + r:n * c + r + 1, :], (c, HG_DK))
                                for n in range(chunks)], axis=0)

    b_mid = rows_of(c // 2 - 1)
    b_last = rows_of(c - 1)
    q_in = (q * jnp.exp(b - b_mid)).astype(BF16)
    k_in = (k * jnp.exp(b_mid - b)).astype(BF16)
    k_dec = (k * jnp.exp(b_last - b)).astype(BF16)
    qe = (q * jnp.exp(b)).astype(BF16)

    o_intra, dst = [], []
    for n in range(chunks):
        sl = slice(n * c, (n + 1) * c)
        a = lax.dot_general(q_in[sl], k_in[sl], _NT, preferred_element_type=F32)
        a = jnp.where(causal, a, 0.0).astype(BF16)
        o_intra.append(jnp.dot(a, vb[sl], preferred_element_type=F32))
        dst.append(lax.dot_general(vb[sl], k_dec[sl], _TN, preferred_element_type=F32))

    st = st_ref[...]
    outs = []
    for n in range(chunks):
        sl = slice(n * c, (n + 1) * c)
        outs.append(o_intra[n] + lax.dot_general(qe[sl], st.astype(BF16), _NT, preferred_element_type=F32))
        st = jnp.exp(b[(n + 1) * c - 1:(n + 1) * c, :]) * st + dst[n]
    st_ref[...] = st

    o = jnp.concatenate(outs, axis=0)
    og = og_ref[...]
    o = o * lax.rsqrt(jnp.mean(o * o, axis=-1, keepdims=True) + NORM_EPS) * gain_ref[...]
    o_ref[...] = (o * (og * jax.nn.sigmoid(og))).astype(o_ref.dtype)


def _hgrn(proj, lb_logits, gain, layer, tt=1024):
    n = proj.shape[0]
    h = HG_HEADS
    nl = lb_logits.shape[0]
    gr = HG_GROUP * HG_CHUNK
    idx = np.arange(gr)
    tri = ((idx[:, None] >= idx[None, :]) & (idx[:, None] // HG_CHUNK == idx[None, :] // HG_CHUNK))
    blk = lambda g: pl.BlockSpec((tt, HG_DK), lambda hh, t, g=g: (t, g * h + hh))
    return pl.pallas_call(
        functools.partial(_hgrn_kernel, layer=layer, chunks=tt // HG_CHUNK),
        grid=(h, n // tt),
        in_specs=[blk(0), blk(1), blk(2), blk(3),
                  pl.BlockSpec((nl, HG_DK), lambda hh, t: (0, hh)),
                  pl.BlockSpec((1, HG_DV), lambda hh, t: (0, hh)),
                  pl.BlockSpec((gr, gr), lambda hh, t: (0, 0))],
        out_specs=pl.BlockSpec((tt, HG_DV), lambda hh, t: (t, hh)),
        out_shape=jax.ShapeDtypeStruct((n, h * HG_DV), BF16),
        scratch_shapes=[pltpu.VMEM((HG_DV, HG_DK), F32)],
        compiler_params=_params("parallel", "arbitrary"),
        name="hgrn2",
    )(proj, proj, proj, proj, lb_logits, gain.reshape(1, -1), jnp.asarray(tri, BF16))


def _mla_prep_kernel(p_ref, pk_ref, pos_ref, fr_ref, qn_ref_g, kvn_ref_g, wq_ref, wk_ref, wvt_ref,
                     qn_ref, qr_ref, kn_ref, vt_ref, kr_ref, *, scale):
    hw = MLA_HEADS * MLA_NOPE
    half = MLA_ROPE // 2
    ang = pos_ref[...].astype(F32) * fr_ref[...]
    lane = lax.broadcasted_iota(jnp.int32, ang.shape, 1)
    cosv = jnp.where(lane < MLA_ROPE, jnp.cos(ang), 0.0)
    sinv = jnp.where(lane < MLA_ROPE, jnp.sin(ang), 0.0)

    def rope(y):
        return y * cosv + pltpu.roll(y, MLA_ROPE, 1) * sinv

    cq = _rms(p_ref[:, :MLA_Q_LORA], qn_ref_g[...]).astype(BF16)
    qres = jnp.dot(cq, wq_ref[...], preferred_element_type=F32)
    qn_ref[...] = (qres[:, :hw] * scale).astype(qn_ref.dtype)
    for h in range(MLA_HEADS):
        y = qres[:, hw + h * LANES: hw + (h + 1) * LANES]
        qr_ref[:, h * LANES:(h + 1) * LANES] = (rope(y) * scale).astype(qr_ref.dtype)
    ckv = _rms(p_ref[:, MLA_Q_LORA:MLA_Q_LORA + MLA_KV_LORA], kvn_ref_g[...]).astype(BF16)
    kn_ref[...] = jnp.dot(ckv, wk_ref[...], preferred_element_type=F32).astype(kn_ref.dtype)
    vt_ref[...] = lax.dot_general(wvt_ref[...], ckv, _NT, preferred_element_type=F32).astype(vt_ref.dtype)
    y = pk_ref[...]
    rot = jnp.where(lane < half, -pltpu.roll(y, LANES - half, 1), pltpu.roll(y, half, 1))
    kr_ref[...] = (y * cosv + rot * sinv).astype(kr_ref.dtype)


def _mla_prep(proj, positions, q_norm, kv_norm, wq, wk, wvt, tm=512):
    n = proj.shape[0]
    pw = MLA_Q_LORA + MLA_KV_LORA
    lat_blk = 4 * HG_WIDTH // pw
    kr_blk = (4 * HG_WIDTH + pw) // LANES
    hw = MLA_HEADS * MLA_NOPE
    half = MLA_ROPE // 2
    freqs = ROPE_THETA ** (-jnp.arange(half, dtype=F32) / half)
    fr = jnp.concatenate([freqs, freqs, jnp.zeros((LANES - MLA_ROPE,), F32)]).reshape(1, LANES)
    scale = float(1.0 / np.sqrt(MLA_QK).astype(np.float32)) * float(np.log2(np.e))
    full = lambda shape: pl.BlockSpec(shape, lambda i: (0, 0))
    tok = lambda w: pl.BlockSpec((tm, w), lambda i: (i, 0))
    return pl.pallas_call(
        functools.partial(_mla_prep_kernel, scale=scale),
        grid=(n // tm,),
        in_specs=[pl.BlockSpec((tm, pw), lambda i: (i, lat_blk)), pl.BlockSpec((tm, LANES), lambda i: (i, kr_blk)),
                  tok(1), full((1, LANES)), full((1, MLA_Q_LORA)), full((1, MLA_KV_LORA)),
                  full(wq.shape), full(wk.shape), full(wvt.shape)],
        out_specs=[tok(hw), tok(MLA_HEADS * LANES), tok(hw),
                   pl.BlockSpec((MLA_HEADS * MLA_V, tm), lambda i: (0, i)), tok(LANES)],
        out_shape=[jax.ShapeDtypeStruct((n, hw), BF16),
                   jax.ShapeDtypeStruct((n, MLA_HEADS * LANES), BF16),
                   jax.ShapeDtypeStruct((n, hw), BF16),
                   jax.ShapeDtypeStruct((MLA_HEADS * MLA_V, n), BF16),
                   jax.ShapeDtypeStruct((n, LANES), BF16)],
        compiler_params=_params("parallel"),
        name="mla_prep",
    )(proj, proj, positions.reshape(n, 1), fr, q_norm.reshape(1, -1), kv_norm.reshape(1, -1), wq, wk, wvt)


QSUB = 256


def _attn_kernel(qi_ref, kj_ref, qn_ref, qr_ref, kn_ref, kr_ref, vt_ref, o_ref, m_ref, l_ref, acc_ref, *, blk, hpb):
    s_id = pl.program_id(1)
    qi = qi_ref[s_id]
    kj = kj_ref[s_id]

    @pl.when(kj == 0)
    def _():
        m_ref[...] = jnp.full_like(m_ref, -jnp.inf)
        l_ref[...] = jnp.zeros_like(l_ref)
        acc_ref[...] = jnp.zeros_like(acc_ref)

    def update(masked):
        nsub = blk // QSUB
        items = [(hh, c) for hh in range(hpb) for c in range(nsub)]

        def nkeys(c):
            return (c + 1) * QSUB if masked else blk

        def scores(item):
            hh, c = item
            hl = slice(hh * LANES, (hh + 1) * LANES)
            sl = pl.ds(c * QSUB, QSUB)
            k = jnp.concatenate([kn_ref[0:nkeys(c), hl], kr_ref[0:nkeys(c), :]], axis=1)
            q = jnp.concatenate([qn_ref[sl, hl], qr_ref[sl, hl]], axis=1)
            return lax.dot_general(k, q, _NT, preferred_element_type=F32)

        sts = [scores(it) for it in items[:2]]
        for n, (hh, c) in enumerate(items):
            sl = pl.ds(c * QSUB, QSUB)
            st = sts[n]
            if n + 2 < len(items):
                sts.append(scores(items[n + 2]))
            if masked:
                key = lax.broadcasted_iota(jnp.int32, st.shape, 0)
                qry = lax.broadcasted_iota(jnp.int32, st.shape, 1) + c * QSUB
                st = jnp.where(key <= qry, st, MASK_VALUE)
            m_prev = m_ref[hh, :, sl]
            m_new = jnp.maximum(m_prev, jnp.max(st, axis=0, keepdims=True))
            alpha = jnp.exp2(m_prev - m_new)
            p = jnp.exp2(st - m_new)
            l_ref[hh, :, sl] = alpha * l_ref[hh, :, sl] + jnp.sum(p, axis=0, keepdims=True)
            pv = jnp.dot(vt_ref[hh * MLA_V:(hh + 1) * MLA_V, 0:nkeys(c)], p.astype(BF16),
                         preferred_element_type=F32)
            acc_ref[hh, :, sl] = alpha * acc_ref[hh, :, sl] + pv
            m_ref[hh, :, sl] = m_new

    @pl.when(kj < qi)
    def _():
        update(False)

    @pl.when(kj == qi)
    def _():
        update(True)
        for hh in range(hpb):
            o_ref[:, hh * MLA_V:(hh + 1) * MLA_V] = (acc_ref[hh] / l_ref[hh]).T.astype(o_ref.dtype)


def _attention(qn, qr, kn, kr, vt, blk=1024, hpb=4):
    n = qn.shape[0]
    nb = n // blk
    qi = np.array([i for i in range(nb) for j in range(i + 1)], np.int32)
    kj = np.array([j for i in range(nb) for j in range(i + 1)], np.int32)
    qspec = pl.BlockSpec((blk, hpb * LANES), lambda h, s, qi, kj: (qi[s], h))
    kspec = pl.BlockSpec((blk, hpb * LANES), lambda h, s, qi, kj: (kj[s], h))
    krspec = pl.BlockSpec((blk, LANES), lambda h, s, qi, kj: (kj[s], 0))
    vtspec = pl.BlockSpec((hpb * MLA_V, blk), lambda h, s, qi, kj: (h, kj[s]))
    grid_spec = pltpu.PrefetchScalarGridSpec(
        num_scalar_prefetch=2,
        grid=(MLA_HEADS // hpb, len(qi)),
        in_specs=[qspec, qspec, kspec, krspec, vtspec],
        out_specs=qspec,
        scratch_shapes=[pltpu.VMEM((hpb, 1, blk), F32), pltpu.VMEM((hpb, 1, blk), F32),
                        pltpu.VMEM((hpb, MLA_V, blk), F32)],
    )
    return pl.pallas_call(
        functools.partial(_attn_kernel, blk=blk, hpb=hpb),
        grid_spec=grid_spec,
        out_shape=jax.ShapeDtypeStruct((n, MLA_HEADS * MLA_V), BF16),
        compiler_params=_params("parallel", "arbitrary"),
        name="mla_attention",
    )(jnp.asarray(qi), jnp.asarray(kj), qn, qr, kn, kr, vt)


def _merge_kernel(oa_ref, ob_ref, g0_ref, g1_ref, g2_ref, g3_ref, gt_ref, x_ref, wa_ref, wb_ref, wo_ref,
                  g_ref, x1_ref, hn_ref, *, off):
    d = x_ref.shape[1]
    ya = jnp.dot(oa_ref[...], wa_ref[...], preferred_element_type=F32)
    yb = jnp.dot(ob_ref[...], wb_ref[...], preferred_element_type=F32)
    gates = jnp.concatenate([g0_ref[...], g1_ref[...], g2_ref[...], g3_ref[...], gt_ref[...]], axis=1)
    ga = gates[:, off:off + d]
    gb = gates[:, off + d:off + 2 * d]
    y = jax.nn.sigmoid(ga) * ya + jax.nn.sigmoid(gb) * yb
    x1 = x_ref[...] + jnp.dot(y.astype(BF16), wo_ref[...], preferred_element_type=F32)
    x1_ref[...] = x1
    hn_ref[...] = _rms(x1, g_ref[...]).astype(hn_ref.dtype)


def _merge(oa, ob, proj, tail, g0, x, wa, wb, wo, gain, tm=256):
    n, d = x.shape
    gw = d // 2
    base = (g0 // gw) * gw
    assert base + 4 * gw == proj.shape[1] and g0 + 2 * d <= base + 4 * gw + LANES
    tok = lambda w, c=0: pl.BlockSpec((tm, w), lambda i, c=c: (i, c))
    res = lambda shape: pl.BlockSpec(shape, lambda i: (0, 0), pipeline_mode=pl.Buffered(1))
    return pl.pallas_call(
        functools.partial(_merge_kernel, off=g0 - base),
        grid=(n // tm,),
        in_specs=[tok(oa.shape[1]), tok(ob.shape[1])] + [tok(gw, base // gw + c) for c in range(4)]
                 + [tok(LANES), tok(d), res(wa.shape), res(wb.shape), res(wo.shape), res((1, d))],
        out_specs=[tok(d), tok(d)],
        out_shape=[jax.ShapeDtypeStruct((n, d), F32), jax.ShapeDtypeStruct((n, d), BF16)],
        compiler_params=_params("parallel"),
        name="merge_outproj",
    )(oa, ob, proj, proj, proj, proj, tail, x, wa, wb, wo, gain.reshape(1, d))


R_TM = 1024
R_BLOCKS = R_TM // LANES
R_PITCH = PEER_NKEYS + 8


def _cand_pairs():
    k = PEER_TOPK
    return [(a, b) for a in range(k) for b in range(k) if (a + 1) * (b + 1) <= k]


def _first_max(vals, *payloads):
    payloads = [list(p) for p in payloads]
    while len(vals) > 1:
        nv, npl = [], [[] for _ in payloads]
        for i in range(0, len(vals) - 1, 2):
            take = vals[i] >= vals[i + 1]
            nv.append(jnp.maximum(vals[i], vals[i + 1]))
            for dst, p in zip(npl, payloads):
                dst.append(jnp.where(take, p[i], p[i + 1]))
        if len(vals) % 2:
            nv.append(vals[-1])
            for dst, p in zip(npl, payloads):
                dst.append(p[-1])
        vals, payloads = nv, npl
    return (vals[0], *[p[0] for p in payloads])


def _top_keys(x, k):
    x = list(x)
    ids = [float(i) for i in range(len(x))]
    vals, idxs = [], []
    for _ in range(k):
        m, am = _first_max(x, ids)
        vals.append(m)
        idxs.append(am)
        x = [jnp.where(am == i, -jnp.inf, xi) for i, xi in zip(ids, x)]
    return vals, idxs


def _split_bf16(x):
    hi = x.astype(BF16)
    return hi, (x - hi.astype(F32)).astype(BF16)


def _route_kernel(h_ref, wq_ref, k1_ref, k2_ref, e_ref, g_ref, s1_ref, s2_ref, *, hps):
    k = PEER_TOPK
    half = PEER_QDIM // 2
    h = h_ref[...]
    pairs = _cand_pairs()

    def scores(hh):
        qt = lax.dot_general(wq_ref[hh * PEER_QDIM:(hh + 1) * PEER_QDIM, :], h, _NT, preferred_element_type=F32)
        out = []
        for part, k_ref in ((0, k1_ref), (1, k2_ref)):
            q_hi, q_lo = _split_bf16(qt[part * half:(part + 1) * half])
            qs = jnp.concatenate([q_hi, q_lo, q_hi], axis=0)
            out.append(jnp.dot(k_ref[...], qs, preferred_element_type=F32))
        return out

    def by_key(s, s_ref):
        for j in range(R_BLOCKS):
            s_ref[j * R_PITCH:j * R_PITCH + PEER_NKEYS, :] = s[:, j * LANES:(j + 1) * LANES]
        return [s_ref[pl.ds(key, R_BLOCKS, stride=R_PITCH), :] for key in range(PEER_NKEYS)]

    nxt = scores(0)
    for hh in range(hps):
        s1, s2 = nxt
        if hh + 1 < hps:
            nxt = scores(hh + 1)
        v1, i1 = _top_keys(by_key(s1, s1_ref), k)
        v2, i2 = _top_keys(by_key(s2, s2_ref), k)
        cand = [v1[a] + v2[b] for a, b in pairs]
        ec = [i1[a] * float(PEER_NKEYS) + i2[b] for a, b in pairs]
        ci = [float(a * k + b) for a, b in pairs]
        cv, ce = [], []
        for _ in range(k):
            m, sel, e = _first_max(cand, ci, ec)
            cv.append(m)
            ce.append(e)
            cand = [jnp.where(sel == c, -jnp.inf, x) for c, x in zip(ci, cand)]
        ex = [jnp.exp(c - cv[0]) for c in cv]
        den = ex[0]
        for x in ex[1:]:
            den = den + x
        for j in range(k):
            g_ref[0, hh * k + j] = ex[j] / den
            e_ref[0, hh * k + j] = ce[j].astype(jnp.int32)


def _route(hn, wq_t, k1, k2, hps=2):
    n, d = hn.shape
    k = PEER_TOPK
    tiles = n // R_TM

    def stack(kk):
        hi, lo = _split_bf16(kk)
        return jnp.concatenate([hi, hi, lo], axis=1)

    full = lambda shape: pl.BlockSpec(shape, lambda i, g: (0, 0))
    out = pl.BlockSpec((1, hps * k, R_BLOCKS, LANES), lambda i, g: (i, g, 0, 0))
    sc = lambda: pltpu.VMEM((R_BLOCKS * R_PITCH, LANES), F32)
    return pl.pallas_call(
        functools.partial(_route_kernel, hps=hps),
        grid=(tiles, PEER_HEADS // hps),
        in_specs=[pl.BlockSpec((R_TM, d), lambda i, g: (i, 0)),
                  pl.BlockSpec((hps * PEER_QDIM, d), lambda i, g: (g, 0)),
                  full((PEER_NKEYS, 3 * (PEER_QDIM // 2))), full((PEER_NKEYS, 3 * (PEER_QDIM // 2)))],
        out_specs=[out, out],
        out_shape=[jax.ShapeDtypeStruct((tiles, PEER_HEADS * k, R_BLOCKS, LANES), jnp.int32),
                   jax.ShapeDtypeStruct((tiles, PEER_HEADS * k, R_BLOCKS, LANES), F32)],
        scratch_shapes=[sc(), sc()],
        compiler_params=_params("parallel", "arbitrary"),
        name="peer_route",
    )(hn, wq_t, stack(k1), stack(k2))


W_TILE = 128
W_PITCH = W_TILE + 8


def _wbuild_kernel(e_ref, g_ref, w_ref):
    nk = PEER_NKEYS
    iota = lax.broadcasted_iota(jnp.int32, (nk, LANES), 0)
    for i in range(nk):
        w_ref[i * W_PITCH + W_TILE:(i + 1) * W_PITCH, :] = jnp.zeros((W_PITCH - W_TILE, nk), F32)

    def body(t, carry):
        e = e_ref[pl.ds(t, 1), :]
        g = g_ref[pl.ds(t, 1), :]
        p1 = jnp.where(iota == (e >> 7), g, 0.0).astype(BF16)
        p2 = jnp.where(iota == (e & (nk - 1)), 1.0, 0.0).astype(BF16)
        wt = lax.dot_general(p1, p2, _NT, preferred_element_type=F32)
        w_ref[pl.ds(t, nk, stride=W_PITCH), :] = wt
        return carry

    lax.fori_loop(0, W_TILE, body, 0, unroll=32)


def _wbuild(e, g):
    n, pairs = e.shape
    nk = PEER_NKEYS
    tiles = n // W_TILE
    w = pl.pallas_call(
        _wbuild_kernel,
        grid=(tiles,),
        in_specs=[pl.BlockSpec((W_TILE, pairs), lambda i: (i, 0)), pl.BlockSpec((W_TILE, pairs), lambda i: (i, 0))],
        out_specs=pl.BlockSpec((nk * W_PITCH, nk), lambda i: (i, 0)),
        out_shape=jax.ShapeDtypeStruct((tiles * nk * W_PITCH, nk), F32),
        compiler_params=_params("parallel"),
        name="peer_wbuild",
    )(e, g)
    return w.reshape(tiles, nk, W_PITCH, nk)


def _peer_kernel(h_ref, u_ref, v_ref, w_ref, o_ref, *, groups, tiles):
    @pl.when(pl.program_id(1) == 0)
    def _():
        o_ref[...] = jnp.zeros_like(o_ref)

    z = lax.dot_general(h_ref[...], u_ref[...].astype(BF16), _NT, preferred_element_type=F32)
    rows = []
    for k in range(tiles):
        cols = []
        for i in range(groups):
            zi = z[k * W_TILE:(k + 1) * W_TILE, i * LANES:(i + 1) * LANES]
            gelu = 0.5 * zi * (1.0 + lax.erf(zi * np.float32(np.sqrt(0.5))))
            cols.append((w_ref[k, i, 0:W_TILE, :] * gelu).astype(BF16))
        rows.append(jnp.concatenate(cols, axis=1))
    a = jnp.concatenate(rows, axis=0)
    o_ref[...] += jnp.dot(a, v_ref[...].astype(BF16), preferred_element_type=F32)


def _peer(hn, u, v, w, tm=1024, te=512):
    n, d = hn.shape
    ne = u.shape[0]
    groups = te // PEER_NKEYS
    tiles = tm // W_TILE
    return pl.pallas_call(
        functools.partial(_peer_kernel, groups=groups, tiles=tiles),
        grid=(n // tm, ne // te),
        in_specs=[pl.BlockSpec((tm, d), lambda i, c: (i, 0)),
                  pl.BlockSpec((te, d), lambda i, c: (c, 0)),
                  pl.BlockSpec((te, d), lambda i, c: (c, 0)),
                  pl.BlockSpec((tiles, groups, W_PITCH, PEER_NKEYS), lambda i, c: (i, c, 0, 0))],
        out_specs=pl.BlockSpec((tm, d), lambda i, c: (i, 0)),
        out_shape=jax.ShapeDtypeStruct((n, d), F32),
        compiler_params=_params("parallel", "arbitrary"),
        name="peer_experts",
    )(hn, u, v, w)


def _ple_kernel(x1_ref, po_ref, p_ref, wpg_ref, wpe_ref, gp_ref, gf_ref, o_ref):
    x2 = x1_ref[...] + po_ref[...]
    hp = _rms(x2, gp_ref[...]).astype(BF16)
    gate = jax.nn.sigmoid(jnp.dot(hp, wpg_ref[...], preferred_element_type=F32))
    pe = jnp.dot(p_ref[...], wpe_ref[...], preferred_element_type=F32)
    x3 = x2 + gate * pe
    o_ref[...] = _rms(x3, gf_ref[...]).astype(o_ref.dtype)


def _ple(x1, po, p, wpg, wpe, g_ple, g_final, tm=512):
    n, d = x1.shape
    tok = lambda w: pl.BlockSpec((tm, w), lambda i: (i, 0))
    res = lambda shape: pl.BlockSpec(shape, lambda i: (0, 0), pipeline_mode=pl.Buffered(1))
    return pl.pallas_call(
        _ple_kernel,
        grid=(n // tm,),
        in_specs=[tok(d), tok(d), tok(p.shape[1]), res(wpg.shape), res(wpe.shape), res((1, d)), res((1, d))],
        out_specs=tok(d),
        out_shape=jax.ShapeDtypeStruct((n, d), F32),
        compiler_params=_params("parallel"),
        name="ple_final",
    )(x1, po, p, wpg, wpe, g_ple.reshape(1, d), g_final.reshape(1, d))


def _layer(x, p, positions, layer, norm_mix, w_in, lb_logits, hg_norm, mla_q_norm, mla_kv_norm,
           w_uq, w_ukv, w_a, w_b, w_o, norm_ffn, peer_wq, peer_k1, peer_k2, peer_u, peer_v,
           norm_ple, w_pg, w_pe, norm_out):
    n, d = x.shape
    hgw = 4 * HG_WIDTH
    mla0 = hgw
    kr0 = mla0 + MLA_Q_LORA + MLA_KV_LORA
    g0 = kr0 + MLA_ROPE
    half = MLA_ROPE // 2

    uq = w_uq.reshape(MLA_Q_LORA, MLA_HEADS, MLA_QK)
    uq_r = uq[:, :, MLA_NOPE:]
    uq_rot = jnp.concatenate([-uq_r[:, :, half:], uq_r[:, :, :half]], axis=2)
    wq = jnp.concatenate([uq[:, :, :MLA_NOPE].reshape(MLA_Q_LORA, -1),
                          jnp.concatenate([uq_r, uq_rot], axis=2).reshape(MLA_Q_LORA, -1)],
                         axis=1).astype(BF16)
    ukv = w_ukv.reshape(MLA_KV_LORA, MLA_HEADS, MLA_NOPE + MLA_V)
    wk = ukv[:, :, :MLA_NOPE].reshape(MLA_KV_LORA, -1).astype(BF16)
    wvt = ukv[:, :, MLA_NOPE:].reshape(MLA_KV_LORA, -1).T.astype(BF16)

    tn = 1024
    w_in_t = w_in.T
    main_cols = (w_in_t.shape[0] // tn) * tn
    w_tail_t = w_in_t[main_cols:]
    w_tail_t = jnp.pad(w_tail_t, ((0, LANES - w_tail_t.shape[0]), (0, 0)))
    hn = _rmsnorm(x, norm_mix)
    proj = _matmul_ws(hn, w_in_t, main_cols, F32, 1024, tn, "in_proj_main")
    tail = _matmul_ws(hn, w_tail_t, LANES, F32, 1024, LANES, "in_proj_tail")

    o_a = _hgrn(proj, lb_logits, hg_norm, layer)
    qn, qr, kn, vt, kr = _mla_prep(proj, positions, mla_q_norm, mla_kv_norm, wq, wk, wvt)
    o_b = _attention(qn, qr, kn, kr, vt)
    x1, hn2 = _merge(o_a, o_b, proj, tail, g0, x, w_a.astype(BF16), w_b.astype(BF16), w_o.astype(BF16),
                     norm_ffn)

    e4, g4 = _route(hn2, peer_wq.T.astype(BF16), peer_k1, peer_k2)
    by_token = lambda a: a.transpose(0, 2, 3, 1).reshape(n, a.shape[1])
    w = _wbuild(by_token(e4), by_token(g4))
    po = _peer(hn2, peer_u, peer_v, w)
    return _ple(x1, po, p.astype(BF16), w_pg.astype(BF16), w_pe.astype(BF16), norm_ple, norm_out)


def kernel(x, p, positions, norm_mix, w_in, lb_logits, hg_norm, mla_q_norm, mla_kv_norm, w_uq, w_ukv,
           w_a, w_b, w_o, norm_ffn, peer_wq, peer_k1, peer_k2, peer_u, peer_v, norm_ple, w_pg, w_pe,
           norm_final):
    b, t, d = x.shape
    depth = w_in.shape[0]
    assert b == 1 and depth == 1, "kernel supports one sequence and one layer"
    out = _layer(x.reshape(b * t, d), p[0].reshape(b * t, -1), positions.reshape(-1), 0,
                 norm_mix[0], w_in[0], lb_logits, hg_norm[0], mla_q_norm[0], mla_kv_norm[0],
                 w_uq[0], w_ukv[0], w_a[0], w_b[0], w_o[0], norm_ffn[0], peer_wq[0], peer_k1[0],
                 peer_k2[0], peer_u[0], peer_v[0], norm_ple[0], w_pg[0], w_pe[0], norm_final)
    return out.reshape(b, t, d)
```

```python
import functools

import jax
import jax.numpy as jnp
import numpy as np
from jax import lax
from jax.experimental import pallas as pl
from jax.experimental.pallas import tpu as pltpu

D_MODEL = 2048
HG_HEADS = 8
HG_DK = 128
HG_DV = 128
HG_WIDTH = HG_HEADS * HG_DK
HG_CHUNK = 64
MLA_HEADS = 8
MLA_NOPE = 128
MLA_ROPE = 64
MLA_V = 128
MLA_Q_LORA = 512
MLA_KV_LORA = 512
MLA_QK = MLA_NOPE + MLA_ROPE
ROPE_THETA = 10000.0
PEER_HEADS = 8
PEER_NKEYS = 128
PEER_QDIM = 256
PEER_TOPK = 16
NORM_EPS = 1e-6
MASK_VALUE = -1e30

LANES = 128
VMEM_LIMIT = 56 * 1024 * 1024

_NT = (((1,), (1,)), ((), ()))
_TN = (((0,), (0,)), ((), ()))

F32 = jnp.float32
BF16 = jnp.bfloat16


def _params(*sem):
    return pltpu.CompilerParams(dimension_semantics=sem, vmem_limit_bytes=VMEM_LIMIT)


def _rms(x, gain):
    return x * lax.rsqrt(jnp.mean(x * x, axis=-1, keepdims=True) + NORM_EPS) * gain


def _rmsnorm_kernel(x_ref, g_ref, o_ref):
    o_ref[...] = _rms(x_ref[...], g_ref[...]).astype(o_ref.dtype)


def _rmsnorm(x, gain, tm=512):
    n, d = x.shape
    return pl.pallas_call(
        _rmsnorm_kernel,
        grid=(n // tm,),
        in_specs=[pl.BlockSpec((tm, d), lambda i: (i, 0)), pl.BlockSpec((1, d), lambda i: (0, 0))],
        out_specs=pl.BlockSpec((tm, d), lambda i: (i, 0)),
        out_shape=jax.ShapeDtypeStruct((n, d), BF16),
        compiler_params=_params("parallel"),
        name="rmsnorm",
    )(x, gain.reshape(1, d))


def _matmul_ws_kernel(a_ref, w_ref, o_ref, wb_ref):
    @pl.when(pl.program_id(1) == 0)
    def _():
        wb_ref[...] = w_ref[...].astype(BF16)

    o_ref[...] = lax.dot_general(a_ref[...], wb_ref[...], _NT, preferred_element_type=F32).astype(o_ref.dtype)


def _matmul_ws(a, w_t, cols, out_dtype, tm, tn, name):
    m, k = a.shape
    return pl.pallas_call(
        _matmul_ws_kernel,
        grid=(cols // tn, m // tm),
        in_specs=[pl.BlockSpec((tm, k), lambda j, i: (i, 0)), pl.BlockSpec((tn, k), lambda j, i: (j, 0))],
        out_specs=pl.BlockSpec((tm, tn), lambda j, i: (i, j)),
        out_shape=jax.ShapeDtypeStruct((m, cols), out_dtype),
        scratch_shapes=[pltpu.VMEM((tn, k), BF16)],
        compiler_params=_params("arbitrary", "arbitrary"),
        name=name,
    )(a, w_t)


HG_GROUP = 4


def _hgrn_kernel(q_ref, f_ref, i_ref, og_ref, lbl_ref, gain_ref, tri_ref, o_ref, st_ref, *, layer, chunks):
    @pl.when(pl.program_id(1) == 0)
    def _():
        st_ref[...] = jnp.zeros_like(st_ref)

    lbl = lbl_ref[...]
    e = jnp.exp(lbl - jnp.max(lbl, axis=0, keepdims=True))
    sm = e / jnp.sum(e, axis=0, keepdims=True)
    lb = jnp.sum(sm[: layer + 1], axis=0, keepdims=True)
    c = HG_CHUNK
    row = lax.broadcasted_iota(jnp.int32, (c, c), 0)
    col = lax.broadcasted_iota(jnp.int32, (c, c), 1)
    causal = row >= col

    q = q_ref[...]
    vb = i_ref[...].astype(BF16)
    f = lb + (1.0 - lb) * jax.nn.sigmoid(f_ref[...])
    g = jnp.log(f)
    k = 1.0 - f
    g_hi = g.astype(BF16)
    r1 = g - g_hi.astype(F32)
    g_mid = r1.astype(BF16)
    g_lo = (r1 - g_mid.astype(F32)).astype(BF16)
    gs = jnp.concatenate([g_hi, g_mid, g_lo], axis=1)
    gr = HG_GROUP * c
    parts = []
    for n in range(chunks // HG_GROUP):
        b3 = jnp.dot(tri_ref[...], gs[n * gr:(n + 1) * gr, :], preferred_element_type=F32)
        parts.append(b3[:, :HG_DK] + b3[:, HG_DK:2 * HG_DK] + b3[:, 2 * HG_DK:])
    b = jnp.concatenate(parts, axis=0)

    def rows_of(r):
        return jnp.concatenate([jnp.broadcast_to(b[n * c + r:n * c + r + 1, :], (c, HG_DK))
                                for n in range(chunks)], axis=0)

    b_mid = rows_of(c // 2 - 1)
    b_last = rows_of(c - 1)
    q_in = (q * jnp.exp(b - b_mid)).astype(BF16)
    k_in = (k * jnp.exp(b_mid - b)).astype(BF16)
    k_dec = (k * jnp.exp(b_last - b)).astype(BF16)
    qe = (q * jnp.exp(b)).astype(BF16)

    o_intra, dst = [], []
    for n in range(chunks):
        sl = slice(n * c, (n + 1) * c)
        a = lax.dot_general(q_in[sl], k_in[sl], _NT, preferred_element_type=F32)
        a = jnp.where(causal, a, 0.0).astype(BF16)
        o_intra.append(jnp.dot(a, vb[sl], preferred_element_type=F32))
        dst.append(lax.dot_general(vb[sl], k_dec[sl], _TN, preferred_element_type=F32))

    st = st_ref[...]
    outs = []
    for n in range(chunks):
        sl = slice(n * c, (n + 1) * c)
        outs.append(o_intra[n] + lax.dot_general(qe[sl], st.astype(BF16), _NT, preferred_element_type=F32))
        st = jnp.exp(b[(n + 1) * c - 1:(n + 1) * c, :]) * st + dst[n]
    st_ref[...] = st

    o = jnp.concatenate(outs, axis=0)
    og = og_ref[...]
    o = o * lax.rsqrt(jnp.mean(o * o, axis=-1, keepdims=True) + NORM_EPS) * gain_ref[...]
    o_ref[...] = (o * (og * jax.nn.sigmoid(og))).astype(o_ref.dtype)


def _hgrn(proj, lb_logits, gain, layer, tt=1024):
    n = proj.shape[0]
    h = HG_HEADS
    nl = lb_logits.shape[0]
    gr = HG_GROUP * HG_CHUNK
    idx = np.arange(gr)
    tri = ((idx[:, None] >= idx[None, :]) & (idx[:, None] // HG_CHUNK == idx[None, :] // HG_CHUNK))
    blk = lambda g: pl.BlockSpec((tt, HG_DK), lambda hh, t, g=g: (t, g * h + hh))
    return pl.pallas_call(
        functools.partial(_hgrn_kernel, layer=layer, chunks=tt // HG_CHUNK),
        grid=(h, n // tt),
        in_specs=[blk(0), blk(1), blk(2), blk(3),
                  pl.BlockSpec((nl, HG_DK), lambda hh, t: (0, hh)),
                  pl.BlockSpec((1, HG_DV), lambda hh, t: (0, hh)),
                  pl.BlockSpec((gr, gr), lambda hh, t: (0, 0))],
        out_specs=pl.BlockSpec((tt, HG_DV), lambda hh, t: (t, hh)),
        out_shape=jax.ShapeDtypeStruct((n, h * HG_DV), BF16),
        scratch_shapes=[pltpu.VMEM((HG_DV, HG_DK), F32)],
        compiler_params=_params("parallel", "arbitrary"),
        name="hgrn2",
    )(proj, proj, proj, proj, lb_logits, gain.reshape(1, -1), jnp.asarray(tri, BF16))


def _mla_prep_kernel(p_ref, pk_ref, pos_ref, fr_ref, qn_ref_g, kvn_ref_g, wq_ref, wk_ref, wvt_ref,
                     qn_ref, qr_ref, kn_ref, vt_ref, kr_ref, *, scale):
    hw = MLA_HEADS * MLA_NOPE
    half = MLA_ROPE // 2
    ang = pos_ref[...].astype(F32) * fr_ref[...]
    lane = lax.broadcasted_iota(jnp.int32, ang.shape, 1)
    cosv = jnp.where(lane < MLA_ROPE, jnp.cos(ang), 0.0)
    sinv = jnp.where(lane < MLA_ROPE, jnp.sin(ang), 0.0)

    def rope(y):
        return y * cosv + pltpu.roll(y, MLA_ROPE, 1) * sinv

    cq = _rms(p_ref[:, :MLA_Q_LORA], qn_ref_g[...]).astype(BF16)
    qres = jnp.dot(cq, wq_ref[...], preferred_element_type=F32)
    qn_ref[...] = (qres[:, :hw] * scale).astype(qn_ref.dtype)
    for h in range(MLA_HEADS):
        y = qres[:, hw + h * LANES: hw + (h + 1) * LANES]
        qr_ref[:, h * LANES:(h + 1) * LANES] = (rope(y) * scale).astype(qr_ref.dtype)
    ckv = _rms(p_ref[:, MLA_Q_LORA:MLA_Q_LORA + MLA_KV_LORA], kvn_ref_g[...]).astype(BF16)
    kn_ref[...] = jnp.dot(ckv, wk_ref[...], preferred_element_type=F32).astype(kn_ref.dtype)
    vt_ref[...] = lax.dot_general(wvt_ref[...], ckv, _NT, preferred_element_type=F32).astype(vt_ref.dtype)
    y = pk_ref[...]
    rot = jnp.where(lane < half, -pltpu.roll(y, LANES - half, 1), pltpu.roll(y, half, 1))
    kr_ref[...] = (y * cosv + rot * sinv).astype(kr_ref.dtype)


def _mla_prep(proj, positions, q_norm, kv_norm, wq, wk, wvt, tm=512):
    n = proj.shape[0]
    pw = MLA_Q_LORA + MLA_KV_LORA
    lat_blk = 4 * HG_WIDTH // pw
    kr_blk = (4 * HG_WIDTH + pw) // LANES
    hw = MLA_HEADS * MLA_NOPE
    half = MLA_ROPE // 2
    freqs = ROPE_THETA ** (-jnp.arange(half, dtype=F32) / half)
    fr = jnp.concatenate([freqs, freqs, jnp.zeros((LANES - MLA_ROPE,), F32)]).reshape(1, LANES)
    scale = float(1.0 / np.sqrt(MLA_QK).astype(np.float32)) * float(np.log2(np.e))
    full = lambda shape: pl.BlockSpec(shape, lambda i: (0, 0))
    tok = lambda w: pl.BlockSpec((tm, w), lambda i: (i, 0))
    return pl.pallas_call(
        functools.partial(_mla_prep_kernel, scale=scale),
        grid=(n // tm,),
        in_specs=[pl.BlockSpec((tm, pw), lambda i: (i, lat_blk)), pl.BlockSpec((tm, LANES), lambda i: (i, kr_blk)),
                  tok(1), full((1, LANES)), full((1, MLA_Q_LORA)), full((1, MLA_KV_LORA)),
                  full(wq.shape), full(wk.shape), full(wvt.shape)],
        out_specs=[tok(hw), tok(MLA_HEADS * LANES), tok(hw),
                   pl.BlockSpec((MLA_HEADS * MLA_V, tm), lambda i: (0, i)), tok(LANES)],
        out_shape=[jax.ShapeDtypeStruct((n, hw), BF16),
                   jax.ShapeDtypeStruct((n, MLA_HEADS * LANES), BF16),
                   jax.ShapeDtypeStruct((n, hw), BF16),
                   jax.ShapeDtypeStruct((MLA_HEADS * MLA_V, n), BF16),
                   jax.ShapeDtypeStruct((n, LANES), BF16)],
        compiler_params=_params("parallel"),
        name="mla_prep",
    )(proj, proj, positions.reshape(n, 1), fr, q_norm.reshape(1, -1), kv_norm.reshape(1, -1), wq, wk, wvt)


QSUB = 256


def _attn_kernel(qi_ref, kj_ref, qn_ref, qr_ref, kn_ref, kr_ref, vt_ref, o_ref, m_ref, l_ref, acc_ref, *, blk, hpb):
    s_id = pl.program_id(1)
    qi = qi_ref[s_id]
    kj = kj_ref[s_id]

    @pl.when(kj == 0)
    def _():
        m_ref[...] = jnp.full_like(m_ref, -jnp.inf)
        l_ref[...] = jnp.zeros_like(l_ref)
        acc_ref[...] = jnp.zeros_like(acc_ref)

    def update(masked):
        nsub = blk // QSUB
        items = [(hh, c) for hh in range(hpb) for c in range(nsub)]

        def nkeys(c):
            return (c + 1) * QSUB if masked else blk

        def scores(item):
            hh, c = item
            hl = slice(hh * LANES, (hh + 1) * LANES)
            sl = pl.ds(c * QSUB, QSUB)
            k = jnp.concatenate([kn_ref[0:nkeys(c), hl], kr_ref[0:nkeys(c), :]], axis=1)
            q = jnp.concatenate([qn_ref[sl, hl], qr_ref[sl, hl]], axis=1)
            return lax.dot_general(k, q, _NT, preferred_element_type=F32)

        sts = [scores(it) for it in items[:2]]
        for n, (hh, c) in enumerate(items):
            sl = pl.ds(c * QSUB, QSUB)
            st = sts[n]
            if n + 2 < len(items):
                sts.append(scores(items[n + 2]))
            if masked:
                key = lax.broadcasted_iota(jnp.int32, st.shape, 0)
                qry = lax.broadcasted_iota(jnp.int32, st.shape, 1) + c * QSUB
                st = jnp.where(key <= qry, st, MASK_VALUE)
            m_prev = m_ref[hh, :, sl]
            m_new = jnp.maximum(m_prev, jnp.max(st, axis=0, keepdims=True))
            alpha = jnp.exp2(m_prev - m_new)
            p = jnp.exp2(st - m_new)
            l_ref[hh, :, sl] = alpha * l_ref[hh, :, sl] + jnp.sum(p, axis=0, keepdims=True)
            pv = jnp.dot(vt_ref[hh * MLA_V:(hh + 1) * MLA_V, 0:nkeys(c)], p.astype(BF16),
                         preferred_element_type=F32)
            acc_ref[hh, :, sl] = alpha * acc_ref[hh, :, sl] + pv
            m_ref[hh, :, sl] = m_new

    @pl.when(kj < qi)
    def _():
        update(False)

    @pl.when(kj == qi)
    def _():
        update(True)
        for hh in range(hpb):
            o_ref[:, hh * MLA_V:(hh + 1) * MLA_V] = (acc_ref[hh] / l_ref[hh]).T.astype(o_ref.dtype)


def _attention(qn, qr, kn, kr, vt, blk=1024, hpb=4):
    n = qn.shape[0]
    nb = n // blk
    qi = np.array([i for i in range(nb) for j in range(i + 1)], np.int32)
    kj = np.array([j for i in range(nb) for j in range(i + 1)], np.int32)
    qspec = pl.BlockSpec((blk, hpb * LANES), lambda h, s, qi, kj: (qi[s], h))
    kspec = pl.BlockSpec((blk, hpb * LANES), lambda h, s, qi, kj: (kj[s], h))
    krspec = pl.BlockSpec((blk, LANES), lambda h, s, qi, kj: (kj[s], 0))
    vtspec = pl.BlockSpec((hpb * MLA_V, blk), lambda h, s, qi, kj: (h, kj[s]))
    grid_spec = pltpu.PrefetchScalarGridSpec(
        num_scalar_prefetch=2,
        grid=(MLA_HEADS // hpb, len(qi)),
        in_specs=[qspec, qspec, kspec, krspec, vtspec],
        out_specs=qspec,
        scratch_shapes=[pltpu.VMEM((hpb, 1, blk), F32), pltpu.VMEM((hpb, 1, blk), F32),
                        pltpu.VMEM((hpb, MLA_V, blk), F32)],
    )
    return pl.pallas_call(
        functools.partial(_attn_kernel, blk=blk, hpb=hpb),
        grid_spec=grid_spec,
        out_shape=jax.ShapeDtypeStruct((n, MLA_HEADS * MLA_V), BF16),
        compiler_params=_params("parallel", "arbitrary"),
        name="mla_attention",
    )(jnp.asarray(qi), jnp.asarray(kj), qn, qr, kn, kr, vt)


def _merge_kernel(oa_ref, ob_ref, g0_ref, g1_ref, g2_ref, g3_ref, gt_ref, x_ref, wa_ref, wb_ref, wo_ref,
                  g_ref, x1_ref, hn_ref, *, off):
    d = x_ref.shape[1]
    ya = jnp.dot(oa_ref[...], wa_ref[...], preferred_element_type=F32)
    yb = jnp.dot(ob_ref[...], wb_ref[...], preferred_element_type=F32)
    gates = jnp.concatenate([g0_ref[...], g1_ref[...], g2_ref[...], g3_ref[...], gt_ref[...]], axis=1)
    ga = gates[:, off:off + d]
    gb = gates[:, off + d:off + 2 * d]
    y = jax.nn.sigmoid(ga) * ya + jax.nn.sigmoid(gb) * yb
    x1 = x_ref[...] + jnp.dot(y.astype(BF16), wo_ref[...], preferred_element_type=F32)
    x1_ref[...] = x1
    hn_ref[...] = _rms(x1, g_ref[...]).astype(hn_ref.dtype)


def _merge(oa, ob, proj, tail, g0, x, wa, wb, wo, gain, tm=256):
    n, d = x.shape
    gw = d // 2
    base = (g0 // gw) * gw
    assert base + 4 * gw == proj.shape[1] and g0 + 2 * d <= base + 4 * gw + LANES
    tok = lambda w, c=0: pl.BlockSpec((tm, w), lambda i, c=c: (i, c))
    res = lambda shape: pl.BlockSpec(shape, lambda i: (0, 0), pipeline_mode=pl.Buffered(1))
    return pl.pallas_call(
        functools.partial(_merge_kernel, off=g0 - base),
        grid=(n // tm,),
        in_specs=[tok(oa.shape[1]), tok(ob.shape[1])] + [tok(gw, base // gw + c) for c in range(4)]
                 + [tok(LANES), tok(d), res(wa.shape), res(wb.shape), res(wo.shape), res((1, d))],
        out_specs=[tok(d), tok(d)],
        out_shape=[jax.ShapeDtypeStruct((n, d), F32), jax.ShapeDtypeStruct((n, d), BF16)],
        compiler_params=_params("parallel"),
        name="merge_outproj",
    )(oa, ob, proj, proj, proj, proj, tail, x, wa, wb, wo, gain.reshape(1, d))


R_TM = 1024
R_BLOCKS = R_TM // LANES
R_PITCH = PEER_NKEYS + 8


def _cand_pairs():
    k = PEER_TOPK
    return [(a, b) for a in range(k) for b in range(k) if (a + 1) * (b + 1) <= k]


def _first_max(vals, *payloads):
    payloads = [list(p) for p in payloads]
    while len(vals) > 1:
        nv, npl = [], [[] for _ in payloads]
        for i in range(0, len(vals) - 1, 2):
            take = vals[i] >= vals[i + 1]
            nv.append(jnp.maximum(vals[i], vals[i + 1]))
            for dst, p in zip(npl, payloads):
                dst.append(jnp.where(take, p[i], p[i + 1]))
        if len(vals) % 2:
            nv.append(vals[-1])
            for dst, p in zip(npl, payloads):
                dst.append(p[-1])
        vals, payloads = nv, npl
    return (vals[0], *[p[0] for p in payloads])


TOP_GROUP = 8


def _sort_group(vals, ids):
    vals, ids = list(vals), list(ids)
    n = len(vals)
    for rnd in range(n):
        for p in range(rnd % 2, n - 1, 2):
            take = vals[p] >= vals[p + 1]
            hi, lo = jnp.maximum(vals[p], vals[p + 1]), jnp.minimum(vals[p], vals[p + 1])
            ids[p], ids[p + 1] = jnp.where(take, ids[p], ids[p + 1]), jnp.where(take, ids[p + 1], ids[p])
            vals[p], vals[p + 1] = hi, lo
    return vals, ids


def _top_keys(x, k):
    gs = TOP_GROUP
    ng = len(x) // gs
    lv, li = [], []
    for g in range(ng):
        v, i = _sort_group(x[g * gs:(g + 1) * gs], [float(j) for j in range(g * gs, (g + 1) * gs)])
        lv.append(v)
        li.append(i)
    gids = [float(g) for g in range(ng)]
    vals, idxs = [], []
    for j in range(k):
        m, am, win = _first_max([v[0] for v in lv], [i[0] for i in li], gids)
        vals.append(m)
        idxs.append(am)
        depth = min(gs - 1, k - 1 - j)
        for g in range(ng):
            pop = win == gids[g]
            for r in range(depth):
                lv[g][r] = jnp.where(pop, lv[g][r + 1], lv[g][r])
                li[g][r] = jnp.where(pop, li[g][r + 1], li[g][r])
            if depth == gs - 1:
                lv[g][gs - 1] = jnp.where(pop, -jnp.inf, lv[g][gs - 1])
    return vals, idxs


def _split_bf16(x):
    hi = x.astype(BF16)
    return hi, (x - hi.astype(F32)).astype(BF16)


def _route_kernel(h_ref, wq_ref, k1_ref, k2_ref, e_ref, g_ref, s1_ref, s2_ref, *, hps):
    k = PEER_TOPK
    half = PEER_QDIM // 2
    h = h_ref[...]
    pairs = _cand_pairs()

    def scores(hh):
        qt = lax.dot_general(wq_ref[hh * PEER_QDIM:(hh + 1) * PEER_QDIM, :], h, _NT, preferred_element_type=F32)
        out = []
        for part, k_ref in ((0, k1_ref), (1, k2_ref)):
            q_hi, q_lo = _split_bf16(qt[part * half:(part + 1) * half])
            qs = jnp.concatenate([q_hi, q_lo, q_hi], axis=0)
            out.append(jnp.dot(k_ref[...], qs, preferred_element_type=F32))
        return out

    def by_key(s, s_ref):
        for j in range(R_BLOCKS):
            s_ref[j * R_PITCH:j * R_PITCH + PEER_NKEYS, :] = s[:, j * LANES:(j + 1) * LANES]
        return [s_ref[pl.ds(key, R_BLOCKS, stride=R_PITCH), :] for key in range(PEER_NKEYS)]

    nxt = scores(0)
    for hh in range(hps):
        s1, s2 = nxt
        if hh + 1 < hps:
            nxt = scores(hh + 1)
        v1, i1 = _top_keys(by_key(s1, s1_ref), k)
        v2, i2 = _top_keys(by_key(s2, s2_ref), k)
        cand = [v1[a] + v2[b] for a, b in pairs]
        ec = [i1[a] * float(PEER_NKEYS) + i2[b] for a, b in pairs]
        ci = [float(a * k + b) for a, b in pairs]
        cv, ce = [], []
        for _ in range(k):
            m, sel, e = _first_max(cand, ci, ec)
            cv.append(m)
            ce.append(e)
            cand = [jnp.where(sel == c, -jnp.inf, x) for c, x in zip(ci, cand)]
        ex = [jnp.exp(c - cv[0]) for c in cv]
        den = ex[0]
        for x in ex[1:]:
            den = den + x
        for j in range(k):
            g_ref[0, hh * k + j] = ex[j] / den
            e_ref[0, hh * k + j] = ce[j].astype(jnp.int32)


def _route(hn, wq_t, k1, k2, hps=2):
    n, d = hn.shape
    k = PEER_TOPK
    tiles = n // R_TM

    def stack(kk):
        hi, lo = _split_bf16(kk)
        return jnp.concatenate([hi, hi, lo], axis=1)

    full = lambda shape: pl.BlockSpec(shape, lambda i, g: (0, 0))
    out = pl.BlockSpec((1, hps * k, R_BLOCKS, LANES), lambda i, g: (i, g, 0, 0))
    sc = lambda: pltpu.VMEM((R_BLOCKS * R_PITCH, LANES), F32)
    return pl.pallas_call(
        functools.partial(_route_kernel, hps=hps),
        grid=(tiles, PEER_HEADS // hps),
        in_specs=[pl.BlockSpec((R_TM, d), lambda i, g: (i, 0)),
                  pl.BlockSpec((hps * PEER_QDIM, d), lambda i, g: (g, 0)),
                  full((PEER_NKEYS, 3 * (PEER_QDIM // 2))), full((PEER_NKEYS, 3 * (PEER_QDIM // 2)))],
        out_specs=[out, out],
        out_shape=[jax.ShapeDtypeStruct((tiles, PEER_HEADS * k, R_BLOCKS, LANES), jnp.int32),
                   jax.ShapeDtypeStruct((tiles, PEER_HEADS * k, R_BLOCKS, LANES), F32)],
        scratch_shapes=[sc(), sc()],
        compiler_params=_params("parallel", "arbitrary"),
        name="peer_route",
    )(hn, wq_t, stack(k1), stack(k2))


W_TILE = 128
W_PITCH = W_TILE + 8


def _wbuild_kernel(e_ref, g_ref, w_ref):
    nk = PEER_NKEYS
    iota = lax.broadcasted_iota(jnp.int32, (nk, LANES), 0)
    for i in range(nk):
        w_ref[i * W_PITCH + W_TILE:(i + 1) * W_PITCH, :] = jnp.zeros((W_PITCH - W_TILE, nk), F32)

    def body(t, carry):
        e = e_ref[pl.ds(t, 1), :]
        g = g_ref[pl.ds(t, 1), :]
        p1 = jnp.where(iota == (e >> 7), g, 0.0).astype(BF16)
        p2 = jnp.where(iota == (e & (nk - 1)), 1.0, 0.0).astype(BF16)
        wt = lax.dot_general(p1, p2, _NT, preferred_element_type=F32)
        w_ref[pl.ds(t, nk, stride=W_PITCH), :] = wt
        return carry

    lax.fori_loop(0, W_TILE, body, 0, unroll=32)


def _wbuild(e, g):
    n, pairs = e.shape
    nk = PEER_NKEYS
    tiles = n // W_TILE
    w = pl.pallas_call(
        _wbuild_kernel,
        grid=(tiles,),
        in_specs=[pl.BlockSpec((W_TILE, pairs), lambda i: (i, 0)), pl.BlockSpec((W_TILE, pairs), lambda i: (i, 0))],
        out_specs=pl.BlockSpec((nk * W_PITCH, nk), lambda i: (i, 0)),
        out_shape=jax.ShapeDtypeStruct((tiles * nk * W_PITCH, nk), F32),
        compiler_params=_params("parallel"),
        name="peer_wbuild",
    )(e, g)
    return w.reshape(tiles, nk, W_PITCH, nk)


def _peer_kernel(h_ref, u_ref, v_ref, w_ref, o_ref, *, groups, tiles):
    @pl.when(pl.program_id(1) == 0)
    def _():
        o_ref[...] = jnp.zeros_like(o_ref)

    z = lax.dot_general(h_ref[...], u_ref[...].astype(BF16), _NT, preferred_element_type=F32)
    rows = []
    for k in range(tiles):
        cols = []
        for i in range(groups):
            zi = z[k * W_TILE:(k + 1) * W_TILE, i * LANES:(i + 1) * LANES]
            gelu = 0.5 * zi * (1.0 + lax.erf(zi * np.float32(np.sqrt(0.5))))
            cols.append((w_ref[k, i, 0:W_TILE, :] * gelu).astype(BF16))
        rows.append(jnp.concatenate(cols, axis=1))
    a = jnp.concatenate(rows, axis=0)
    o_ref[...] += jnp.dot(a, v_ref[...].astype(BF16), preferred_element_type=F32)


def _peer(hn, u, v, w, tm=1024, te=512):
    n, d = hn.shape
    ne = u.shape[0]
    groups = te // PEER_NKEYS
    tiles = tm // W_TILE
    return pl.pallas_call(
        functools.partial(_peer_kernel, groups=groups, tiles=tiles),
        grid=(n // tm, ne // te),
        in_specs=[pl.BlockSpec((tm, d), lambda i, c: (i, 0)),
                  pl.BlockSpec((te, d), lambda i, c: (c, 0)),
                  pl.BlockSpec((te, d), lambda i, c: (c, 0)),
                  pl.BlockSpec((tiles, groups, W_PITCH, PEER_NKEYS), lambda i, c: (i, c, 0, 0))],
        out_specs=pl.BlockSpec((tm, d), lambda i, c: (i, 0)),
        out_shape=jax.ShapeDtypeStruct((n, d), F32),
        compiler_params=_params("parallel", "arbitrary"),
        name="peer_experts",
    )(hn, u, v, w)


def _ple_kernel(x1_ref, po_ref, p_ref, wpg_ref, wpe_ref, gp_ref, gf_ref, o_ref):
    x2 = x1_ref[...] + po_ref[...]
    hp = _rms(x2, gp_ref[...]).astype(BF16)
    gate = jax.nn.sigmoid(jnp.dot(hp, wpg_ref[...], preferred_element_type=F32))
    pe = jnp.dot(p_ref[...], wpe_ref[...], preferred_element_type=F32)
    x3 = x2 + gate * pe
    o_ref[...] = _rms(x3, gf_ref[...]).astype(o_ref.dtype)


def _ple(x1, po, p, wpg, wpe, g_ple, g_final, tm=512):
    n, d = x1.shape
    tok = lambda w: pl.BlockSpec((tm, w), lambda i: (i, 0))
    res = lambda shape: pl.BlockSpec(shape, lambda i: (0, 0), pipeline_mode=pl.Buffered(1))
    return pl.pallas_call(
        _ple_kernel,
        grid=(n // tm,),
        in_specs=[tok(d), tok(d), tok(p.shape[1]), res(wpg.shape), res(wpe.shape), res((1, d)), res((1, d))],
        out_specs=tok(d),
        out_shape=jax.ShapeDtypeStruct((n, d), F32),
        compiler_params=_params("parallel"),
        name="ple_final",
    )(x1, po, p, wpg, wpe, g_ple.reshape(1, d), g_final.reshape(1, d))


def _layer(x, p, positions, layer, norm_mix, w_in, lb_logits, hg_norm, mla_q_norm, mla_kv_norm,
           w_uq, w_ukv, w_a, w_b, w_o, norm_ffn, peer_wq, peer_k1, peer_k2, peer_u, peer_v,
           norm_ple, w_pg, w_pe, norm_out):
    n, d = x.shape
    hgw = 4 * HG_WIDTH
    mla0 = hgw
    kr0 = mla0 + MLA_Q_LORA + MLA_KV_LORA
    g0 = kr0 + MLA_ROPE
    half = MLA_ROPE // 2

    uq = w_uq.reshape(MLA_Q_LORA, MLA_HEADS, MLA_QK)
    uq_r = uq[:, :, MLA_NOPE:]
    uq_rot = jnp.concatenate([-uq_r[:, :, half:], uq_r[:, :, :half]], axis=2)
    wq = jnp.concatenate([uq[:, :, :MLA_NOPE].reshape(MLA_Q_LORA, -1),
                          jnp.concatenate([uq_r, uq_rot], axis=2).reshape(MLA_Q_LORA, -1)],
                         axis=1).astype(BF16)
    ukv = w_ukv.reshape(MLA_KV_LORA, MLA_HEADS, MLA_NOPE + MLA_V)
    wk = ukv[:, :, :MLA_NOPE].reshape(MLA_KV_LORA, -1).astype(BF16)
    wvt = ukv[:, :, MLA_NOPE:].reshape(MLA_KV_LORA, -1).T.astype(BF16)

    tn = 1024
    w_in_t = w_in.T
    main_cols = (w_in_t.shape[0] // tn) * tn
    w_tail_t = w_in_t[main_cols:]
    w_tail_t = jnp.pad(w_tail_t, ((0, LANES - w_tail_t.shape[0]), (0, 0)))
    hn = _rmsnorm(x, norm_mix)
    proj = _matmul_ws(hn, w_in_t, main_cols, F32, 1024, tn, "in_proj_main")
    tail = _matmul_ws(hn, w_tail_t, LANES, F32, 1024, LANES, "in_proj_tail")

    o_a = _hgrn(proj, lb_logits, hg_norm, layer)
    qn, qr, kn, vt, kr = _mla_prep(proj, positions, mla_q_norm, mla_kv_norm, wq, wk, wvt)
    o_b = _attention(qn, qr, kn, kr, vt)
    x1, hn2 = _merge(o_a, o_b, proj, tail, g0, x, w_a.astype(BF16), w_b.astype(BF16), w_o.astype(BF16),
                     norm_ffn)

    e4, g4 = _route(hn2, peer_wq.T.astype(BF16), peer_k1, peer_k2)
    by_token = lambda a: a.transpose(0, 2, 3, 1).reshape(n, a.shape[1])
    w = _wbuild(by_token(e4), by_token(g4))
    po = _peer(hn2, peer_u, peer_v, w)
    return _ple(x1, po, p.astype(BF16), w_pg.astype(BF16), w_pe.astype(BF16), norm_ple, norm_out)


def kernel(x, p, positions, norm_mix, w_in, lb_logits, hg_norm, mla_q_norm, mla_kv_norm, w_uq, w_ukv,
           w_a, w_b, w_o, norm_ffn, peer_wq, peer_k1, peer_k2, peer_u, peer_v, norm_ple, w_pg, w_pe,
           norm_final):
    b, t, d = x.shape
    depth = w_in.shape[0]
    assert b == 1 and depth == 1, "kernel supports one sequence and one layer"
    out = _layer(x.reshape(b * t, d), p[0].reshape(b * t, -1), positions.reshape(-1), 0,
                 norm_mix[0], w_in[0], lb_logits, hg_norm[0], mla_q_norm[0], mla_kv_norm[0],
                 w_uq[0], w_ukv[0], w_a[0], w_b[0], w_o[0], norm_ffn[0], peer_wq[0], peer_k1[0],
                 peer_k2[0], peer_u[0], peer_v[0], norm_ple[0], w_pg[0], w_pe[0], norm_final)
    return out.reshape(b, t, d)
```

```python
import functools

import jax
import jax.numpy as jnp
import numpy as np
from jax import lax
from jax.experimental import pallas as pl
from jax.experimental.pallas import tpu as pltpu

D_MODEL = 2048
HG_HEADS = 8
HG_DK = 128
HG_DV = 128
HG_WIDTH = HG_HEADS * HG_DK
HG_CHUNK = 64
MLA_HEADS = 8
MLA_NOPE = 128
MLA_ROPE = 64
MLA_V = 128
MLA_Q_LORA = 512
MLA_KV_LORA = 512
MLA_QK = MLA_NOPE + MLA_ROPE
ROPE_THETA = 10000.0
PEER_HEADS = 8
PEER_NKEYS = 128
PEER_QDIM = 256
PEER_TOPK = 16
NORM_EPS = 1e-6
MASK_VALUE = -1e30

LANES = 128
VMEM_LIMIT = 56 * 1024 * 1024

_NT = (((1,), (1,)), ((), ()))
_TN = (((0,), (0,)), ((), ()))

F32 = jnp.float32
BF16 = jnp.bfloat16


def _params(*sem):
    return pltpu.CompilerParams(dimension_semantics=sem, vmem_limit_bytes=VMEM_LIMIT)


def _rms(x, gain):
    return x * lax.rsqrt(jnp.mean(x * x, axis=-1, keepdims=True) + NORM_EPS) * gain


def _rmsnorm_kernel(x_ref, g_ref, wt_ref, o_ref, t_ref):
    hn = _rms(x_ref[...], g_ref[...]).astype(o_ref.dtype)
    o_ref[...] = hn
    t_ref[...] = lax.dot_general(hn, wt_ref[...], _NT, preferred_element_type=F32)


def _rmsnorm(x, gain, w_tail_t, tm=512):
    n, d = x.shape
    tw = w_tail_t.shape[0]
    return pl.pallas_call(
        _rmsnorm_kernel,
        grid=(n // tm,),
        in_specs=[pl.BlockSpec((tm, d), lambda i: (i, 0)), pl.BlockSpec((1, d), lambda i: (0, 0)),
                  pl.BlockSpec((tw, d), lambda i: (0, 0))],
        out_specs=[pl.BlockSpec((tm, d), lambda i: (i, 0)), pl.BlockSpec((tm, tw), lambda i: (i, 0))],
        out_shape=[jax.ShapeDtypeStruct((n, d), BF16), jax.ShapeDtypeStruct((n, tw), F32)],
        compiler_params=_params("parallel"),
        name="rmsnorm",
    )(x, gain.reshape(1, d), w_tail_t)


def _matmul_ws_kernel(a_ref, w_ref, o_ref, wb_ref):
    @pl.when(pl.program_id(1) == 0)
    def _():
        wb_ref[...] = w_ref[...].astype(BF16)

    o_ref[...] = lax.dot_general(a_ref[...], wb_ref[...], _NT, preferred_element_type=F32).astype(o_ref.dtype)


def _matmul_ws(a, w_t, cols, out_dtype, tm, tn, name):
    m, k = a.shape
    return pl.pallas_call(
        _matmul_ws_kernel,
        grid=(cols // tn, m // tm),
        in_specs=[pl.BlockSpec((tm, k), lambda j, i: (i, 0)), pl.BlockSpec((tn, k), lambda j, i: (j, 0))],
        out_specs=pl.BlockSpec((tm, tn), lambda j, i: (i, j)),
        out_shape=jax.ShapeDtypeStruct((m, cols), out_dtype),
        scratch_shapes=[pltpu.VMEM((tn, k), BF16)],
        compiler_params=_params("arbitrary", "arbitrary"),
        name=name,
    )(a, w_t)


HG_GROUP = 4


def _hgrn_kernel(q_ref, f_ref, i_ref, og_ref, lbl_ref, gain_ref, tri_ref, o_ref, st_ref, *, layer, chunks):
    @pl.when(pl.program_id(1) == 0)
    def _():
        st_ref[...] = jnp.zeros_like(st_ref)

    lbl = lbl_ref[...]
    e = jnp.exp(lbl - jnp.max(lbl, axis=0, keepdims=True))
    sm = e / jnp.sum(e, axis=0, keepdims=True)
    lb = jnp.sum(sm[: layer + 1], axis=0, keepdims=True)
    c = HG_CHUNK
    row = lax.broadcasted_iota(jnp.int32, (c, c), 0)
    col = lax.broadcasted_iota(jnp.int32, (c, c), 1)
    causal = row >= col

    q = q_ref[...]
    vb = i_ref[...].astype(BF16)
    f = lb + (1.0 - lb) * jax.nn.sigmoid(f_ref[...])
    g = jnp.log(f)
    k = 1.0 - f
    g_hi = g.astype(BF16)
    r1 = g - g_hi.astype(F32)
    g_mid = r1.astype(BF16)
    g_lo = (r1 - g_mid.astype(F32)).astype(BF16)
    gs = jnp.concatenate([g_hi, g_mid, g_lo], axis=1)
    gr = HG_GROUP * c
    parts = []
    for n in range(chunks // HG_GROUP):
        b3 = jnp.dot(tri_ref[...], gs[n * gr:(n + 1) * gr, :], preferred_element_type=F32)
        parts.append(b3[:, :HG_DK] + b3[:, HG_DK:2 * HG_DK] + b3[:, 2 * HG_DK:])
    b = jnp.concatenate(parts, axis=0)

    def rows_of(r):
        return jnp.concatenate([jnp.broadcast_to(b[n * c + r:n * c + r + 1, :], (c, HG_DK))
                                for n in range(chunks)], axis=0)

    b_mid = rows_of(c // 2 - 1)
    b_last = rows_of(c - 1)
    q_in = (q * jnp.exp(b - b_mid)).astype(BF16)
    k_in = (k * jnp.exp(b_mid - b)).astype(BF16)
    k_dec = (k * jnp.exp(b_last - b)).astype(BF16)
    qe = (q * jnp.exp(b)).astype(BF16)

    o_intra, dst = [], []
    for n in range(chunks):
        sl = slice(n * c, (n + 1) * c)
        a = lax.dot_general(q_in[sl], k_in[sl], _NT, preferred_element_type=F32)
        a = jnp.where(causal, a, 0.0).astype(BF16)
        o_intra.append(jnp.dot(a, vb[sl], preferred_element_type=F32))
        dst.append(lax.dot_general(vb[sl], k_dec[sl], _TN, preferred_element_type=F32))

    st = st_ref[...]
    outs = []
    for n in range(chunks):
        sl = slice(n * c, (n + 1) * c)
        outs.append(o_intra[n] + lax.dot_general(qe[sl], st.astype(BF16), _NT, preferred_element_type=F32))
        st = jnp.exp(b[(n + 1) * c - 1:(n + 1) * c, :]) * st + dst[n]
    st_ref[...] = st

    o = jnp.concatenate(outs, axis=0)
    og = og_ref[...]
    o = o * lax.rsqrt(jnp.mean(o * o, axis=-1, keepdims=True) + NORM_EPS) * gain_ref[...]
    o_ref[...] = (o * (og * jax.nn.sigmoid(og))).astype(o_ref.dtype)


def _hgrn(proj, lb_logits, gain, layer, tt=1024):
    n = proj.shape[0]
    h = HG_HEADS
    nl = lb_logits.shape[0]
    gr = HG_GROUP * HG_CHUNK
    idx = np.arange(gr)
    tri = ((idx[:, None] >= idx[None, :]) & (idx[:, None] // HG_CHUNK == idx[None, :] // HG_CHUNK))
    blk = lambda g: pl.BlockSpec((tt, HG_DK), lambda hh, t, g=g: (t, g * h + hh))
    return pl.pallas_call(
        functools.partial(_hgrn_kernel, layer=layer, chunks=tt // HG_CHUNK),
        grid=(h, n // tt),
        in_specs=[blk(0), blk(1), blk(2), blk(3),
                  pl.BlockSpec((nl, HG_DK), lambda hh, t: (0, hh)),
                  pl.BlockSpec((1, HG_DV), lambda hh, t: (0, hh)),
                  pl.BlockSpec((gr, gr), lambda hh, t: (0, 0))],
        out_specs=pl.BlockSpec((tt, HG_DV), lambda hh, t: (t, hh)),
        out_shape=jax.ShapeDtypeStruct((n, h * HG_DV), BF16),
        scratch_shapes=[pltpu.VMEM((HG_DV, HG_DK), F32)],
        compiler_params=_params("parallel", "arbitrary"),
        name="hgrn2",
    )(proj, proj, proj, proj, lb_logits, gain.reshape(1, -1), jnp.asarray(tri, BF16))


def _mla_prep_kernel(p_ref, pk_ref, pos_ref, fr_ref, qn_ref_g, kvn_ref_g, wq_ref, wk_ref, wvt_ref,
                     qn_ref, qr_ref, kn_ref, vt_ref, kr_ref, *, scale):
    hw = MLA_HEADS * MLA_NOPE
    half = MLA_ROPE // 2
    ang = pos_ref[...].astype(F32) * fr_ref[...]
    lane = lax.broadcasted_iota(jnp.int32, ang.shape, 1)
    cosv = jnp.where(lane < MLA_ROPE, jnp.cos(ang), 0.0)
    sinv = jnp.where(lane < MLA_ROPE, jnp.sin(ang), 0.0)

    def rope(y):
        return y * cosv + pltpu.roll(y, MLA_ROPE, 1) * sinv

    cq = _rms(p_ref[:, :MLA_Q_LORA], qn_ref_g[...]).astype(BF16)
    qres = jnp.dot(cq, wq_ref[...], preferred_element_type=F32)
    qn_ref[...] = (qres[:, :hw] * scale).astype(qn_ref.dtype)
    for h in range(MLA_HEADS):
        y = qres[:, hw + h * LANES: hw + (h + 1) * LANES]
        qr_ref[:, h * LANES:(h + 1) * LANES] = (rope(y) * scale).astype(qr_ref.dtype)
    ckv = _rms(p_ref[:, MLA_Q_LORA:MLA_Q_LORA + MLA_KV_LORA], kvn_ref_g[...]).astype(BF16)
    kn_ref[...] = jnp.dot(ckv, wk_ref[...], preferred_element_type=F32).astype(kn_ref.dtype)
    vt_ref[...] = lax.dot_general(wvt_ref[...], ckv, _NT, preferred_element_type=F32).astype(vt_ref.dtype)
    y = pk_ref[...]
    rot = jnp.where(lane < half, -pltpu.roll(y, LANES - half, 1), pltpu.roll(y, half, 1))
    kr_ref[...] = (y * cosv + rot * sinv).astype(kr_ref.dtype)


def _mla_prep(proj, positions, q_norm, kv_norm, wq, wk, wvt, tm=512):
    n = proj.shape[0]
    pw = MLA_Q_LORA + MLA_KV_LORA
    lat_blk = 4 * HG_WIDTH // pw
    kr_blk = (4 * HG_WIDTH + pw) // LANES
    hw = MLA_HEADS * MLA_NOPE
    half = MLA_ROPE // 2
    freqs = ROPE_THETA ** (-jnp.arange(half, dtype=F32) / half)
    fr = jnp.concatenate([freqs, freqs, jnp.zeros((LANES - MLA_ROPE,), F32)]).reshape(1, LANES)
    scale = float(1.0 / np.sqrt(MLA_QK).astype(np.float32)) * float(np.log2(np.e))
    full = lambda shape: pl.BlockSpec(shape, lambda i: (0, 0))
    tok = lambda w: pl.BlockSpec((tm, w), lambda i: (i, 0))
    return pl.pallas_call(
        functools.partial(_mla_prep_kernel, scale=scale),
        grid=(n // tm,),
        in_specs=[pl.BlockSpec((tm, pw), lambda i: (i, lat_blk)), pl.BlockSpec((tm, LANES), lambda i: (i, kr_blk)),
                  tok(1), full((1, LANES)), full((1, MLA_Q_LORA)), full((1, MLA_KV_LORA)),
                  full(wq.shape), full(wk.shape), full(wvt.shape)],
        out_specs=[tok(hw), tok(MLA_HEADS * LANES), tok(hw),
                   pl.BlockSpec((MLA_HEADS * MLA_V, tm), lambda i: (0, i)), tok(LANES)],
        out_shape=[jax.ShapeDtypeStruct((n, hw), BF16),
                   jax.ShapeDtypeStruct((n, MLA_HEADS * LANES), BF16),
                   jax.ShapeDtypeStruct((n, hw), BF16),
                   jax.ShapeDtypeStruct((MLA_HEADS * MLA_V, n), BF16),
                   jax.ShapeDtypeStruct((n, LANES), BF16)],
        compiler_params=_params("parallel"),
        name="mla_prep",
    )(proj, proj, positions.reshape(n, 1), fr, q_norm.reshape(1, -1), kv_norm.reshape(1, -1), wq, wk, wvt)


QSUB = 256


def _attn_kernel(qi_ref, kj_ref, qn_ref, qr_ref, kn_ref, kr_ref, vt_ref, o_ref, m_ref, l_ref, acc_ref, *, blk, hpb):
    s_id = pl.program_id(1)
    qi = qi_ref[s_id]
    kj = kj_ref[s_id]

    @pl.when(kj == 0)
    def _():
        m_ref[...] = jnp.full_like(m_ref, -jnp.inf)
        l_ref[...] = jnp.zeros_like(l_ref)
        acc_ref[...] = jnp.zeros_like(acc_ref)

    def update(masked):
        nsub = blk // QSUB
        items = [(hh, c) for hh in range(hpb) for c in range(nsub)]

        def nkeys(c):
            return (c + 1) * QSUB if masked else blk

        def scores(item):
            hh, c = item
            hl = slice(hh * LANES, (hh + 1) * LANES)
            sl = pl.ds(c * QSUB, QSUB)
            k = jnp.concatenate([kn_ref[0:nkeys(c), hl], kr_ref[0:nkeys(c), :]], axis=1)
            q = jnp.concatenate([qn_ref[sl, hl], qr_ref[sl, hl]], axis=1)
            return lax.dot_general(k, q, _NT, preferred_element_type=F32)

        sts = [scores(it) for it in items[:2]]
        for n, (hh, c) in enumerate(items):
            sl = pl.ds(c * QSUB, QSUB)
            st = sts[n]
            if n + 2 < len(items):
                sts.append(scores(items[n + 2]))
            if masked:
                key = lax.broadcasted_iota(jnp.int32, st.shape, 0)
                qry = lax.broadcasted_iota(jnp.int32, st.shape, 1) + c * QSUB
                st = jnp.where(key <= qry, st, MASK_VALUE)
            m_prev = m_ref[hh, :, sl]
            m_new = jnp.maximum(m_prev, jnp.max(st, axis=0, keepdims=True))
            alpha = jnp.exp2(m_prev - m_new)
            p = jnp.exp2(st - m_new)
            l_ref[hh, :, sl] = alpha * l_ref[hh, :, sl] + jnp.sum(p, axis=0, keepdims=True)
            pv = jnp.dot(vt_ref[hh * MLA_V:(hh + 1) * MLA_V, 0:nkeys(c)], p.astype(BF16),
                         preferred_element_type=F32)
            acc_ref[hh, :, sl] = alpha * acc_ref[hh, :, sl] + pv
            m_ref[hh, :, sl] = m_new

    @pl.when(kj < qi)
    def _():
        update(False)

    @pl.when(kj == qi)
    def _():
        update(True)
        for hh in range(hpb):
            o_ref[:, hh * MLA_V:(hh + 1) * MLA_V] = (acc_ref[hh] / l_ref[hh]).T.astype(o_ref.dtype)


def _attention(qn, qr, kn, kr, vt, blk=1024, hpb=4):
    n = qn.shape[0]
    nb = n // blk
    qi = np.array([i for i in range(nb) for j in range(i + 1)], np.int32)
    kj = np.array([j for i in range(nb) for j in range(i + 1)], np.int32)
    qspec = pl.BlockSpec((blk, hpb * LANES), lambda h, s, qi, kj: (qi[s], h))
    kspec = pl.BlockSpec((blk, hpb * LANES), lambda h, s, qi, kj: (kj[s], h))
    krspec = pl.BlockSpec((blk, LANES), lambda h, s, qi, kj: (kj[s], 0))
    vtspec = pl.BlockSpec((hpb * MLA_V, blk), lambda h, s, qi, kj: (h, kj[s]))
    grid_spec = pltpu.PrefetchScalarGridSpec(
        num_scalar_prefetch=2,
        grid=(MLA_HEADS // hpb, len(qi)),
        in_specs=[qspec, qspec, kspec, krspec, vtspec],
        out_specs=qspec,
        scratch_shapes=[pltpu.VMEM((hpb, 1, blk), F32), pltpu.VMEM((hpb, 1, blk), F32),
                        pltpu.VMEM((hpb, MLA_V, blk), F32)],
    )
    return pl.pallas_call(
        functools.partial(_attn_kernel, blk=blk, hpb=hpb),
        grid_spec=grid_spec,
        out_shape=jax.ShapeDtypeStruct((n, MLA_HEADS * MLA_V), BF16),
        compiler_params=_params("parallel", "arbitrary"),
        name="mla_attention",
    )(jnp.asarray(qi), jnp.asarray(kj), qn, qr, kn, kr, vt)


def _merge_kernel(oa_ref, ob_ref, g0_ref, g1_ref, g2_ref, g3_ref, gt_ref, x_ref, wa_ref, wb_ref, wo_ref,
                  g_ref, x1_ref, hn_ref, *, off):
    d = x_ref.shape[1]
    ya = jnp.dot(oa_ref[...], wa_ref[...], preferred_element_type=F32)
    yb = jnp.dot(ob_ref[...], wb_ref[...], preferred_element_type=F32)
    gates = jnp.concatenate([g0_ref[...], g1_ref[...], g2_ref[...], g3_ref[...], gt_ref[...]], axis=1)
    ga = gates[:, off:off + d]
    gb = gates[:, off + d:off + 2 * d]
    y = jax.nn.sigmoid(ga) * ya + jax.nn.sigmoid(gb) * yb
    x1 = x_ref[...] + jnp.dot(y.astype(BF16), wo_ref[...], preferred_element_type=F32)
    x1_ref[...] = x1
    hn_ref[...] = _rms(x1, g_ref[...]).astype(hn_ref.dtype)


def _merge(oa, ob, proj, tail, g0, x, wa, wb, wo, gain, tm=256):
    n, d = x.shape
    gw = d // 2
    base = (g0 // gw) * gw
    assert base + 4 * gw == proj.shape[1] and g0 + 2 * d <= base + 4 * gw + LANES
    tok = lambda w, c=0: pl.BlockSpec((tm, w), lambda i, c=c: (i, c))
    res = lambda shape: pl.BlockSpec(shape, lambda i: (0, 0), pipeline_mode=pl.Buffered(1))
    return pl.pallas_call(
        functools.partial(_merge_kernel, off=g0 - base),
        grid=(n // tm,),
        in_specs=[tok(oa.shape[1]), tok(ob.shape[1])] + [tok(gw, base // gw + c) for c in range(4)]
                 + [tok(LANES), tok(d), res(wa.shape), res(wb.shape), res(wo.shape), res((1, d))],
        out_specs=[tok(d), tok(d)],
        out_shape=[jax.ShapeDtypeStruct((n, d), F32), jax.ShapeDtypeStruct((n, d), BF16)],
        compiler_params=_params("parallel"),
        name="merge_outproj",
    )(oa, ob, proj, proj, proj, proj, tail, x, wa, wb, wo, gain.reshape(1, d))


R_TM = 1024
R_BLOCKS = R_TM // LANES
R_PITCH = PEER_NKEYS + 8


def _first_max(vals, *payloads):
    payloads = [list(p) for p in payloads]
    while len(vals) > 1:
        nv, npl = [], [[] for _ in payloads]
        for i in range(0, len(vals) - 1, 2):
            take = vals[i] >= vals[i + 1]
            nv.append(jnp.maximum(vals[i], vals[i + 1]))
            for dst, p in zip(npl, payloads):
                dst.append(jnp.where(take, p[i], p[i + 1]))
        if len(vals) % 2:
            nv.append(vals[-1])
            for dst, p in zip(npl, payloads):
                dst.append(p[-1])
        vals, payloads = nv, npl
    return (vals[0], *[p[0] for p in payloads])


TOP_GROUP = 8


def _sort_group(vals, ids):
    vals, ids = list(vals), list(ids)
    n = len(vals)
    for rnd in range(n):
        for p in range(rnd % 2, n - 1, 2):
            take = vals[p] >= vals[p + 1]
            hi, lo = jnp.maximum(vals[p], vals[p + 1]), jnp.minimum(vals[p], vals[p + 1])
            ids[p], ids[p + 1] = jnp.where(take, ids[p], ids[p + 1]), jnp.where(take, ids[p + 1], ids[p])
            vals[p], vals[p + 1] = hi, lo
    return vals, ids


def _pop_rounds(lists, payloads, k):
    lists = [list(g) for g in lists]
    payloads = [[list(g) for g in p] for p in payloads]
    gids = [float(g) for g in range(len(lists))]
    vals, outs = [], [[] for _ in payloads]
    for j in range(k):
        m, win, *heads = _first_max([g[0] for g in lists], gids, *[[g[0] for g in p] for p in payloads])
        vals.append(m)
        for o, hd in zip(outs, heads):
            o.append(hd)
        for g, lst in enumerate(lists):
            depth = min(len(lst) - 1, k - 1 - j)
            pop = win == gids[g]
            for r in range(depth):
                lst[r] = jnp.where(pop, lst[r + 1], lst[r])
                for p in payloads:
                    p[g][r] = jnp.where(pop, p[g][r + 1], p[g][r])
            if depth == len(lst) - 1:
                lst[-1] = jnp.where(pop, -jnp.inf, lst[-1])
    return vals, outs


def _top_keys(x, k):
    gs = TOP_GROUP
    lv, li = [], []
    for g in range(len(x) // gs):
        v, i = _sort_group(x[g * gs:(g + 1) * gs], [float(j) for j in range(g * gs, (g + 1) * gs)])
        lv.append(v)
        li.append(i)
    vals, (idxs,) = _pop_rounds(lv, [li], k)
    return vals, idxs


def _top_pairs(v1, i1, v2, i2, k):
    ranks = [[b for b in range(k) if (a + 1) * (b + 1) <= k] for a in range(k)]
    sums = [[v1[a] + v2[b] for b in bs] for a, bs in enumerate(ranks)]
    ids = [[i1[a] * float(PEER_NKEYS) + i2[b] for b in bs] for a, bs in enumerate(ranks)]
    vals, (sel,) = _pop_rounds(sums, [ids], k)
    return vals, sel


def _split_bf16(x):
    hi = x.astype(BF16)
    return hi, (x - hi.astype(F32)).astype(BF16)


def _route_kernel(h_ref, wq_ref, k1_ref, k2_ref, e_ref, g_ref, s1_ref, s2_ref, *, hps):
    k = PEER_TOPK
    half = PEER_QDIM // 2
    h = h_ref[...]

    def scores(hh):
        qt = lax.dot_general(wq_ref[hh * PEER_QDIM:(hh + 1) * PEER_QDIM, :], h, _NT, preferred_element_type=F32)
        out = []
        for part, k_ref in ((0, k1_ref), (1, k2_ref)):
            q_hi, q_lo = _split_bf16(qt[part * half:(part + 1) * half])
            qs = jnp.concatenate([q_hi, q_lo, q_hi], axis=0)
            out.append(jnp.dot(k_ref[...], qs, preferred_element_type=F32))
        return out

    def by_key(s, s_ref):
        for j in range(R_BLOCKS):
            s_ref[j * R_PITCH:j * R_PITCH + PEER_NKEYS, :] = s[:, j * LANES:(j + 1) * LANES]
        return [s_ref[pl.ds(key, R_BLOCKS, stride=R_PITCH), :] for key in range(PEER_NKEYS)]

    nxt = scores(0)
    for hh in range(hps):
        s1, s2 = nxt
        if hh + 1 < hps:
            nxt = scores(hh + 1)
        v1, i1 = _top_keys(by_key(s1, s1_ref), k)
        v2, i2 = _top_keys(by_key(s2, s2_ref), k)
        cv, ce = _top_pairs(v1, i1, v2, i2, k)
        ex = [jnp.exp(c - cv[0]) for c in cv]
        den = ex[0]
        for x in ex[1:]:
            den = den + x
        for j in range(k):
            g_ref[0, hh * k + j] = ex[j] / den
            e_ref[0, hh * k + j] = ce[j].astype(jnp.int32)


def _route(hn, wq_t, k1, k2, hps=2):
    n, d = hn.shape
    k = PEER_TOPK
    tiles = n // R_TM

    def stack(kk):
        hi, lo = _split_bf16(kk)
        return jnp.concatenate([hi, hi, lo], axis=1)

    full = lambda shape: pl.BlockSpec(shape, lambda i, g: (0, 0))
    out = pl.BlockSpec((1, hps * k, R_BLOCKS, LANES), lambda i, g: (i, g, 0, 0))
    sc = lambda: pltpu.VMEM((R_BLOCKS * R_PITCH, LANES), F32)
    return pl.pallas_call(
        functools.partial(_route_kernel, hps=hps),
        grid=(tiles, PEER_HEADS // hps),
        in_specs=[pl.BlockSpec((R_TM, d), lambda i, g: (i, 0)),
                  pl.BlockSpec((hps * PEER_QDIM, d), lambda i, g: (g, 0)),
                  full((PEER_NKEYS, 3 * (PEER_QDIM // 2))), full((PEER_NKEYS, 3 * (PEER_QDIM // 2)))],
        out_specs=[out, out],
        out_shape=[jax.ShapeDtypeStruct((tiles, PEER_HEADS * k, R_BLOCKS, LANES), jnp.int32),
                   jax.ShapeDtypeStruct((tiles, PEER_HEADS * k, R_BLOCKS, LANES), F32)],
        scratch_shapes=[sc(), sc()],
        compiler_params=_params("parallel", "arbitrary"),
        name="peer_route",
    )(hn, wq_t, stack(k1), stack(k2))


W_TILE = 128
W_PITCH = W_TILE + 8


def _wbuild_kernel(e_ref, g_ref, w_ref):
    nk = PEER_NKEYS
    iota = lax.broadcasted_iota(jnp.int32, (nk, LANES), 0)
    for i in range(nk):
        w_ref[i * W_PITCH + W_TILE:(i + 1) * W_PITCH, :] = jnp.zeros((W_PITCH - W_TILE, nk), F32)

    def body(t, carry):
        e = e_ref[pl.ds(t, 1), :]
        g = g_ref[pl.ds(t, 1), :]
        p1 = jnp.where(iota == (e >> 7), g, 0.0).astype(BF16)
        p2 = jnp.where(iota == (e & (nk - 1)), 1.0, 0.0).astype(BF16)
        wt = lax.dot_general(p1, p2, _NT, preferred_element_type=F32)
        w_ref[pl.ds(t, nk, stride=W_PITCH), :] = wt
        return carry

    lax.fori_loop(0, W_TILE, body, 0, unroll=32)


def _wbuild(e, g):
    n, pairs = e.shape
    nk = PEER_NKEYS
    tiles = n // W_TILE
    w = pl.pallas_call(
        _wbuild_kernel,
        grid=(tiles,),
        in_specs=[pl.BlockSpec((W_TILE, pairs), lambda i: (i, 0)), pl.BlockSpec((W_TILE, pairs), lambda i: (i, 0))],
        out_specs=pl.BlockSpec((nk * W_PITCH, nk), lambda i: (i, 0)),
        out_shape=jax.ShapeDtypeStruct((tiles * nk * W_PITCH, nk), F32),
        compiler_params=_params("parallel"),
        name="peer_wbuild",
    )(e, g)
    return w.reshape(tiles, nk, W_PITCH, nk)


def _peer_kernel(h_ref, u_ref, v_ref, w_ref, o_ref, *, groups, tiles):
    @pl.when(pl.program_id(1) == 0)
    def _():
        o_ref[...] = jnp.zeros_like(o_ref)

    z = lax.dot_general(h_ref[...], u_ref[...].astype(BF16), _NT, preferred_element_type=F32)
    rows = []
    for k in range(tiles):
        cols = []
        for i in range(groups):
            zi = z[k * W_TILE:(k + 1) * W_TILE, i * LANES:(i + 1) * LANES]
            gelu = 0.5 * zi * (1.0 + lax.erf(zi * np.float32(np.sqrt(0.5))))
            cols.append((w_ref[k, i, 0:W_TILE, :] * gelu).astype(BF16))
        rows.append(jnp.concatenate(cols, axis=1))
    a = jnp.concatenate(rows, axis=0)
    o_ref[...] += jnp.dot(a, v_ref[...].astype(BF16), preferred_element_type=F32)


def _peer(hn, u, v, w, tm=1024, te=512):
    n, d = hn.shape
    ne = u.shape[0]
    groups = te // PEER_NKEYS
    tiles = tm // W_TILE
    return pl.pallas_call(
        functools.partial(_peer_kernel, groups=groups, tiles=tiles),
        grid=(n // tm, ne // te),
        in_specs=[pl.BlockSpec((tm, d), lambda i, c: (i, 0)),
                  pl.BlockSpec((te, d), lambda i, c: (c, 0)),
                  pl.BlockSpec((te, d), lambda i, c: (c, 0)),
                  pl.BlockSpec((tiles, groups, W_PITCH, PEER_NKEYS), lambda i, c: (i, c, 0, 0))],
        out_specs=pl.BlockSpec((tm, d), lambda i, c: (i, 0)),
        out_shape=jax.ShapeDtypeStruct((n, d), F32),
        compiler_params=_params("parallel", "arbitrary"),
        name="peer_experts",
    )(hn, u, v, w)


def _ple_kernel(x1_ref, po_ref, p_ref, wpg_ref, wpe_ref, gp_ref, gf_ref, o_ref):
    x2 = x1_ref[...] + po_ref[...]
    hp = _rms(x2, gp_ref[...]).astype(BF16)
    gate = jax.nn.sigmoid(jnp.dot(hp, wpg_ref[...], preferred_element_type=F32))
    pe = jnp.dot(p_ref[...], wpe_ref[...], preferred_element_type=F32)
    x3 = x2 + gate * pe
    o_ref[...] = _rms(x3, gf_ref[...]).astype(o_ref.dtype)


def _ple(x1, po, p, wpg, wpe, g_ple, g_final, tm=512):
    n, d = x1.shape
    tok = lambda w: pl.BlockSpec((tm, w), lambda i: (i, 0))
    res = lambda shape: pl.BlockSpec(shape, lambda i: (0, 0), pipeline_mode=pl.Buffered(1))
    return pl.pallas_call(
        _ple_kernel,
        grid=(n // tm,),
        in_specs=[tok(d), tok(d), tok(p.shape[1]), res(wpg.shape), res(wpe.shape), res((1, d)), res((1, d))],
        out_specs=tok(d),
        out_shape=jax.ShapeDtypeStruct((n, d), F32),
        compiler_params=_params("parallel"),
        name="ple_final",
    )(x1, po, p, wpg, wpe, g_ple.reshape(1, d), g_final.reshape(1, d))


def _layer(x, p, positions, layer, norm_mix, w_in, lb_logits, hg_norm, mla_q_norm, mla_kv_norm,
           w_uq, w_ukv, w_a, w_b, w_o, norm_ffn, peer_wq, peer_k1, peer_k2, peer_u, peer_v,
           norm_ple, w_pg, w_pe, norm_out):
    n, d = x.shape
    hgw = 4 * HG_WIDTH
    mla0 = hgw
    kr0 = mla0 + MLA_Q_LORA + MLA_KV_LORA
    g0 = kr0 + MLA_ROPE
    half = MLA_ROPE // 2

    uq = w_uq.reshape(MLA_Q_LORA, MLA_HEADS, MLA_QK)
    uq_r = uq[:, :, MLA_NOPE:]
    uq_rot = jnp.concatenate([-uq_r[:, :, half:], uq_r[:, :, :half]], axis=2)
    wq = jnp.concatenate([uq[:, :, :MLA_NOPE].reshape(MLA_Q_LORA, -1),
                          jnp.concatenate([uq_r, uq_rot], axis=2).reshape(MLA_Q_LORA, -1)],
                         axis=1).astype(BF16)
    ukv = w_ukv.reshape(MLA_KV_LORA, MLA_HEADS, MLA_NOPE + MLA_V)
    wk = ukv[:, :, :MLA_NOPE].reshape(MLA_KV_LORA, -1).astype(BF16)
    wvt = ukv[:, :, MLA_NOPE:].reshape(MLA_KV_LORA, -1).T.astype(BF16)

    tn = 1024
    w_in_t = w_in.T
    main_cols = (w_in_t.shape[0] // tn) * tn
    w_tail_t = w_in_t[main_cols:]
    w_tail_t = jnp.pad(w_tail_t, ((0, LANES - w_tail_t.shape[0]), (0, 0))).astype(BF16)
    hn, tail = _rmsnorm(x, norm_mix, w_tail_t)
    proj = _matmul_ws(hn, w_in_t, main_cols, F32, 1024, tn, "in_proj_main")

    o_a = _hgrn(proj, lb_logits, hg_norm, layer)
    qn, qr, kn, vt, kr = _mla_prep(proj, positions, mla_q_norm, mla_kv_norm, wq, wk, wvt)
    o_b = _attention(qn, qr, kn, kr, vt)
    x1, hn2 = _merge(o_a, o_b, proj, tail, g0, x, w_a.astype(BF16), w_b.astype(BF16), w_o.astype(BF16),
                     norm_ffn)

    e4, g4 = _route(hn2, peer_wq.T.astype(BF16), peer_k1, peer_k2)
    by_token = lambda a: a.transpose(0, 2, 3, 1).reshape(n, a.shape[1])
    w = _wbuild(by_token(e4), by_token(g4))
    po = _peer(hn2, peer_u, peer_v, w)
    return _ple(x1, po, p.astype(BF16), w_pg.astype(BF16), w_pe.astype(BF16), norm_ple, norm_out)


def kernel(x, p, positions, norm_mix, w_in, lb_logits, hg_norm, mla_q_norm, mla_kv_norm, w_uq, w_ukv,
           w_a, w_b, w_o, norm_ffn, peer_wq, peer_k1, peer_k2, peer_u, peer_v, norm_ple, w_pg, w_pe,
           norm_final):
    b, t, d = x.shape
    depth = w_in.shape[0]
    assert b == 1 and depth == 1, "kernel supports one sequence and one layer"
    out = _layer(x.reshape(b * t, d), p[0].reshape(b * t, -1), positions.reshape(-1), 0,
                 norm_mix[0], w_in[0], lb_logits, hg_norm[0], mla_q_norm[0], mla_kv_norm[0],
                 w_uq[0], w_ukv[0], w_a[0], w_b[0], w_o[0], norm_ffn[0], peer_wq[0], peer_k1[0],
                 peer_k2[0], peer_u[0], peer_v[0], norm_ple[0], w_pg[0], w_pe[0], norm_final)
    return out.reshape(b, t, d)
```

```python
import functools

import jax
import jax.numpy as jnp
import numpy as np
from jax import lax
from jax.experimental import pallas as pl
from jax.experimental.pallas import tpu as pltpu

D_MODEL = 2048
HG_HEADS = 8
HG_DK = 128
HG_DV = 128
HG_WIDTH = HG_HEADS * HG_DK
HG_CHUNK = 64
MLA_HEADS = 8
MLA_NOPE = 128
MLA_ROPE = 64
MLA_V = 128
MLA_Q_LORA = 512
MLA_KV_LORA = 512
MLA_QK = MLA_NOPE + MLA_ROPE
ROPE_THETA = 10000.0
PEER_HEADS = 8
PEER_NKEYS = 128
PEER_QDIM = 256
PEER_TOPK = 16
NORM_EPS = 1e-6
MASK_VALUE = -1e30

LANES = 128
VMEM_LIMIT = 56 * 1024 * 1024

_NT = (((1,), (1,)), ((), ()))
_TN = (((0,), (0,)), ((), ()))

F32 = jnp.float32
BF16 = jnp.bfloat16


def _params(*sem):
    return pltpu.CompilerParams(dimension_semantics=sem, vmem_limit_bytes=VMEM_LIMIT)


def _rms(x, gain):
    return x * lax.rsqrt(jnp.mean(x * x, axis=-1, keepdims=True) + NORM_EPS) * gain


def _rmsnorm_kernel(x_ref, g_ref, wt_ref, o_ref, t_ref):
    hn = _rms(x_ref[...], g_ref[...]).astype(o_ref.dtype)
    o_ref[...] = hn
    t_ref[...] = lax.dot_general(hn, wt_ref[...], _NT, preferred_element_type=F32)


def _rmsnorm(x, gain, w_tail_t, tm=512):
    n, d = x.shape
    tw = w_tail_t.shape[0]
    return pl.pallas_call(
        _rmsnorm_kernel,
        grid=(n // tm,),
        in_specs=[pl.BlockSpec((tm, d), lambda i: (i, 0)), pl.BlockSpec((1, d), lambda i: (0, 0)),
                  pl.BlockSpec((tw, d), lambda i: (0, 0))],
        out_specs=[pl.BlockSpec((tm, d), lambda i: (i, 0)), pl.BlockSpec((tm, tw), lambda i: (i, 0))],
        out_shape=[jax.ShapeDtypeStruct((n, d), BF16), jax.ShapeDtypeStruct((n, tw), F32)],
        compiler_params=_params("parallel"),
        name="rmsnorm",
    )(x, gain.reshape(1, d), w_tail_t)


def _matmul_ws_kernel(a_ref, w_ref, o_ref, wb_ref):
    @pl.when(pl.program_id(1) == 0)
    def _():
        wb_ref[...] = w_ref[...].astype(BF16)

    o_ref[...] = lax.dot_general(a_ref[...], wb_ref[...], _NT, preferred_element_type=F32).astype(o_ref.dtype)


def _matmul_ws(a, w_t, cols, out_dtype, tm, tn, name):
    m, k = a.shape
    return pl.pallas_call(
        _matmul_ws_kernel,
        grid=(cols // tn, m // tm),
        in_specs=[pl.BlockSpec((tm, k), lambda j, i: (i, 0)), pl.BlockSpec((tn, k), lambda j, i: (j, 0))],
        out_specs=pl.BlockSpec((tm, tn), lambda j, i: (i, j)),
        out_shape=jax.ShapeDtypeStruct((m, cols), out_dtype),
        scratch_shapes=[pltpu.VMEM((tn, k), BF16)],
        compiler_params=_params("arbitrary", "arbitrary"),
        name=name,
    )(a, w_t)


HG_GROUP = 4


def _hgrn_kernel(q_ref, f_ref, i_ref, og_ref, lbl_ref, gain_ref, tri_ref, o_ref, st_ref, *, layer, chunks):
    @pl.when(pl.program_id(1) == 0)
    def _():
        st_ref[...] = jnp.zeros_like(st_ref)

    lbl = lbl_ref[...]
    e = jnp.exp(lbl - jnp.max(lbl, axis=0, keepdims=True))
    sm = e / jnp.sum(e, axis=0, keepdims=True)
    lb = jnp.sum(sm[: layer + 1], axis=0, keepdims=True)
    c = HG_CHUNK
    row = lax.broadcasted_iota(jnp.int32, (c, c), 0)
    col = lax.broadcasted_iota(jnp.int32, (c, c), 1)
    causal = row >= col

    q = q_ref[...]
    vb = i_ref[...].astype(BF16)
    f = lb + (1.0 - lb) * jax.nn.sigmoid(f_ref[...])
    g = jnp.log(f)
    k = 1.0 - f
    g_hi = g.astype(BF16)
    r1 = g - g_hi.astype(F32)
    g_mid = r1.astype(BF16)
    g_lo = (r1 - g_mid.astype(F32)).astype(BF16)
    gs = jnp.concatenate([g_hi, g_mid, g_lo], axis=1)
    gr = HG_GROUP * c
    parts = []
    for n in range(chunks // HG_GROUP):
        b3 = jnp.dot(tri_ref[...], gs[n * gr:(n + 1) * gr, :], preferred_element_type=F32)
        parts.append(b3[:, :HG_DK] + b3[:, HG_DK:2 * HG_DK] + b3[:, 2 * HG_DK:])
    b = jnp.concatenate(parts, axis=0)

    def rows_of(r):
        return jnp.concatenate([jnp.broadcast_to(b[n * c + r:n * c + r + 1, :], (c, HG_DK))
                                for n in range(chunks)], axis=0)

    b_mid = rows_of(c // 2 - 1)
    b_last = rows_of(c - 1)
    q_in = (q * jnp.exp(b - b_mid)).astype(BF16)
    k_in = (k * jnp.exp(b_mid - b)).astype(BF16)
    k_dec = (k * jnp.exp(b_last - b)).astype(BF16)
    qe = (q * jnp.exp(b)).astype(BF16)

    o_intra, dst = [], []
    for n in range(chunks):
        sl = slice(n * c, (n + 1) * c)
        a = lax.dot_general(q_in[sl], k_in[sl], _NT, preferred_element_type=F32)
        a = jnp.where(causal, a, 0.0).astype(BF16)
        o_intra.append(jnp.dot(a, vb[sl], preferred_element_type=F32))
        dst.append(lax.dot_general(vb[sl], k_dec[sl], _TN, preferred_element_type=F32))

    st = st_ref[...]
    outs = []
    for n in range(chunks):
        sl = slice(n * c, (n + 1) * c)
        outs.append(o_intra[n] + lax.dot_general(qe[sl], st.astype(BF16), _NT, preferred_element_type=F32))
        st = jnp.exp(b[(n + 1) * c - 1:(n + 1) * c, :]) * st + dst[n]
    st_ref[...] = st

    o = jnp.concatenate(outs, axis=0)
    og = og_ref[...]
    o = o * lax.rsqrt(jnp.mean(o * o, axis=-1, keepdims=True) + NORM_EPS) * gain_ref[...]
    o_ref[...] = (o * (og * jax.nn.sigmoid(og))).astype(o_ref.dtype)


def _hgrn(proj, lb_logits, gain, layer, tt=1024):
    n = proj.shape[0]
    h = HG_HEADS
    nl = lb_logits.shape[0]
    gr = HG_GROUP * HG_CHUNK
    idx = np.arange(gr)
    tri = ((idx[:, None] >= idx[None, :]) & (idx[:, None] // HG_CHUNK == idx[None, :] // HG_CHUNK))
    blk = lambda g: pl.BlockSpec((tt, HG_DK), lambda hh, t, g=g: (t, g * h + hh))
    return pl.pallas_call(
        functools.partial(_hgrn_kernel, layer=layer, chunks=tt // HG_CHUNK),
        grid=(h, n // tt),
        in_specs=[blk(0), blk(1), blk(2), blk(3),
                  pl.BlockSpec((nl, HG_DK), lambda hh, t: (0, hh)),
                  pl.BlockSpec((1, HG_DV), lambda hh, t: (0, hh)),
                  pl.BlockSpec((gr, gr), lambda hh, t: (0, 0))],
        out_specs=pl.BlockSpec((tt, HG_DV), lambda hh, t: (t, hh)),
        out_shape=jax.ShapeDtypeStruct((n, h * HG_DV), BF16),
        scratch_shapes=[pltpu.VMEM((HG_DV, HG_DK), F32)],
        compiler_params=_params("parallel", "arbitrary"),
        name="hgrn2",
    )(proj, proj, proj, proj, lb_logits, gain.reshape(1, -1), jnp.asarray(tri, BF16))


def _mla_prep_kernel(p_ref, pk_ref, pos_ref, fr_ref, qn_ref_g, kvn_ref_g, wq_ref, wk_ref, wvt_ref,
                     qn_ref, qr_ref, kn_ref, vt_ref, kr_ref, *, scale):
    hw = MLA_HEADS * MLA_NOPE
    half = MLA_ROPE // 2
    ang = pos_ref[...].astype(F32) * fr_ref[...]
    lane = lax.broadcasted_iota(jnp.int32, ang.shape, 1)
    cosv = jnp.where(lane < MLA_ROPE, jnp.cos(ang), 0.0)
    sinv = jnp.where(lane < MLA_ROPE, jnp.sin(ang), 0.0)

    def rope(y):
        return y * cosv + pltpu.roll(y, MLA_ROPE, 1) * sinv

    cq = _rms(p_ref[:, :MLA_Q_LORA], qn_ref_g[...]).astype(BF16)
    qres = jnp.dot(cq, wq_ref[...], preferred_element_type=F32)
    qn_ref[...] = (qres[:, :hw] * scale).astype(qn_ref.dtype)
    for h in range(MLA_HEADS):
        y = qres[:, hw + h * LANES: hw + (h + 1) * LANES]
        qr_ref[:, h * LANES:(h + 1) * LANES] = (rope(y) * scale).astype(qr_ref.dtype)
    ckv = _rms(p_ref[:, MLA_Q_LORA:MLA_Q_LORA + MLA_KV_LORA], kvn_ref_g[...]).astype(BF16)
    kn_ref[...] = jnp.dot(ckv, wk_ref[...], preferred_element_type=F32).astype(kn_ref.dtype)
    vt_ref[...] = lax.dot_general(wvt_ref[...], ckv, _NT, preferred_element_type=F32).astype(vt_ref.dtype)
    y = pk_ref[...]
    rot = jnp.where(lane < half, -pltpu.roll(y, LANES - half, 1), pltpu.roll(y, half, 1))
    kr_ref[...] = (y * cosv + rot * sinv).astype(kr_ref.dtype)


def _mla_prep(proj, positions, q_norm, kv_norm, wq, wk, wvt, tm=512):
    n = proj.shape[0]
    pw = MLA_Q_LORA + MLA_KV_LORA
    lat_blk = 4 * HG_WIDTH // pw
    kr_blk = (4 * HG_WIDTH + pw) // LANES
    hw = MLA_HEADS * MLA_NOPE
    half = MLA_ROPE // 2
    freqs = ROPE_THETA ** (-jnp.arange(half, dtype=F32) / half)
    fr = jnp.concatenate([freqs, freqs, jnp.zeros((LANES - MLA_ROPE,), F32)]).reshape(1, LANES)
    scale = float(1.0 / np.sqrt(MLA_QK).astype(np.float32)) * float(np.log2(np.e))
    full = lambda shape: pl.BlockSpec(shape, lambda i: (0, 0))
    tok = lambda w: pl.BlockSpec((tm, w), lambda i: (i, 0))
    return pl.pallas_call(
        functools.partial(_mla_prep_kernel, scale=scale),
        grid=(n // tm,),
        in_specs=[pl.BlockSpec((tm, pw), lambda i: (i, lat_blk)), pl.BlockSpec((tm, LANES), lambda i: (i, kr_blk)),
                  tok(1), full((1, LANES)), full((1, MLA_Q_LORA)), full((1, MLA_KV_LORA)),
                  full(wq.shape), full(wk.shape), full(wvt.shape)],
        out_specs=[tok(hw), tok(MLA_HEADS * LANES), tok(hw),
                   pl.BlockSpec((MLA_HEADS * MLA_V, tm), lambda i: (0, i)), tok(LANES)],
        out_shape=[jax.ShapeDtypeStruct((n, hw), BF16),
                   jax.ShapeDtypeStruct((n, MLA_HEADS * LANES), BF16),
                   jax.ShapeDtypeStruct((n, hw), BF16),
                   jax.ShapeDtypeStruct((MLA_HEADS * MLA_V, n), BF16),
                   jax.ShapeDtypeStruct((n, LANES), BF16)],
        compiler_params=_params("parallel"),
        name="mla_prep",
    )(proj, proj, positions.reshape(n, 1), fr, q_norm.reshape(1, -1), kv_norm.reshape(1, -1), wq, wk, wvt)


QSUB = 256


def _attn_kernel(qi_ref, kj_ref, qn_ref, qr_ref, kn_ref, kr_ref, vt_ref, o_ref, m_ref, l_ref, acc_ref, *, blk, hpb):
    s_id = pl.program_id(1)
    qi = qi_ref[s_id]
    kj = kj_ref[s_id]

    @pl.when(kj == 0)
    def _():
        m_ref[...] = jnp.full_like(m_ref, -jnp.inf)
        l_ref[...] = jnp.zeros_like(l_ref)
        acc_ref[...] = jnp.zeros_like(acc_ref)

    def update(masked):
        nsub = blk // QSUB
        items = [(hh, c) for hh in range(hpb) for c in range(nsub)]

        def nkeys(c):
            return (c + 1) * QSUB if masked else blk

        def scores(item):
            hh, c = item
            hl = slice(hh * LANES, (hh + 1) * LANES)
            sl = pl.ds(c * QSUB, QSUB)
            k = jnp.concatenate([kn_ref[0:nkeys(c), hl], kr_ref[0:nkeys(c), :]], axis=1)
            q = jnp.concatenate([qn_ref[sl, hl], qr_ref[sl, hl]], axis=1)
            return lax.dot_general(k, q, _NT, preferred_element_type=F32)

        sts = [scores(it) for it in items[:2]]
        for n, (hh, c) in enumerate(items):
            sl = pl.ds(c * QSUB, QSUB)
            st = sts[n]
            if n + 2 < len(items):
                sts.append(scores(items[n + 2]))
            if masked:
                key = lax.broadcasted_iota(jnp.int32, st.shape, 0)
                qry = lax.broadcasted_iota(jnp.int32, st.shape, 1) + c * QSUB
                st = jnp.where(key <= qry, st, MASK_VALUE)
            m_prev = m_ref[hh, :, sl]
            m_new = jnp.maximum(m_prev, jnp.max(st, axis=0, keepdims=True))
            alpha = jnp.exp2(m_prev - m_new)
            p = jnp.exp2(st - m_new)
            l_ref[hh, :, sl] = alpha * l_ref[hh, :, sl] + jnp.sum(p, axis=0, keepdims=True)
            pv = jnp.dot(vt_ref[hh * MLA_V:(hh + 1) * MLA_V, 0:nkeys(c)], p.astype(BF16),
                         preferred_element_type=F32)
            acc_ref[hh, :, sl] = alpha * acc_ref[hh, :, sl] + pv
            m_ref[hh, :, sl] = m_new

    @pl.when(kj < qi)
    def _():
        update(False)

    @pl.when(kj == qi)
    def _():
        update(True)
        for hh in range(hpb):
            o_ref[:, hh * MLA_V:(hh + 1) * MLA_V] = (acc_ref[hh] / l_ref[hh]).T.astype(o_ref.dtype)


def _attention(qn, qr, kn, kr, vt, blk=1024, hpb=4):
    n = qn.shape[0]
    nb = n // blk
    qi = np.array([i for i in range(nb) for j in range(i + 1)], np.int32)
    kj = np.array([j for i in range(nb) for j in range(i + 1)], np.int32)
    qspec = pl.BlockSpec((blk, hpb * LANES), lambda h, s, qi, kj: (qi[s], h))
    kspec = pl.BlockSpec((blk, hpb * LANES), lambda h, s, qi, kj: (kj[s], h))
    krspec = pl.BlockSpec((blk, LANES), lambda h, s, qi, kj: (kj[s], 0))
    vtspec = pl.BlockSpec((hpb * MLA_V, blk), lambda h, s, qi, kj: (h, kj[s]))
    grid_spec = pltpu.PrefetchScalarGridSpec(
        num_scalar_prefetch=2,
        grid=(MLA_HEADS // hpb, len(qi)),
        in_specs=[qspec, qspec, kspec, krspec, vtspec],
        out_specs=qspec,
        scratch_shapes=[pltpu.VMEM((hpb, 1, blk), F32), pltpu.VMEM((hpb, 1, blk), F32),
                        pltpu.VMEM((hpb, MLA_V, blk), F32)],
    )
    return pl.pallas_call(
        functools.partial(_attn_kernel, blk=blk, hpb=hpb),
        grid_spec=grid_spec,
        out_shape=jax.ShapeDtypeStruct((n, MLA_HEADS * MLA_V), BF16),
        compiler_params=_params("parallel", "arbitrary"),
        name="mla_attention",
    )(jnp.asarray(qi), jnp.asarray(kj), qn, qr, kn, kr, vt)


def _merge_kernel(oa_ref, ob_ref, g0_ref, g1_ref, g2_ref, g3_ref, gt_ref, x_ref, wa_ref, wb_ref, wo_ref,
                  g_ref, x1_ref, hn_ref, *, off):
    d = x_ref.shape[1]
    ya = jnp.dot(oa_ref[...], wa_ref[...], preferred_element_type=F32)
    yb = jnp.dot(ob_ref[...], wb_ref[...], preferred_element_type=F32)
    gates = jnp.concatenate([g0_ref[...], g1_ref[...], g2_ref[...], g3_ref[...], gt_ref[...]], axis=1)
    ga = gates[:, off:off + d]
    gb = gates[:, off + d:off + 2 * d]
    y = jax.nn.sigmoid(ga) * ya + jax.nn.sigmoid(gb) * yb
    x1 = x_ref[...] + jnp.dot(y.astype(BF16), wo_ref[...], preferred_element_type=F32)
    x1_ref[...] = x1
    hn_ref[...] = _rms(x1, g_ref[...]).astype(hn_ref.dtype)


def _merge(oa, ob, proj, tail, g0, x, wa, wb, wo, gain, tm=256):
    n, d = x.shape
    gw = d // 2
    base = (g0 // gw) * gw
    assert base + 4 * gw == proj.shape[1] and g0 + 2 * d <= base + 4 * gw + LANES
    tok = lambda w, c=0: pl.BlockSpec((tm, w), lambda i, c=c: (i, c))
    res = lambda shape: pl.BlockSpec(shape, lambda i: (0, 0), pipeline_mode=pl.Buffered(1))
    return pl.pallas_call(
        functools.partial(_merge_kernel, off=g0 - base),
        grid=(n // tm,),
        in_specs=[tok(oa.shape[1]), tok(ob.shape[1])] + [tok(gw, base // gw + c) for c in range(4)]
                 + [tok(LANES), tok(d), res(wa.shape), res(wb.shape), res(wo.shape), res((1, d))],
        out_specs=[tok(d), tok(d)],
        out_shape=[jax.ShapeDtypeStruct((n, d), F32), jax.ShapeDtypeStruct((n, d), BF16)],
        compiler_params=_params("parallel"),
        name="merge_outproj",
    )(oa, ob, proj, proj, proj, proj, tail, x, wa, wb, wo, gain.reshape(1, d))


R_TM = 1024
R_BLOCKS = R_TM // LANES
R_PITCH = PEER_NKEYS + 8


def _first_max(vals, *payloads):
    payloads = [list(p) for p in payloads]
    while len(vals) > 1:
        nv, npl = [], [[] for _ in payloads]
        for i in range(0, len(vals) - 1, 2):
            take = vals[i] >= vals[i + 1]
            nv.append(jnp.maximum(vals[i], vals[i + 1]))
            for dst, p in zip(npl, payloads):
                dst.append(jnp.where(take, p[i], p[i + 1]))
        if len(vals) % 2:
            nv.append(vals[-1])
            for dst, p in zip(npl, payloads):
                dst.append(p[-1])
        vals, payloads = nv, npl
    return (vals[0], *[p[0] for p in payloads])


TOP_GROUP = 8


def _sort_group(vals, ids):
    vals, ids = list(vals), list(ids)
    n = len(vals)
    for rnd in range(n):
        for p in range(rnd % 2, n - 1, 2):
            take = vals[p] >= vals[p + 1]
            hi, lo = jnp.maximum(vals[p], vals[p + 1]), jnp.minimum(vals[p], vals[p + 1])
            ids[p], ids[p + 1] = jnp.where(take, ids[p], ids[p + 1]), jnp.where(take, ids[p + 1], ids[p])
            vals[p], vals[p + 1] = hi, lo
    return vals, ids


def _pop_rounds(lists, payloads, k):
    lists = [list(g) for g in lists]
    payloads = [[list(g) for g in p] for p in payloads]
    gids = [float(g) for g in range(len(lists))]
    vals, outs = [], [[] for _ in payloads]
    for j in range(k):
        m, win, *heads = _first_max([g[0] for g in lists], gids, *[[g[0] for g in p] for p in payloads])
        vals.append(m)
        for o, hd in zip(outs, heads):
            o.append(hd)
        for g, lst in enumerate(lists):
            depth = min(len(lst) - 1, k - 1 - j)
            pop = win == gids[g]
            for r in range(depth):
                lst[r] = jnp.where(pop, lst[r + 1], lst[r])
                for p in payloads:
                    p[g][r] = jnp.where(pop, p[g][r + 1], p[g][r])
            if depth == len(lst) - 1:
                lst[-1] = jnp.where(pop, -jnp.inf, lst[-1])
    return vals, outs


def _top_keys(x, k):
    gs = TOP_GROUP
    lv, li = [], []
    for g in range(len(x) // gs):
        v, i = _sort_group(x[g * gs:(g + 1) * gs], [float(j) for j in range(g * gs, (g + 1) * gs)])
        lv.append(v)
        li.append(i)
    vals, (idxs,) = _pop_rounds(lv, [li], k)
    return vals, idxs


def _top_pairs(v1, i1, v2, i2, k):
    ranks = [[b for b in range(k) if (a + 1) * (b + 1) <= k] for a in range(k)]
    sums = [[v1[a] + v2[b] for b in bs] for a, bs in enumerate(ranks)]
    ids = [[i1[a] * float(PEER_NKEYS) + i2[b] for b in bs] for a, bs in enumerate(ranks)]
    vals, (sel,) = _pop_rounds(sums, [ids], k)
    return vals, sel


def _split_bf16(x):
    hi = x.astype(BF16)
    return hi, (x - hi.astype(F32)).astype(BF16)


def _route_kernel(h_ref, wq_ref, k1_ref, k2_ref, e_ref, g_ref, s1_ref, s2_ref, *, hps):
    k = PEER_TOPK
    half = PEER_QDIM // 2
    h = h_ref[...]

    def scores(hh):
        qt = lax.dot_general(wq_ref[hh * PEER_QDIM:(hh + 1) * PEER_QDIM, :], h, _NT, preferred_element_type=F32)
        out = []
        for part, k_ref in ((0, k1_ref), (1, k2_ref)):
            q_hi, q_lo = _split_bf16(qt[part * half:(part + 1) * half])
            qs = jnp.concatenate([q_hi, q_lo, q_hi], axis=0)
            out.append(jnp.dot(k_ref[...], qs, preferred_element_type=F32))
        return out

    def by_key(s, s_ref):
        for j in range(R_BLOCKS):
            s_ref[j * R_PITCH:j * R_PITCH + PEER_NKEYS, :] = s[:, j * LANES:(j + 1) * LANES]
        return [s_ref[pl.ds(key, R_BLOCKS, stride=R_PITCH), :] for key in range(PEER_NKEYS)]

    nxt = scores(0)
    for hh in range(hps):
        s1, s2 = nxt
        if hh + 1 < hps:
            nxt = scores(hh + 1)
        v1, i1 = _top_keys(by_key(s1, s1_ref), k)
        v2, i2 = _top_keys(by_key(s2, s2_ref), k)
        cv, ce = _top_pairs(v1, i1, v2, i2, k)
        ex = [jnp.exp(c - cv[0]) for c in cv]
        den = ex[0]
        for x in ex[1:]:
            den = den + x
        for j in range(k):
            g_ref[0, hh * k + j] = ex[j] / den
            e_ref[0, hh * k + j] = ce[j].astype(jnp.int32)


def _route(hn, wq_t, k1, k2, hps=2):
    n, d = hn.shape
    k = PEER_TOPK
    tiles = n // R_TM

    def stack(kk):
        hi, lo = _split_bf16(kk)
        return jnp.concatenate([hi, hi, lo], axis=1)

    full = lambda shape: pl.BlockSpec(shape, lambda i, g: (0, 0))
    out = pl.BlockSpec((1, hps * k, R_BLOCKS, LANES), lambda i, g: (i, g, 0, 0))
    sc = lambda: pltpu.VMEM((R_BLOCKS * R_PITCH, LANES), F32)
    return pl.pallas_call(
        functools.partial(_route_kernel, hps=hps),
        grid=(tiles, PEER_HEADS // hps),
        in_specs=[pl.BlockSpec((R_TM, d), lambda i, g: (i, 0)),
                  pl.BlockSpec((hps * PEER_QDIM, d), lambda i, g: (g, 0)),
                  full((PEER_NKEYS, 3 * (PEER_QDIM // 2))), full((PEER_NKEYS, 3 * (PEER_QDIM // 2)))],
        out_specs=[out, out],
        out_shape=[jax.ShapeDtypeStruct((tiles, PEER_HEADS * k, R_BLOCKS, LANES), jnp.int32),
                   jax.ShapeDtypeStruct((tiles, PEER_HEADS * k, R_BLOCKS, LANES), F32)],
        scratch_shapes=[sc(), sc()],
        compiler_params=_params("parallel", "arbitrary"),
        name="peer_route",
    )(hn, wq_t, stack(k1), stack(k2))


W_TILE = 128
W_PITCH = W_TILE + 8


def _wbuild_kernel(e_ref, g_ref, w_ref, buf_ref):
    nk = PEER_NKEYS
    iota = lax.broadcasted_iota(jnp.int32, (nk, LANES), 0)

    def body(t, carry):
        e = e_ref[pl.ds(t, 1), :]
        g = g_ref[pl.ds(t, 1), :]
        p1 = jnp.where(iota == (e >> 7), g, 0.0).astype(BF16)
        p2 = jnp.where(iota == (e & (nk - 1)), 1.0, 0.0).astype(BF16)
        wt = lax.dot_general(p1, p2, _NT, preferred_element_type=F32)
        buf_ref[pl.ds(t, nk, stride=W_PITCH), :] = wt
        return carry

    lax.fori_loop(0, W_TILE, body, 0, unroll=32)
    for i in range(nk):
        w_ref[0, i] = buf_ref[i * W_PITCH:i * W_PITCH + W_TILE, :].astype(w_ref.dtype)


def _wbuild(e, g):
    n, pairs = e.shape
    nk = PEER_NKEYS
    tiles = n // W_TILE
    return pl.pallas_call(
        _wbuild_kernel,
        grid=(tiles,),
        in_specs=[pl.BlockSpec((W_TILE, pairs), lambda i: (i, 0)), pl.BlockSpec((W_TILE, pairs), lambda i: (i, 0))],
        out_specs=pl.BlockSpec((1, nk, W_TILE, nk), lambda i: (i, 0, 0, 0)),
        out_shape=jax.ShapeDtypeStruct((tiles, nk, W_TILE, nk), BF16),
        scratch_shapes=[pltpu.VMEM((nk * W_PITCH, nk), F32)],
        compiler_params=_params("parallel"),
        name="peer_wbuild",
    )(e, g)


def _peer_kernel(h_ref, u_ref, v_ref, w_ref, o_ref, *, groups, tiles):
    @pl.when(pl.program_id(1) == 0)
    def _():
        o_ref[...] = jnp.zeros_like(o_ref)

    z = lax.dot_general(h_ref[...], u_ref[...].astype(BF16), _NT, preferred_element_type=F32)
    rows = []
    for k in range(tiles):
        cols = []
        for i in range(groups):
            zi = z[k * W_TILE:(k + 1) * W_TILE, i * LANES:(i + 1) * LANES]
            gelu = 0.5 * zi * (1.0 + lax.erf(zi * np.float32(np.sqrt(0.5))))
            cols.append((w_ref[k, i].astype(F32) * gelu).astype(BF16))
        rows.append(jnp.concatenate(cols, axis=1))
    a = jnp.concatenate(rows, axis=0)
    o_ref[...] += jnp.dot(a, v_ref[...].astype(BF16), preferred_element_type=F32)


def _peer(hn, u, v, w, tm=1024, te=512):
    n, d = hn.shape
    ne = u.shape[0]
    groups = te // PEER_NKEYS
    tiles = tm // W_TILE
    return pl.pallas_call(
        functools.partial(_peer_kernel, groups=groups, tiles=tiles),
        grid=(n // tm, ne // te),
        in_specs=[pl.BlockSpec((tm, d), lambda i, c: (i, 0)),
                  pl.BlockSpec((te, d), lambda i, c: (c, 0)),
                  pl.BlockSpec((te, d), lambda i, c: (c, 0)),
                  pl.BlockSpec((tiles, groups, W_TILE, PEER_NKEYS), lambda i, c: (i, c, 0, 0))],
        out_specs=pl.BlockSpec((tm, d), lambda i, c: (i, 0)),
        out_shape=jax.ShapeDtypeStruct((n, d), F32),
        compiler_params=_params("parallel", "arbitrary"),
        name="peer_experts",
    )(hn, u, v, w)


def _ple_kernel(x1_ref, po_ref, p_ref, wpg_ref, wpe_ref, gp_ref, gf_ref, o_ref):
    x2 = x1_ref[...] + po_ref[...]
    hp = _rms(x2, gp_ref[...]).astype(BF16)
    gate = jax.nn.sigmoid(jnp.dot(hp, wpg_ref[...], preferred_element_type=F32))
    pe = jnp.dot(p_ref[...], wpe_ref[...], preferred_element_type=F32)
    x3 = x2 + gate * pe
    o_ref[...] = _rms(x3, gf_ref[...]).astype(o_ref.dtype)


def _ple(x1, po, p, wpg, wpe, g_ple, g_final, tm=512):
    n, d = x1.shape
    tok = lambda w: pl.BlockSpec((tm, w), lambda i: (i, 0))
    res = lambda shape: pl.BlockSpec(shape, lambda i: (0, 0), pipeline_mode=pl.Buffered(1))
    return pl.pallas_call(
        _ple_kernel,
        grid=(n // tm,),
        in_specs=[tok(d), tok(d), tok(p.shape[1]), res(wpg.shape), res(wpe.shape), res((1, d)), res((1, d))],
        out_specs=tok(d),
        out_shape=jax.ShapeDtypeStruct((n, d), F32),
        compiler_params=_params("parallel"),
        name="ple_final",
    )(x1, po, p, wpg, wpe, g_ple.reshape(1, d), g_final.reshape(1, d))


def _layer(x, p, positions, layer, norm_mix, w_in, lb_logits, hg_norm, mla_q_norm, mla_kv_norm,
           w_uq, w_ukv, w_a, w_b, w_o, norm_ffn, peer_wq, peer_k1, peer_k2, peer_u, peer_v,
           norm_ple, w_pg, w_pe, norm_out):
    n, d = x.shape
    hgw = 4 * HG_WIDTH
    mla0 = hgw
    kr0 = mla0 + MLA_Q_LORA + MLA_KV_LORA
    g0 = kr0 + MLA_ROPE
    half = MLA_ROPE // 2

    uq = w_uq.reshape(MLA_Q_LORA, MLA_HEADS, MLA_QK)
    uq_r = uq[:, :, MLA_NOPE:]
    uq_rot = jnp.concatenate([-uq_r[:, :, half:], uq_r[:, :, :half]], axis=2)
    wq = jnp.concatenate([uq[:, :, :MLA_NOPE].reshape(MLA_Q_LORA, -1),
                          jnp.concatenate([uq_r, uq_rot], axis=2).reshape(MLA_Q_LORA, -1)],
                         axis=1).astype(BF16)
    ukv = w_ukv.reshape(MLA_KV_LORA, MLA_HEADS, MLA_NOPE + MLA_V)
    wk = ukv[:, :, :MLA_NOPE].reshape(MLA_KV_LORA, -1).astype(BF16)
    wvt = ukv[:, :, MLA_NOPE:].reshape(MLA_KV_LORA, -1).T.astype(BF16)

    tn = 1024
    w_in_t = w_in.T
    main_cols = (w_in_t.shape[0] // tn) * tn
    w_tail_t = w_in_t[main_cols:]
    w_tail_t = jnp.pad(w_tail_t, ((0, LANES - w_tail_t.shape[0]), (0, 0))).astype(BF16)
    hn, tail = _rmsnorm(x, norm_mix, w_tail_t)
    proj = _matmul_ws(hn, w_in_t, main_cols, F32, 1024, tn, "in_proj_main")

    o_a = _hgrn(proj, lb_logits, hg_norm, layer)
    qn, qr, kn, vt, kr = _mla_prep(proj, positions, mla_q_norm, mla_kv_norm, wq, wk, wvt)
    o_b = _attention(qn, qr, kn, kr, vt)
    x1, hn2 = _merge(o_a, o_b, proj, tail, g0, x, w_a.astype(BF16), w_b.astype(BF16), w_o.astype(BF16),
                     norm_ffn)

    e4, g4 = _route(hn2, peer_wq.T.astype(BF16), peer_k1, peer_k2)
    by_token = lambda a: a.transpose(0, 2, 3, 1).reshape(n, a.shape[1])
    w = _wbuild(by_token(e4), by_token(g4))
    po = _peer(hn2, peer_u, peer_v, w)
    return _ple(x1, po, p.astype(BF16), w_pg.astype(BF16), w_pe.astype(BF16), norm_ple, norm_out)


def kernel(x, p, positions, norm_mix, w_in, lb_logits, hg_norm, mla_q_norm, mla_kv_norm, w_uq, w_ukv,
           w_a, w_b, w_o, norm_ffn, peer_wq, peer_k1, peer_k2, peer_u, peer_v, norm_ple, w_pg, w_pe,
           norm_final):
    b, t, d = x.shape
    depth = w_in.shape[0]
    assert b == 1 and depth == 1, "kernel supports one sequence and one layer"
    out = _layer(x.reshape(b * t, d), p[0].reshape(b * t, -1), positions.reshape(-1), 0,
                 norm_mix[0], w_in[0], lb_logits, hg_norm[0], mla_q_norm[0], mla_kv_norm[0],
                 w_uq[0], w_ukv[0], w_a[0], w_b[0], w_o[0], norm_ffn[0], peer_wq[0], peer_k1[0],
                 peer_k2[0], peer_u[0], peer_v[0], norm_ple[0], w_pg[0], w_pe[0], norm_final)
    return out.reshape(b, t, d)
```

```python
import functools

import jax
import jax.numpy as jnp
import numpy as np
from jax import lax
from jax.experimental import pallas as pl
from jax.experimental.pallas import tpu as pltpu

D_MODEL = 2048
HG_HEADS = 8
HG_DK = 128
HG_DV = 128
HG_WIDTH = HG_HEADS * HG_DK
HG_CHUNK = 64
MLA_HEADS = 8
MLA_NOPE = 128
MLA_ROPE = 64
MLA_V = 128
MLA_Q_LORA = 512
MLA_KV_LORA = 512
MLA_QK = MLA_NOPE + MLA_ROPE
ROPE_THETA = 10000.0
PEER_HEADS = 8
PEER_NKEYS = 128
PEER_QDIM = 256
PEER_TOPK = 16
NORM_EPS = 1e-6
MASK_VALUE = -1e30

LANES = 128
VMEM_LIMIT = 56 * 1024 * 1024

_NT = (((1,), (1,)), ((), ()))
_TN = (((0,), (0,)), ((), ()))

F32 = jnp.float32
BF16 = jnp.bfloat16


def _params(*sem):
    return pltpu.CompilerParams(dimension_semantics=sem, vmem_limit_bytes=VMEM_LIMIT)


def _rms(x, gain):
    return x * lax.rsqrt(jnp.mean(x * x, axis=-1, keepdims=True) + NORM_EPS) * gain


def _rmsnorm_kernel(x_ref, g_ref, wt_ref, o_ref, t_ref):
    hn = _rms(x_ref[...], g_ref[...]).astype(o_ref.dtype)
    o_ref[...] = hn
    t_ref[...] = lax.dot_general(hn, wt_ref[...], _NT, preferred_element_type=F32)


def _rmsnorm(x, gain, w_tail_t, tm=512):
    n, d = x.shape
    tw = w_tail_t.shape[0]
    return pl.pallas_call(
        _rmsnorm_kernel,
        grid=(n // tm,),
        in_specs=[pl.BlockSpec((tm, d), lambda i: (i, 0)), pl.BlockSpec((1, d), lambda i: (0, 0)),
                  pl.BlockSpec((tw, d), lambda i: (0, 0))],
        out_specs=[pl.BlockSpec((tm, d), lambda i: (i, 0)), pl.BlockSpec((tm, tw), lambda i: (i, 0))],
        out_shape=[jax.ShapeDtypeStruct((n, d), BF16), jax.ShapeDtypeStruct((n, tw), F32)],
        compiler_params=_params("parallel"),
        name="rmsnorm",
    )(x, gain.reshape(1, d), w_tail_t)


def _matmul_ws_kernel(a_ref, w_ref, o_ref, wb_ref):
    @pl.when(pl.program_id(1) == 0)
    def _():
        wb_ref[...] = w_ref[...].astype(BF16)

    o_ref[...] = lax.dot_general(a_ref[...], wb_ref[...], _NT, preferred_element_type=F32).astype(o_ref.dtype)


def _matmul_ws(a, w_t, cols, out_dtype, tm, tn, name):
    m, k = a.shape
    return pl.pallas_call(
        _matmul_ws_kernel,
        grid=(cols // tn, m // tm),
        in_specs=[pl.BlockSpec((tm, k), lambda j, i: (i, 0)), pl.BlockSpec((tn, k), lambda j, i: (j, 0))],
        out_specs=pl.BlockSpec((tm, tn), lambda j, i: (i, j)),
        out_shape=jax.ShapeDtypeStruct((m, cols), out_dtype),
        scratch_shapes=[pltpu.VMEM((tn, k), BF16)],
        compiler_params=_params("arbitrary", "arbitrary"),
        name=name,
    )(a, w_t)


HG_GROUP = 4


def _hgrn_kernel(q_ref, f_ref, i_ref, og_ref, lbl_ref, gain_ref, tri_ref, o_ref, st_ref, *, layer, chunks):
    @pl.when(pl.program_id(1) == 0)
    def _():
        st_ref[...] = jnp.zeros_like(st_ref)

    lbl = lbl_ref[...]
    e = jnp.exp(lbl - jnp.max(lbl, axis=0, keepdims=True))
    sm = e / jnp.sum(e, axis=0, keepdims=True)
    lb = jnp.sum(sm[: layer + 1], axis=0, keepdims=True)
    c = HG_CHUNK
    row = lax.broadcasted_iota(jnp.int32, (c, c), 0)
    col = lax.broadcasted_iota(jnp.int32, (c, c), 1)
    causal = row >= col

    q = q_ref[...]
    vb = i_ref[...].astype(BF16)
    f = lb + (1.0 - lb) * jax.nn.sigmoid(f_ref[...])
    g = jnp.log(f)
    k = 1.0 - f
    g_hi = g.astype(BF16)
    r1 = g - g_hi.astype(F32)
    g_mid = r1.astype(BF16)
    g_lo = (r1 - g_mid.astype(F32)).astype(BF16)
    gs = jnp.concatenate([g_hi, g_mid, g_lo], axis=1)
    gr = HG_GROUP * c
    parts = []
    for n in range(chunks // HG_GROUP):
        b3 = jnp.dot(tri_ref[...], gs[n * gr:(n + 1) * gr, :], preferred_element_type=F32)
        parts.append(b3[:, :HG_DK] + b3[:, HG_DK:2 * HG_DK] + b3[:, 2 * HG_DK:])
    b = jnp.concatenate(parts, axis=0)

    def rows_of(r):
        return jnp.concatenate([jnp.broadcast_to(b[n * c + r:n * c + r + 1, :], (c, HG_DK))
                                for n in range(chunks)], axis=0)

    b_mid = rows_of(c // 2 - 1)
    b_last = rows_of(c - 1)
    q_in = (q * jnp.exp(b - b_mid)).astype(BF16)
    k_in = (k * jnp.exp(b_mid - b)).astype(BF16)
    k_dec = (k * jnp.exp(b_last - b)).astype(BF16)
    qe = (q * jnp.exp(b)).astype(BF16)

    o_intra, dst = [], []
    for n in range(chunks):
        sl = slice(n * c, (n + 1) * c)
        a = lax.dot_general(q_in[sl], k_in[sl], _NT, preferred_element_type=F32)
        a = jnp.where(causal, a, 0.0).astype(BF16)
        o_intra.append(jnp.dot(a, vb[sl], preferred_element_type=F32))
        dst.append(lax.dot_general(vb[sl], k_dec[sl], _TN, preferred_element_type=F32))

    st = st_ref[...]
    outs = []
    for n in range(chunks):
        sl = slice(n * c, (n + 1) * c)
        outs.append(o_intra[n] + lax.dot_general(qe[sl], st.astype(BF16), _NT, preferred_element_type=F32))
        st = jnp.exp(b[(n + 1) * c - 1:(n + 1) * c, :]) * st + dst[n]
    st_ref[...] = st

    o = jnp.concatenate(outs, axis=0)
    og = og_ref[...]
    o = o * lax.rsqrt(jnp.mean(o * o, axis=-1, keepdims=True) + NORM_EPS) * gain_ref[...]
    o_ref[...] = (o * (og * jax.nn.sigmoid(og))).astype(o_ref.dtype)


def _hgrn(proj, lb_logits, gain, layer, tt=2048):
    n = proj.shape[0]
    h = HG_HEADS
    nl = lb_logits.shape[0]
    gr = HG_GROUP * HG_CHUNK
    idx = np.arange(gr)
    tri = ((idx[:, None] >= idx[None, :]) & (idx[:, None] // HG_CHUNK == idx[None, :] // HG_CHUNK))
    blk = lambda g: pl.BlockSpec((tt, HG_DK), lambda hh, t, g=g: (t, g * h + hh))
    return pl.pallas_call(
        functools.partial(_hgrn_kernel, layer=layer, chunks=tt // HG_CHUNK),
        grid=(h, n // tt),
        in_specs=[blk(0), blk(1), blk(2), blk(3),
                  pl.BlockSpec((nl, HG_DK), lambda hh, t: (0, hh)),
                  pl.BlockSpec((1, HG_DV), lambda hh, t: (0, hh)),
                  pl.BlockSpec((gr, gr), lambda hh, t: (0, 0))],
        out_specs=pl.BlockSpec((tt, HG_DV), lambda hh, t: (t, hh)),
        out_shape=jax.ShapeDtypeStruct((n, h * HG_DV), BF16),
        scratch_shapes=[pltpu.VMEM((HG_DV, HG_DK), F32)],
        compiler_params=_params("parallel", "arbitrary"),
        name="hgrn2",
    )(proj, proj, proj, proj, lb_logits, gain.reshape(1, -1), jnp.asarray(tri, BF16))


def _mla_prep_kernel(p_ref, pk_ref, pos_ref, fr_ref, qn_ref_g, kvn_ref_g, wq_ref, wk_ref, wvt_ref,
                     qn_ref, qr_ref, kn_ref, vt_ref, kr_ref, *, scale):
    hw = MLA_HEADS * MLA_NOPE
    half = MLA_ROPE // 2
    ang = pos_ref[...].astype(F32) * fr_ref[...]
    lane = lax.broadcasted_iota(jnp.int32, ang.shape, 1)
    cosv = jnp.where(lane < MLA_ROPE, jnp.cos(ang), 0.0)
    sinv = jnp.where(lane < MLA_ROPE, jnp.sin(ang), 0.0)

    def rope(y):
        return y * cosv + pltpu.roll(y, MLA_ROPE, 1) * sinv

    cq = _rms(p_ref[:, :MLA_Q_LORA], qn_ref_g[...]).astype(BF16)
    qres = jnp.dot(cq, wq_ref[...], preferred_element_type=F32)
    qn_ref[...] = (qres[:, :hw] * scale).astype(qn_ref.dtype)
    for h in range(MLA_HEADS):
        y = qres[:, hw + h * LANES: hw + (h + 1) * LANES]
        qr_ref[:, h * LANES:(h + 1) * LANES] = (rope(y) * scale).astype(qr_ref.dtype)
    ckv = _rms(p_ref[:, MLA_Q_LORA:MLA_Q_LORA + MLA_KV_LORA], kvn_ref_g[...]).astype(BF16)
    kn_ref[...] = jnp.dot(ckv, wk_ref[...], preferred_element_type=F32).astype(kn_ref.dtype)
    vt_ref[...] = lax.dot_general(wvt_ref[...], ckv, _NT, preferred_element_type=F32).astype(vt_ref.dtype)
    y = pk_ref[...]
    rot = jnp.where(lane < half, -pltpu.roll(y, LANES - half, 1), pltpu.roll(y, half, 1))
    kr_ref[...] = (y * cosv + rot * sinv).astype(kr_ref.dtype)


def _mla_prep(proj, positions, q_norm, kv_norm, wq, wk, wvt, tm=512):
    n = proj.shape[0]
    pw = MLA_Q_LORA + MLA_KV_LORA
    lat_blk = 4 * HG_WIDTH // pw
    kr_blk = (4 * HG_WIDTH + pw) // LANES
    hw = MLA_HEADS * MLA_NOPE
    half = MLA_ROPE // 2
    freqs = ROPE_THETA ** (-jnp.arange(half, dtype=F32) / half)
    fr = jnp.concatenate([freqs, freqs, jnp.zeros((LANES - MLA_ROPE,), F32)]).reshape(1, LANES)
    scale = float(1.0 / np.sqrt(MLA_QK).astype(np.float32)) * float(np.log2(np.e))
    full = lambda shape: pl.BlockSpec(shape, lambda i: (0, 0))
    tok = lambda w: pl.BlockSpec((tm, w), lambda i: (i, 0))
    return pl.pallas_call(
        functools.partial(_mla_prep_kernel, scale=scale),
        grid=(n // tm,),
        in_specs=[pl.BlockSpec((tm, pw), lambda i: (i, lat_blk)), pl.BlockSpec((tm, LANES), lambda i: (i, kr_blk)),
                  tok(1), full((1, LANES)), full((1, MLA_Q_LORA)), full((1, MLA_KV_LORA)),
                  full(wq.shape), full(wk.shape), full(wvt.shape)],
        out_specs=[tok(hw), tok(MLA_HEADS * LANES), tok(hw),
                   pl.BlockSpec((MLA_HEADS * MLA_V, tm), lambda i: (0, i)), tok(LANES)],
        out_shape=[jax.ShapeDtypeStruct((n, hw), BF16),
                   jax.ShapeDtypeStruct((n, MLA_HEADS * LANES), BF16),
                   jax.ShapeDtypeStruct((n, hw), BF16),
                   jax.ShapeDtypeStruct((MLA_HEADS * MLA_V, n), BF16),
                   jax.ShapeDtypeStruct((n, LANES), BF16)],
        compiler_params=_params("parallel"),
        name="mla_prep",
    )(proj, proj, positions.reshape(n, 1), fr, q_norm.reshape(1, -1), kv_norm.reshape(1, -1), wq, wk, wvt)


QSUB = 256


def _attn_kernel(qi_ref, kj_ref, qn_ref, qr_ref, kn_ref, kr_ref, vt_ref, o_ref, m_ref, l_ref, acc_ref, *, blk, hpb):
    s_id = pl.program_id(1)
    qi = qi_ref[s_id]
    kj = kj_ref[s_id]

    @pl.when(kj == 0)
    def _():
        m_ref[...] = jnp.full_like(m_ref, -jnp.inf)
        l_ref[...] = jnp.zeros_like(l_ref)
        acc_ref[...] = jnp.zeros_like(acc_ref)

    def update(masked):
        nsub = blk // QSUB
        items = [(hh, c) for hh in range(hpb) for c in range(nsub)]

        def nkeys(c):
            return (c + 1) * QSUB if masked else blk

        def scores(item):
            hh, c = item
            hl = slice(hh * LANES, (hh + 1) * LANES)
            sl = pl.ds(c * QSUB, QSUB)
            k = jnp.concatenate([kn_ref[0:nkeys(c), hl], kr_ref[0:nkeys(c), :]], axis=1)
            q = jnp.concatenate([qn_ref[sl, hl], qr_ref[sl, hl]], axis=1)
            return lax.dot_general(k, q, _NT, preferred_element_type=F32)

        sts = [scores(it) for it in items[:2]]
        for n, (hh, c) in enumerate(items):
            sl = pl.ds(c * QSUB, QSUB)
            st = sts[n]
            if n + 2 < len(items):
                sts.append(scores(items[n + 2]))
            if masked:
                key = lax.broadcasted_iota(jnp.int32, st.shape, 0)
                qry = lax.broadcasted_iota(jnp.int32, st.shape, 1) + c * QSUB
                st = jnp.where(key <= qry, st, MASK_VALUE)
            m_prev = m_ref[hh, :, sl]
            m_new = jnp.maximum(m_prev, jnp.max(st, axis=0, keepdims=True))
            alpha = jnp.exp2(m_prev - m_new)
            p = jnp.exp2(st - m_new)
            l_ref[hh, :, sl] = alpha * l_ref[hh, :, sl] + jnp.sum(p, axis=0, keepdims=True)
            pv = jnp.dot(vt_ref[hh * MLA_V:(hh + 1) * MLA_V, 0:nkeys(c)], p.astype(BF16),
                         preferred_element_type=F32)
            acc_ref[hh, :, sl] = alpha * acc_ref[hh, :, sl] + pv
            m_ref[hh, :, sl] = m_new

    @pl.when(kj < qi)
    def _():
        update(False)

    @pl.when(kj == qi)
    def _():
        update(True)
        for hh in range(hpb):
            o_ref[:, hh * MLA_V:(hh + 1) * MLA_V] = (acc_ref[hh] / l_ref[hh]).T.astype(o_ref.dtype)


def _attention(qn, qr, kn, kr, vt, blk=1024, hpb=8):
    n = qn.shape[0]
    nb = n // blk
    qi = np.array([i for i in range(nb) for j in range(i + 1)], np.int32)
    kj = np.array([j for i in range(nb) for j in range(i + 1)], np.int32)
    qspec = pl.BlockSpec((blk, hpb * LANES), lambda h, s, qi, kj: (qi[s], h))
    kspec = pl.BlockSpec((blk, hpb * LANES), lambda h, s, qi, kj: (kj[s], h))
    krspec = pl.BlockSpec((blk, LANES), lambda h, s, qi, kj: (kj[s], 0))
    vtspec = pl.BlockSpec((hpb * MLA_V, blk), lambda h, s, qi, kj: (h, kj[s]))
    grid_spec = pltpu.PrefetchScalarGridSpec(
        num_scalar_prefetch=2,
        grid=(MLA_HEADS // hpb, len(qi)),
        in_specs=[qspec, qspec, kspec, krspec, vtspec],
        out_specs=qspec,
        scratch_shapes=[pltpu.VMEM((hpb, 1, blk), F32), pltpu.VMEM((hpb, 1, blk), F32),
                        pltpu.VMEM((hpb, MLA_V, blk), F32)],
    )
    return pl.pallas_call(
        functools.partial(_attn_kernel, blk=blk, hpb=hpb),
        grid_spec=grid_spec,
        out_shape=jax.ShapeDtypeStruct((n, MLA_HEADS * MLA_V), BF16),
        compiler_params=_params("parallel", "arbitrary"),
        name="mla_attention",
    )(jnp.asarray(qi), jnp.asarray(kj), qn, qr, kn, kr, vt)


def _merge_kernel(oa_ref, ob_ref, g0_ref, g1_ref, g2_ref, g3_ref, gt_ref, x_ref, wa_ref, wb_ref, wo_ref,
                  g_ref, x1_ref, hn_ref, *, off):
    d = x_ref.shape[1]
    ya = jnp.dot(oa_ref[...], wa_ref[...], preferred_element_type=F32)
    yb = jnp.dot(ob_ref[...], wb_ref[...], preferred_element_type=F32)
    gates = jnp.concatenate([g0_ref[...], g1_ref[...], g2_ref[...], g3_ref[...], gt_ref[...]], axis=1)
    ga = gates[:, off:off + d]
    gb = gates[:, off + d:off + 2 * d]
    y = jax.nn.sigmoid(ga) * ya + jax.nn.sigmoid(gb) * yb
    x1 = x_ref[...] + jnp.dot(y.astype(BF16), wo_ref[...], preferred_element_type=F32)
    x1_ref[...] = x1
    hn_ref[...] = _rms(x1, g_ref[...]).astype(hn_ref.dtype)


def _merge(oa, ob, proj, tail, g0, x, wa, wb, wo, gain, tm=256):
    n, d = x.shape
    gw = d // 2
    base = (g0 // gw) * gw
    assert base + 4 * gw == proj.shape[1] and g0 + 2 * d <= base + 4 * gw + LANES
    tok = lambda w, c=0: pl.BlockSpec((tm, w), lambda i, c=c: (i, c))
    res = lambda shape: pl.BlockSpec(shape, lambda i: (0, 0), pipeline_mode=pl.Buffered(1))
    return pl.pallas_call(
        functools.partial(_merge_kernel, off=g0 - base),
        grid=(n // tm,),
        in_specs=[tok(oa.shape[1]), tok(ob.shape[1])] + [tok(gw, base // gw + c) for c in range(4)]
                 + [tok(LANES), tok(d), res(wa.shape), res(wb.shape), res(wo.shape), res((1, d))],
        out_specs=[tok(d), tok(d)],
        out_shape=[jax.ShapeDtypeStruct((n, d), F32), jax.ShapeDtypeStruct((n, d), BF16)],
        compiler_params=_params("parallel"),
        name="merge_outproj",
    )(oa, ob, proj, proj, proj, proj, tail, x, wa, wb, wo, gain.reshape(1, d))


R_TM = 1024
R_BLOCKS = R_TM // LANES
R_PITCH = PEER_NKEYS + 8


def _first_max(vals, *payloads):
    payloads = [list(p) for p in payloads]
    while len(vals) > 1:
        nv, npl = [], [[] for _ in payloads]
        for i in range(0, len(vals) - 1, 2):
            take = vals[i] >= vals[i + 1]
            nv.append(jnp.maximum(vals[i], vals[i + 1]))
            for dst, p in zip(npl, payloads):
                dst.append(jnp.where(take, p[i], p[i + 1]))
        if len(vals) % 2:
            nv.append(vals[-1])
            for dst, p in zip(npl, payloads):
                dst.append(p[-1])
        vals, payloads = nv, npl
    return (vals[0], *[p[0] for p in payloads])


TOP_GROUP = 8


def _sort_group(vals, ids):
    vals, ids = list(vals), list(ids)
    n = len(vals)
    for rnd in range(n):
        for p in range(rnd % 2, n - 1, 2):
            take = vals[p] >= vals[p + 1]
            hi, lo = jnp.maximum(vals[p], vals[p + 1]), jnp.minimum(vals[p], vals[p + 1])
            ids[p], ids[p + 1] = jnp.where(take, ids[p], ids[p + 1]), jnp.where(take, ids[p + 1], ids[p])
            vals[p], vals[p + 1] = hi, lo
    return vals, ids


def _pop_rounds(lists, payloads, k):
    lists = [list(g) for g in lists]
    payloads = [[list(g) for g in p] for p in payloads]
    gids = [float(g) for g in range(len(lists))]
    vals, outs = [], [[] for _ in payloads]
    for j in range(k):
        m, win, *heads = _first_max([g[0] for g in lists], gids, *[[g[0] for g in p] for p in payloads])
        vals.append(m)
        for o, hd in zip(outs, heads):
            o.append(hd)
        for g, lst in enumerate(lists):
            depth = min(len(lst) - 1, k - 1 - j)
            pop = win == gids[g]
            for r in range(depth):
                lst[r] = jnp.where(pop, lst[r + 1], lst[r])
                for p in payloads:
                    p[g][r] = jnp.where(pop, p[g][r + 1], p[g][r])
            if depth == len(lst) - 1:
                lst[-1] = jnp.where(pop, -jnp.inf, lst[-1])
    return vals, outs


def _top_keys(x, k):
    gs = TOP_GROUP
    lv, li = [], []
    for g in range(len(x) // gs):
        v, i = _sort_group(x[g * gs:(g + 1) * gs], [float(j) for j in range(g * gs, (g + 1) * gs)])
        lv.append(v)
        li.append(i)
    vals, (idxs,) = _pop_rounds(lv, [li], k)
    return vals, idxs


def _top_pairs(v1, i1, v2, i2, k):
    ranks = [[b for b in range(k) if (a + 1) * (b + 1) <= k] for a in range(k)]
    sums = [[v1[a] + v2[b] for b in bs] for a, bs in enumerate(ranks)]
    ids = [[i1[a] * float(PEER_NKEYS) + i2[b] for b in bs] for a, bs in enumerate(ranks)]
    vals, (sel,) = _pop_rounds(sums, [ids], k)
    return vals, sel


def _split_bf16(x):
    hi = x.astype(BF16)
    return hi, (x - hi.astype(F32)).astype(BF16)


def _route_kernel(h_ref, wq_ref, k1_ref, k2_ref, e_ref, g_ref, s1_ref, s2_ref, *, hps):
    k = PEER_TOPK
    half = PEER_QDIM // 2
    h = h_ref[...]

    def scores(hh):
        qt = lax.dot_general(wq_ref[hh * PEER_QDIM:(hh + 1) * PEER_QDIM, :], h, _NT, preferred_element_type=F32)
        out = []
        for part, k_ref in ((0, k1_ref), (1, k2_ref)):
            q_hi, q_lo = _split_bf16(qt[part * half:(part + 1) * half])
            qs = jnp.concatenate([q_hi, q_lo, q_hi], axis=0)
            out.append(jnp.dot(k_ref[...], qs, preferred_element_type=F32))
        return out

    def by_key(s, s_ref):
        for j in range(R_BLOCKS):
            s_ref[j * R_PITCH:j * R_PITCH + PEER_NKEYS, :] = s[:, j * LANES:(j + 1) * LANES]
        return [s_ref[pl.ds(key, R_BLOCKS, stride=R_PITCH), :] for key in range(PEER_NKEYS)]

    nxt = scores(0)
    for hh in range(hps):
        s1, s2 = nxt
        if hh + 1 < hps:
            nxt = scores(hh + 1)
        v1, i1 = _top_keys(by_key(s1, s1_ref), k)
        v2, i2 = _top_keys(by_key(s2, s2_ref), k)
        cv, ce = _top_pairs(v1, i1, v2, i2, k)
        ex = [jnp.exp(c - cv[0]) for c in cv]
        den = ex[0]
        for x in ex[1:]:
            den = den + x
        for j in range(k):
            g_ref[0, hh * k + j] = ex[j] / den
            e_ref[0, hh * k + j] = ce[j].astype(jnp.int32)


def _route(hn, wq_t, k1, k2, hps=2):
    n, d = hn.shape
    k = PEER_TOPK
    tiles = n // R_TM

    def stack(kk):
        hi, lo = _split_bf16(kk)
        return jnp.concatenate([hi, hi, lo], axis=1)

    full = lambda shape: pl.BlockSpec(shape, lambda i, g: (0, 0))
    out = pl.BlockSpec((1, hps * k, R_BLOCKS, LANES), lambda i, g: (i, g, 0, 0))
    sc = lambda: pltpu.VMEM((R_BLOCKS * R_PITCH, LANES), F32)
    return pl.pallas_call(
        functools.partial(_route_kernel, hps=hps),
        grid=(tiles, PEER_HEADS // hps),
        in_specs=[pl.BlockSpec((R_TM, d), lambda i, g: (i, 0)),
                  pl.BlockSpec((hps * PEER_QDIM, d), lambda i, g: (g, 0)),
                  full((PEER_NKEYS, 3 * (PEER_QDIM // 2))), full((PEER_NKEYS, 3 * (PEER_QDIM // 2)))],
        out_specs=[out, out],
        out_shape=[jax.ShapeDtypeStruct((tiles, PEER_HEADS * k, R_BLOCKS, LANES), jnp.int32),
                   jax.ShapeDtypeStruct((tiles, PEER_HEADS * k, R_BLOCKS, LANES), F32)],
        scratch_shapes=[sc(), sc()],
        compiler_params=_params("parallel", "arbitrary"),
        name="peer_route",
    )(hn, wq_t, stack(k1), stack(k2))


W_TILE = 128
W_PITCH = W_TILE + 8


def _wbuild_kernel(e_ref, g_ref, w_ref):
    nk = PEER_NKEYS
    iota = lax.broadcasted_iota(jnp.int32, (nk, LANES), 0)
    for i in range(nk):
        w_ref[i * W_PITCH + W_TILE:(i + 1) * W_PITCH, :] = jnp.zeros((W_PITCH - W_TILE, nk), F32)

    def body(t, carry):
        e = e_ref[pl.ds(t, 1), :]
        g = g_ref[pl.ds(t, 1), :]
        p1 = jnp.where(iota == (e >> 7), g, 0.0).astype(BF16)
        p2 = jnp.where(iota == (e & (nk - 1)), 1.0, 0.0).astype(BF16)
        wt = lax.dot_general(p1, p2, _NT, preferred_element_type=F32)
        w_ref[pl.ds(t, nk, stride=W_PITCH), :] = wt
        return carry

    lax.fori_loop(0, W_TILE, body, 0, unroll=32)


def _wbuild(e, g):
    n, pairs = e.shape
    nk = PEER_NKEYS
    tiles = n // W_TILE
    w = pl.pallas_call(
        _wbuild_kernel,
        grid=(tiles,),
        in_specs=[pl.BlockSpec((W_TILE, pairs), lambda i: (i, 0)), pl.BlockSpec((W_TILE, pairs), lambda i: (i, 0))],
        out_specs=pl.BlockSpec((nk * W_PITCH, nk), lambda i: (i, 0)),
        out_shape=jax.ShapeDtypeStruct((tiles * nk * W_PITCH, nk), F32),
        compiler_params=_params("parallel"),
        name="peer_wbuild",
    )(e, g)
    return w.reshape(tiles, nk, W_PITCH, nk)


def _peer_kernel(h_ref, u_ref, v_ref, w_ref, o_ref, *, groups, tiles):
    @pl.when(pl.program_id(1) == 0)
    def _():
        o_ref[...] = jnp.zeros_like(o_ref)

    z = lax.dot_general(h_ref[...], u_ref[...].astype(BF16), _NT, preferred_element_type=F32)
    rows = []
    for k in range(tiles):
        cols = []
        for i in range(groups):
            zi = z[k * W_TILE:(k + 1) * W_TILE, i * LANES:(i + 1) * LANES]
            gelu = 0.5 * zi * (1.0 + lax.erf(zi * np.float32(np.sqrt(0.5))))
            cols.append((w_ref[k, i, 0:W_TILE, :] * gelu).astype(BF16))
        rows.append(jnp.concatenate(cols, axis=1))
    a = jnp.concatenate(rows, axis=0)
    o_ref[...] += jnp.dot(a, v_ref[...].astype(BF16), preferred_element_type=F32)


def _peer(hn, u, v, w, tm=1024, te=512):
    n, d = hn.shape
    ne = u.shape[0]
    groups = te // PEER_NKEYS
    tiles = tm // W_TILE
    return pl.pallas_call(
        functools.partial(_peer_kernel, groups=groups, tiles=tiles),
        grid=(n // tm, ne // te),
        in_specs=[pl.BlockSpec((tm, d), lambda i, c: (i, 0)),
                  pl.BlockSpec((te, d), lambda i, c: (c, 0)),
                  pl.BlockSpec((te, d), lambda i, c: (c, 0)),
                  pl.BlockSpec((tiles, groups, W_PITCH, PEER_NKEYS), lambda i, c: (i, c, 0, 0))],
        out_specs=pl.BlockSpec((tm, d), lambda i, c: (i, 0)),
        out_shape=jax.ShapeDtypeStruct((n, d), F32),
        compiler_params=_params("parallel", "arbitrary"),
        name="peer_experts",
    )(hn, u, v, w)


def _ple_kernel(x1_ref, po_ref, p_ref, wpg_ref, wpe_ref, gp_ref, gf_ref, o_ref):
    x2 = x1_ref[...] + po_ref[...]
    hp = _rms(x2, gp_ref[...]).astype(BF16)
    gate = jax.nn.sigmoid(jnp.dot(hp, wpg_ref[...], preferred_element_type=F32))
    pe = jnp.dot(p_ref[...], wpe_ref[...], preferred_element_type=F32)
    x3 = x2 + gate * pe
    o_ref[...] = _rms(x3, gf_ref[...]).astype(o_ref.dtype)


def _ple(x1, po, p, wpg, wpe, g_ple, g_final, tm=512):
    n, d = x1.shape
    tok = lambda w: pl.BlockSpec((tm, w), lambda i: (i, 0))
    res = lambda shape: pl.BlockSpec(shape, lambda i: (0, 0), pipeline_mode=pl.Buffered(1))
    return pl.pallas_call(
        _ple_kernel,
        grid=(n // tm,),
        in_specs=[tok(d), tok(d), tok(p.shape[1]), res(wpg.shape), res(wpe.shape), res((1, d)), res((1, d))],
        out_specs=tok(d),
        out_shape=jax.ShapeDtypeStruct((n, d), F32),
        compiler_params=_params("parallel"),
        name="ple_final",
    )(x1, po, p, wpg, wpe, g_ple.reshape(1, d), g_final.reshape(1, d))


def _layer(x, p, positions, layer, norm_mix, w_in, lb_logits, hg_norm, mla_q_norm, mla_kv_norm,
           w_uq, w_ukv, w_a, w_b, w_o, norm_ffn, peer_wq, peer_k1, peer_k2, peer_u, peer_v,
           norm_ple, w_pg, w_pe, norm_out):
    n, d = x.shape
    hgw = 4 * HG_WIDTH
    mla0 = hgw
    kr0 = mla0 + MLA_Q_LORA + MLA_KV_LORA
    g0 = kr0 + MLA_ROPE
    half = MLA_ROPE // 2

    uq = w_uq.reshape(MLA_Q_LORA, MLA_HEADS, MLA_QK)
    uq_r = uq[:, :, MLA_NOPE:]
    uq_rot = jnp.concatenate([-uq_r[:, :, half:], uq_r[:, :, :half]], axis=2)
    wq = jnp.concatenate([uq[:, :, :MLA_NOPE].reshape(MLA_Q_LORA, -1),
                          jnp.concatenate([uq_r, uq_rot], axis=2).reshape(MLA_Q_LORA, -1)],
                         axis=1).astype(BF16)
    ukv = w_ukv.reshape(MLA_KV_LORA, MLA_HEADS, MLA_NOPE + MLA_V)
    wk = ukv[:, :, :MLA_NOPE].reshape(MLA_KV_LORA, -1).astype(BF16)
    wvt = ukv[:, :, MLA_NOPE:].reshape(MLA_KV_LORA, -1).T.astype(BF16)

    tn = 1024
    w_in_t = w_in.T
    main_cols = (w_in_t.shape[0] // tn) * tn
    w_tail_t = w_in_t[main_cols:]
    w_tail_t = jnp.pad(w_tail_t, ((0, LANES - w_tail_t.shape[0]), (0, 0))).astype(BF16)
    hn, tail = _rmsnorm(x, norm_mix, w_tail_t)
    proj = _matmul_ws(hn, w_in_t, main_cols, F32, 1024, tn, "in_proj_main")

    o_a = _hgrn(proj, lb_logits, hg_norm, layer)
    qn, qr, kn, vt, kr = _mla_prep(proj, positions, mla_q_norm, mla_kv_norm, wq, wk, wvt)
    o_b = _attention(qn, qr, kn, kr, vt)
    x1, hn2 = _merge(o_a, o_b, proj, tail, g0, x, w_a.astype(BF16), w_b.astype(BF16), w_o.astype(BF16),
                     norm_ffn)

    e4, g4 = _route(hn2, peer_wq.T.astype(BF16), peer_k1, peer_k2)
    by_token = lambda a: a.transpose(0, 2, 3, 1).reshape(n, a.shape[1])
    w = _wbuild(by_token(e4), by_token(g4))
    po = _peer(hn2, peer_u, peer_v, w)
    return _ple(x1, po, p.astype(BF16), w_pg.astype(BF16), w_pe.astype(BF16), norm_ple, norm_out)


def kernel(x, p, positions, norm_mix, w_in, lb_logits, hg_norm, mla_q_norm, mla_kv_norm, w_uq, w_ukv,
           w_a, w_b, w_o, norm_ffn, peer_wq, peer_k1, peer_k2, peer_u, peer_v, norm_ple, w_pg, w_pe,
           norm_final):
    b, t, d = x.shape
    depth = w_in.shape[0]
    assert b == 1 and depth == 1, "kernel supports one sequence and one layer"
    out = _layer(x.reshape(b * t, d), p[0].reshape(b * t, -1), positions.reshape(-1), 0,
                 norm_mix[0], w_in[0], lb_logits, hg_norm[0], mla_q_norm[0], mla_kv_norm[0],
                 w_uq[0], w_ukv[0], w_a[0], w_b[0], w_o[0], norm_ffn[0], peer_wq[0], peer_k1[0],
                 peer_k2[0], peer_u[0], peer_v[0], norm_ple[0], w_pg[0], w_pe[0], norm_final)
    return out.reshape(b, t, d)
```

```python
import functools

import jax
import jax.numpy as jnp
import numpy as np
from jax import lax
from jax.experimental import pallas as pl
from jax.experimental.pallas import tpu as pltpu

D_MODEL = 2048
HG_HEADS = 8
HG_DK = 128
HG_DV = 128
HG_WIDTH = HG_HEADS * HG_DK
HG_CHUNK = 64
MLA_HEADS = 8
MLA_NOPE = 128
MLA_ROPE = 64
MLA_V = 128
MLA_Q_LORA = 512
MLA_KV_LORA = 512
MLA_QK = MLA_NOPE + MLA_ROPE
ROPE_THETA = 10000.0
PEER_HEADS = 8
PEER_NKEYS = 128
PEER_QDIM = 256
PEER_TOPK = 16
NORM_EPS = 1e-6
MASK_VALUE = -1e30

LANES = 128
VMEM_LIMIT = 56 * 1024 * 1024

_NT = (((1,), (1,)), ((), ()))
_TN = (((0,), (0,)), ((), ()))

F32 = jnp.float32
BF16 = jnp.bfloat16


def _params(*sem):
    return pltpu.CompilerParams(dimension_semantics=sem, vmem_limit_bytes=VMEM_LIMIT)


def _rms(x, gain):
    return x * lax.rsqrt(jnp.mean(x * x, axis=-1, keepdims=True) + NORM_EPS) * gain


def _rmsnorm_kernel(x_ref, g_ref, wt_ref, o_ref, t_ref):
    hn = _rms(x_ref[...], g_ref[...]).astype(o_ref.dtype)
    o_ref[...] = hn
    t_ref[...] = lax.dot_general(hn, wt_ref[...], _NT, preferred_element_type=F32)


def _rmsnorm(x, gain, w_tail_t, tm=512):
    n, d = x.shape
    tw = w_tail_t.shape[0]
    return pl.pallas_call(
        _rmsnorm_kernel,
        grid=(n // tm,),
        in_specs=[pl.BlockSpec((tm, d), lambda i: (i, 0)), pl.BlockSpec((1, d), lambda i: (0, 0)),
                  pl.BlockSpec((tw, d), lambda i: (0, 0))],
        out_specs=[pl.BlockSpec((tm, d), lambda i: (i, 0)), pl.BlockSpec((tm, tw), lambda i: (i, 0))],
        out_shape=[jax.ShapeDtypeStruct((n, d), BF16), jax.ShapeDtypeStruct((n, tw), F32)],
        compiler_params=_params("parallel"),
        name="rmsnorm",
    )(x, gain.reshape(1, d), w_tail_t)


def _matmul_ws_kernel(a_ref, w_ref, o_ref, wb_ref):
    @pl.when(pl.program_id(1) == 0)
    def _():
        wb_ref[...] = w_ref[...].astype(BF16)

    o_ref[...] = lax.dot_general(a_ref[...], wb_ref[...], _NT, preferred_element_type=F32).astype(o_ref.dtype)


def _matmul_ws(a, w_t, cols, out_dtype, tm, tn, name):
    m, k = a.shape
    return pl.pallas_call(
        _matmul_ws_kernel,
        grid=(cols // tn, m // tm),
        in_specs=[pl.BlockSpec((tm, k), lambda j, i: (i, 0)), pl.BlockSpec((tn, k), lambda j, i: (j, 0))],
        out_specs=pl.BlockSpec((tm, tn), lambda j, i: (i, j)),
        out_shape=jax.ShapeDtypeStruct((m, cols), out_dtype),
        scratch_shapes=[pltpu.VMEM((tn, k), BF16)],
        compiler_params=_params("arbitrary", "arbitrary"),
        name=name,
    )(a, w_t)


HG_GROUP = 4


def _hgrn_kernel(q_ref, f_ref, i_ref, og_ref, lbl_ref, gain_ref, tri_ref, o_ref, st_ref, *, layer, chunks):
    @pl.when(pl.program_id(1) == 0)
    def _():
        st_ref[...] = jnp.zeros_like(st_ref)

    lbl = lbl_ref[...]
    e = jnp.exp(lbl - jnp.max(lbl, axis=0, keepdims=True))
    sm = e / jnp.sum(e, axis=0, keepdims=True)
    lb = jnp.sum(sm[: layer + 1], axis=0, keepdims=True)
    c = HG_CHUNK
    row = lax.broadcasted_iota(jnp.int32, (c, c), 0)
    col = lax.broadcasted_iota(jnp.int32, (c, c), 1)
    causal = row >= col

    q = q_ref[...]
    vb = i_ref[...].astype(BF16)
    f = lb + (1.0 - lb) * jax.nn.sigmoid(f_ref[...])
    g = jnp.log(f)
    k = 1.0 - f
    g_hi = g.astype(BF16)
    r1 = g - g_hi.astype(F32)
    g_mid = r1.astype(BF16)
    g_lo = (r1 - g_mid.astype(F32)).astype(BF16)
    gs = jnp.concatenate([g_hi, g_mid, g_lo], axis=1)
    gr = HG_GROUP * c
    parts = []
    for n in range(chunks // HG_GROUP):
        b3 = jnp.dot(tri_ref[...], gs[n * gr:(n + 1) * gr, :], preferred_element_type=F32)
        parts.append(b3[:, :HG_DK] + b3[:, HG_DK:2 * HG_DK] + b3[:, 2 * HG_DK:])
    b = jnp.concatenate(parts, axis=0)

    def rows_of(r):
        return jnp.concatenate([jnp.broadcast_to(b[n * c + r:n * c + r + 1, :], (c, HG_DK))
                                for n in range(chunks)], axis=0)

    b_mid = rows_of(c // 2 - 1)
    b_last = rows_of(c - 1)
    q_in = (q * jnp.exp(b - b_mid)).astype(BF16)
    k_in = (k * jnp.exp(b_mid - b)).astype(BF16)
    k_dec = (k * jnp.exp(b_last - b)).astype(BF16)
    qe = (q * jnp.exp(b)).astype(BF16)

    o_intra, dst = [], []
    for n in range(chunks):
        sl = slice(n * c, (n + 1) * c)
        a = lax.dot_general(q_in[sl], k_in[sl], _NT, preferred_element_type=F32)
        a = jnp.where(causal, a, 0.0).astype(BF16)
        o_intra.append(jnp.dot(a, vb[sl], preferred_element_type=F32))
        dst.append(lax.dot_general(vb[sl], k_dec[sl], _TN, preferred_element_type=F32))

    st = st_ref[...]
    outs = []
    for n in range(chunks):
        sl = slice(n * c, (n + 1) * c)
        outs.append(o_intra[n] + lax.dot_general(qe[sl], st.astype(BF16), _NT, preferred_element_type=F32))
        st = jnp.exp(b[(n + 1) * c - 1:(n + 1) * c, :]) * st + dst[n]
    st_ref[...] = st

    o = jnp.concatenate(outs, axis=0)
    og = og_ref[...]
    o = o * lax.rsqrt(jnp.mean(o * o, axis=-1, keepdims=True) + NORM_EPS) * gain_ref[...]
    o_ref[...] = (o * (og * jax.nn.sigmoid(og))).astype(o_ref.dtype)


def _hgrn(proj, lb_logits, gain, layer, tt=2048):
    n = proj.shape[0]
    h = HG_HEADS
    nl = lb_logits.shape[0]
    gr = HG_GROUP * HG_CHUNK
    idx = np.arange(gr)
    tri = ((idx[:, None] >= idx[None, :]) & (idx[:, None] // HG_CHUNK == idx[None, :] // HG_CHUNK))
    blk = lambda g: pl.BlockSpec((tt, HG_DK), lambda hh, t, g=g: (t, g * h + hh))
    return pl.pallas_call(
        functools.partial(_hgrn_kernel, layer=layer, chunks=tt // HG_CHUNK),
        grid=(h, n // tt),
        in_specs=[blk(0), blk(1), blk(2), blk(3),
                  pl.BlockSpec((nl, HG_DK), lambda hh, t: (0, hh)),
                  pl.BlockSpec((1, HG_DV), lambda hh, t: (0, hh)),
                  pl.BlockSpec((gr, gr), lambda hh, t: (0, 0))],
        out_specs=pl.BlockSpec((tt, HG_DV), lambda hh, t: (t, hh)),
        out_shape=jax.ShapeDtypeStruct((n, h * HG_DV), BF16),
        scratch_shapes=[pltpu.VMEM((HG_DV, HG_DK), F32)],
        compiler_params=_params("parallel", "arbitrary"),
        name="hgrn2",
    )(proj, proj, proj, proj, lb_logits, gain.reshape(1, -1), jnp.asarray(tri, BF16))


def _mla_prep_kernel(p_ref, pk_ref, pos_ref, fr_ref, qn_ref_g, kvn_ref_g, wq_ref, wk_ref, wvt_ref,
                     qn_ref, qr_ref, kn_ref, vt_ref, kr_ref, *, scale):
    hw = MLA_HEADS * MLA_NOPE
    half = MLA_ROPE // 2
    ang = pos_ref[...].astype(F32) * fr_ref[...]
    lane = lax.broadcasted_iota(jnp.int32, ang.shape, 1)
    cosv = jnp.where(lane < MLA_ROPE, jnp.cos(ang), 0.0)
    sinv = jnp.where(lane < MLA_ROPE, jnp.sin(ang), 0.0)

    def rope(y):
        return y * cosv + pltpu.roll(y, MLA_ROPE, 1) * sinv

    cq = _rms(p_ref[:, :MLA_Q_LORA], qn_ref_g[...]).astype(BF16)
    qres = jnp.dot(cq, wq_ref[...], preferred_element_type=F32)
    qn_ref[...] = (qres[:, :hw] * scale).astype(qn_ref.dtype)
    for h in range(MLA_HEADS):
        y = qres[:, hw + h * LANES: hw + (h + 1) * LANES]
        qr_ref[:, h * LANES:(h + 1) * LANES] = (rope(y) * scale).astype(qr_ref.dtype)
    ckv = _rms(p_ref[:, MLA_Q_LORA:MLA_Q_LORA + MLA_KV_LORA], kvn_ref_g[...]).astype(BF16)
    kn_ref[...] = jnp.dot(ckv, wk_ref[...], preferred_element_type=F32).astype(kn_ref.dtype)
    vt_ref[...] = lax.dot_general(wvt_ref[...], ckv, _NT, preferred_element_type=F32).astype(vt_ref.dtype)
    y = pk_ref[...]
    rot = jnp.where(lane < half, -pltpu.roll(y, LANES - half, 1), pltpu.roll(y, half, 1))
    kr_ref[...] = (y * cosv + rot * sinv).astype(kr_ref.dtype)


def _mla_prep(proj, positions, q_norm, kv_norm, wq, wk, wvt, tm=512):
    n = proj.shape[0]
    pw = MLA_Q_LORA + MLA_KV_LORA
    lat_blk = 4 * HG_WIDTH // pw
    kr_blk = (4 * HG_WIDTH + pw) // LANES
    hw = MLA_HEADS * MLA_NOPE
    half = MLA_ROPE // 2
    freqs = ROPE_THETA ** (-jnp.arange(half, dtype=F32) / half)
    fr = jnp.concatenate([freqs, freqs, jnp.zeros((LANES - MLA_ROPE,), F32)]).reshape(1, LANES)
    scale = float(1.0 / np.sqrt(MLA_QK).astype(np.float32)) * float(np.log2(np.e))
    full = lambda shape: pl.BlockSpec(shape, lambda i: (0, 0))
    tok = lambda w: pl.BlockSpec((tm, w), lambda i: (i, 0))
    return pl.pallas_call(
        functools.partial(_mla_prep_kernel, scale=scale),
        grid=(n // tm,),
        in_specs=[pl.BlockSpec((tm, pw), lambda i: (i, lat_blk)), pl.BlockSpec((tm, LANES), lambda i: (i, kr_blk)),
                  tok(1), full((1, LANES)), full((1, MLA_Q_LORA)), full((1, MLA_KV_LORA)),
                  full(wq.shape), full(wk.shape), full(wvt.shape)],
        out_specs=[tok(hw), tok(MLA_HEADS * LANES), tok(hw),
                   pl.BlockSpec((MLA_HEADS * MLA_V, tm), lambda i: (0, i)), tok(LANES)],
        out_shape=[jax.ShapeDtypeStruct((n, hw), BF16),
                   jax.ShapeDtypeStruct((n, MLA_HEADS * LANES), BF16),
                   jax.ShapeDtypeStruct((n, hw), BF16),
                   jax.ShapeDtypeStruct((MLA_HEADS * MLA_V, n), BF16),
                   jax.ShapeDtypeStruct((n, LANES), BF16)],
        compiler_params=_params("parallel"),
        name="mla_prep",
    )(proj, proj, positions.reshape(n, 1), fr, q_norm.reshape(1, -1), kv_norm.reshape(1, -1), wq, wk, wvt)


QSUB = 512


def _attn_kernel(qi_ref, kj_ref, qn_ref, qr_ref, kn_ref, kr_ref, vt_ref, o_ref, m_ref, l_ref, acc_ref, *, blk, hpb):
    s_id = pl.program_id(1)
    qi = qi_ref[s_id]
    kj = kj_ref[s_id]

    @pl.when(kj == 0)
    def _():
        m_ref[...] = jnp.full_like(m_ref, -jnp.inf)
        l_ref[...] = jnp.zeros_like(l_ref)
        acc_ref[...] = jnp.zeros_like(acc_ref)

    def update(masked):
        nsub = blk // QSUB
        items = [(hh, c) for hh in range(hpb) for c in range(nsub)]

        def nkeys(c):
            return (c + 1) * QSUB if masked else blk

        def scores(item):
            hh, c = item
            hl = slice(hh * LANES, (hh + 1) * LANES)
            sl = pl.ds(c * QSUB, QSUB)
            k = jnp.concatenate([kn_ref[0:nkeys(c), hl], kr_ref[0:nkeys(c), :]], axis=1)
            q = jnp.concatenate([qn_ref[sl, hl], qr_ref[sl, hl]], axis=1)
            return lax.dot_general(k, q, _NT, preferred_element_type=F32)

        sts = [scores(it) for it in items[:2]]
        for n, (hh, c) in enumerate(items):
            sl = pl.ds(c * QSUB, QSUB)
            st = sts[n]
            if n + 2 < len(items):
                sts.append(scores(items[n + 2]))
            if masked:
                key = lax.broadcasted_iota(jnp.int32, st.shape, 0)
                qry = lax.broadcasted_iota(jnp.int32, st.shape, 1) + c * QSUB
                st = jnp.where(key <= qry, st, MASK_VALUE)
            m_prev = m_ref[hh, :, sl]
            m_new = jnp.maximum(m_prev, jnp.max(st, axis=0, keepdims=True))
            alpha = jnp.exp2(m_prev - m_new)
            p = jnp.exp2(st - m_new)
            l_ref[hh, :, sl] = alpha * l_ref[hh, :, sl] + jnp.sum(p, axis=0, keepdims=True)
            pv = jnp.dot(vt_ref[hh * MLA_V:(hh + 1) * MLA_V, 0:nkeys(c)], p.astype(BF16),
                         preferred_element_type=F32)
            acc_ref[hh, :, sl] = alpha * acc_ref[hh, :, sl] + pv
            m_ref[hh, :, sl] = m_new

    @pl.when(kj < qi)
    def _():
        update(False)

    @pl.when(kj == qi)
    def _():
        update(True)
        for hh in range(hpb):
            o_ref[:, hh * MLA_V:(hh + 1) * MLA_V] = (acc_ref[hh] / l_ref[hh]).T.astype(o_ref.dtype)


def _attention(qn, qr, kn, kr, vt, blk=1024, hpb=8):
    n = qn.shape[0]
    nb = n // blk
    qi = np.array([i for i in range(nb) for j in range(i + 1)], np.int32)
    kj = np.array([j for i in range(nb) for j in range(i + 1)], np.int32)
    qspec = pl.BlockSpec((blk, hpb * LANES), lambda h, s, qi, kj: (qi[s], h))
    kspec = pl.BlockSpec((blk, hpb * LANES), lambda h, s, qi, kj: (kj[s], h))
    krspec = pl.BlockSpec((blk, LANES), lambda h, s, qi, kj: (kj[s], 0))
    vtspec = pl.BlockSpec((hpb * MLA_V, blk), lambda h, s, qi, kj: (h, kj[s]))
    grid_spec = pltpu.PrefetchScalarGridSpec(
        num_scalar_prefetch=2,
        grid=(MLA_HEADS // hpb, len(qi)),
        in_specs=[qspec, qspec, kspec, krspec, vtspec],
        out_specs=qspec,
        scratch_shapes=[pltpu.VMEM((hpb, 1, blk), F32), pltpu.VMEM((hpb, 1, blk), F32),
                        pltpu.VMEM((hpb, MLA_V, blk), F32)],
    )
    return pl.pallas_call(
        functools.partial(_attn_kernel, blk=blk, hpb=hpb),
        grid_spec=grid_spec,
        out_shape=jax.ShapeDtypeStruct((n, MLA_HEADS * MLA_V), BF16),
        compiler_params=_params("parallel", "arbitrary"),
        name="mla_attention",
    )(jnp.asarray(qi), jnp.asarray(kj), qn, qr, kn, kr, vt)


def _merge_kernel(oa_ref, ob_ref, g0_ref, g1_ref, g2_ref, g3_ref, gt_ref, x_ref, wa_ref, wb_ref, wo_ref,
                  g_ref, x1_ref, hn_ref, *, off):
    d = x_ref.shape[1]
    ya = jnp.dot(oa_ref[...], wa_ref[...], preferred_element_type=F32)
    yb = jnp.dot(ob_ref[...], wb_ref[...], preferred_element_type=F32)
    gates = jnp.concatenate([g0_ref[...], g1_ref[...], g2_ref[...], g3_ref[...], gt_ref[...]], axis=1)
    ga = gates[:, off:off + d]
    gb = gates[:, off + d:off + 2 * d]
    y = jax.nn.sigmoid(ga) * ya + jax.nn.sigmoid(gb) * yb
    x1 = x_ref[...] + jnp.dot(y.astype(BF16), wo_ref[...], preferred_element_type=F32)
    x1_ref[...] = x1
    hn_ref[...] = _rms(x1, g_ref[...]).astype(hn_ref.dtype)


def _merge(oa, ob, proj, tail, g0, x, wa, wb, wo, gain, tm=256):
    n, d = x.shape
    gw = d // 2
    base = (g0 // gw) * gw
    assert base + 4 * gw == proj.shape[1] and g0 + 2 * d <= base + 4 * gw + LANES
    tok = lambda w, c=0: pl.BlockSpec((tm, w), lambda i, c=c: (i, c))
    res = lambda shape: pl.BlockSpec(shape, lambda i: (0, 0), pipeline_mode=pl.Buffered(1))
    return pl.pallas_call(
        functools.partial(_merge_kernel, off=g0 - base),
        grid=(n // tm,),
        in_specs=[tok(oa.shape[1]), tok(ob.shape[1])] + [tok(gw, base // gw + c) for c in range(4)]
                 + [tok(LANES), tok(d), res(wa.shape), res(wb.shape), res(wo.shape), res((1, d))],
        out_specs=[tok(d), tok(d)],
        out_shape=[jax.ShapeDtypeStruct((n, d), F32), jax.ShapeDtypeStruct((n, d), BF16)],
        compiler_params=_params("parallel"),
        name="merge_outproj",
    )(oa, ob, proj, proj, proj, proj, tail, x, wa, wb, wo, gain.reshape(1, d))


R_TM = 1024
R_BLOCKS = R_TM // LANES
R_PITCH = PEER_NKEYS + 8


def _first_max(vals, *payloads):
    payloads = [list(p) for p in payloads]
    while len(vals) > 1:
        nv, npl = [], [[] for _ in payloads]
        for i in range(0, len(vals) - 1, 2):
            take = vals[i] >= vals[i + 1]
            nv.append(jnp.maximum(vals[i], vals[i + 1]))
            for dst, p in zip(npl, payloads):
                dst.append(jnp.where(take, p[i], p[i + 1]))
        if len(vals) % 2:
            nv.append(vals[-1])
            for dst, p in zip(npl, payloads):
                dst.append(p[-1])
        vals, payloads = nv, npl
    return (vals[0], *[p[0] for p in payloads])


TOP_GROUP = 8


def _sort_group(vals, ids):
    vals, ids = list(vals), list(ids)
    n = len(vals)
    for rnd in range(n):
        for p in range(rnd % 2, n - 1, 2):
            take = vals[p] >= vals[p + 1]
            hi, lo = jnp.maximum(vals[p], vals[p + 1]), jnp.minimum(vals[p], vals[p + 1])
            ids[p], ids[p + 1] = jnp.where(take, ids[p], ids[p + 1]), jnp.where(take, ids[p + 1], ids[p])
            vals[p], vals[p + 1] = hi, lo
    return vals, ids


def _pop_rounds(lists, payloads, k):
    lists = [list(g) for g in lists]
    payloads = [[list(g) for g in p] for p in payloads]
    gids = [float(g) for g in range(len(lists))]
    vals, outs = [], [[] for _ in payloads]
    for j in range(k):
        m, win, *heads = _first_max([g[0] for g in lists], gids, *[[g[0] for g in p] for p in payloads])
        vals.append(m)
        for o, hd in zip(outs, heads):
            o.append(hd)
        for g, lst in enumerate(lists):
            depth = min(len(lst) - 1, k - 1 - j)
            pop = win == gids[g]
            for r in range(depth):
                lst[r] = jnp.where(pop, lst[r + 1], lst[r])
                for p in payloads:
                    p[g][r] = jnp.where(pop, p[g][r + 1], p[g][r])
            if depth == len(lst) - 1:
                lst[-1] = jnp.where(pop, -jnp.inf, lst[-1])
    return vals, outs


def _top_keys(x, k):
    gs = TOP_GROUP
    lv, li = [], []
    for g in range(len(x) // gs):
        v, i = _sort_group(x[g * gs:(g + 1) * gs], [float(j) for j in range(g * gs, (g + 1) * gs)])
        lv.append(v)
        li.append(i)
    vals, (idxs,) = _pop_rounds(lv, [li], k)
    return vals, idxs


def _top_pairs(v1, i1, v2, i2, k):
    ranks = [[b for b in range(k) if (a + 1) * (b + 1) <= k] for a in range(k)]
    sums = [[v1[a] + v2[b] for b in bs] for a, bs in enumerate(ranks)]
    ids = [[i1[a] * float(PEER_NKEYS) + i2[b] for b in bs] for a, bs in enumerate(ranks)]
    vals, (sel,) = _pop_rounds(sums, [ids], k)
    return vals, sel


def _split_bf16(x):
    hi = x.astype(BF16)
    return hi, (x - hi.astype(F32)).astype(BF16)


def _route_kernel(h_ref, wq_ref, k1_ref, k2_ref, e_ref, g_ref, s1_ref, s2_ref, *, hps):
    k = PEER_TOPK
    half = PEER_QDIM // 2
    h = h_ref[...]

    def scores(hh):
        qt = lax.dot_general(wq_ref[hh * PEER_QDIM:(hh + 1) * PEER_QDIM, :], h, _NT, preferred_element_type=F32)
        out = []
        for part, k_ref in ((0, k1_ref), (1, k2_ref)):
            q_hi, q_lo = _split_bf16(qt[part * half:(part + 1) * half])
            qs = jnp.concatenate([q_hi, q_lo, q_hi], axis=0)
            out.append(jnp.dot(k_ref[...], qs, preferred_element_type=F32))
        return out

    def by_key(s, s_ref):
        for j in range(R_BLOCKS):
            s_ref[j * R_PITCH:j * R_PITCH + PEER_NKEYS, :] = s[:, j * LANES:(j + 1) * LANES]
        return [s_ref[pl.ds(key, R_BLOCKS, stride=R_PITCH), :] for key in range(PEER_NKEYS)]

    nxt = scores(0)
    for hh in range(hps):
        s1, s2 = nxt
        if hh + 1 < hps:
            nxt = scores(hh + 1)
        v1, i1 = _top_keys(by_key(s1, s1_ref), k)
        v2, i2 = _top_keys(by_key(s2, s2_ref), k)
        cv, ce = _top_pairs(v1, i1, v2, i2, k)
        ex = [jnp.exp(c - cv[0]) for c in cv]
        den = ex[0]
        for x in ex[1:]:
            den = den + x
        for j in range(k):
            g_ref[0, hh * k + j] = ex[j] / den
            e_ref[0, hh * k + j] = ce[j].astype(jnp.int32)


def _route(hn, wq_t, k1, k2, hps=2):
    n, d = hn.shape
    k = PEER_TOPK
    tiles = n // R_TM

    def stack(kk):
        hi, lo = _split_bf16(kk)
        return jnp.concatenate([hi, hi, lo], axis=1)

    full = lambda shape: pl.BlockSpec(shape, lambda i, g: (0, 0))
    out = pl.BlockSpec((1, hps * k, R_BLOCKS, LANES), lambda i, g: (i, g, 0, 0))
    sc = lambda: pltpu.VMEM((R_BLOCKS * R_PITCH, LANES), F32)
    return pl.pallas_call(
        functools.partial(_route_kernel, hps=hps),
        grid=(tiles, PEER_HEADS // hps),
        in_specs=[pl.BlockSpec((R_TM, d), lambda i, g: (i, 0)),
                  pl.BlockSpec((hps * PEER_QDIM, d), lambda i, g: (g, 0)),
                  full((PEER_NKEYS, 3 * (PEER_QDIM // 2))), full((PEER_NKEYS, 3 * (PEER_QDIM // 2)))],
        out_specs=[out, out],
        out_shape=[jax.ShapeDtypeStruct((tiles, PEER_HEADS * k, R_BLOCKS, LANES), jnp.int32),
                   jax.ShapeDtypeStruct((tiles, PEER_HEADS * k, R_BLOCKS, LANES), F32)],
        scratch_shapes=[sc(), sc()],
        compiler_params=_params("parallel", "arbitrary"),
        name="peer_route",
    )(hn, wq_t, stack(k1), stack(k2))


W_TILE = 128
W_PITCH = W_TILE + 8


def _wbuild_kernel(e_ref, g_ref, w_ref):
    nk = PEER_NKEYS
    iota = lax.broadcasted_iota(jnp.int32, (nk, LANES), 0)
    for i in range(nk):
        w_ref[i * W_PITCH + W_TILE:(i + 1) * W_PITCH, :] = jnp.zeros((W_PITCH - W_TILE, nk), F32)

    def body(t, carry):
        e = e_ref[pl.ds(t, 1), :]
        g = g_ref[pl.ds(t, 1), :]
        p1 = jnp.where(iota == (e >> 7), g, 0.0).astype(BF16)
        p2 = jnp.where(iota == (e & (nk - 1)), 1.0, 0.0).astype(BF16)
        wt = lax.dot_general(p1, p2, _NT, preferred_element_type=F32)
        w_ref[pl.ds(t, nk, stride=W_PITCH), :] = wt
        return carry

    lax.fori_loop(0, W_TILE, body, 0, unroll=32)


def _wbuild(e, g):
    n, pairs = e.shape
    nk = PEER_NKEYS
    tiles = n // W_TILE
    w = pl.pallas_call(
        _wbuild_kernel,
        grid=(tiles,),
        in_specs=[pl.BlockSpec((W_TILE, pairs), lambda i: (i, 0)), pl.BlockSpec((W_TILE, pairs), lambda i: (i, 0))],
        out_specs=pl.BlockSpec((nk * W_PITCH, nk), lambda i: (i, 0)),
        out_shape=jax.ShapeDtypeStruct((tiles * nk * W_PITCH, nk), F32),
        compiler_params=_params("parallel"),
        name="peer_wbuild",
    )(e, g)
    return w.reshape(tiles, nk, W_PITCH, nk)


def _peer_kernel(h_ref, u_ref, v_ref, w_ref, o_ref, *, groups, tiles):
    @pl.when(pl.program_id(1) == 0)
    def _():
        o_ref[...] = jnp.zeros_like(o_ref)

    z = lax.dot_general(h_ref[...], u_ref[...].astype(BF16), _NT, preferred_element_type=F32)
    rows = []
    for k in range(tiles):
        cols = []
        for i in range(groups):
            zi = z[k * W_TILE:(k + 1) * W_TILE, i * LANES:(i + 1) * LANES]
            gelu = 0.5 * zi * (1.0 + lax.erf(zi * np.float32(np.sqrt(0.5))))
            cols.append((w_ref[k, i, 0:W_TILE, :] * gelu).astype(BF16))
        rows.append(jnp.concatenate(cols, axis=1))
    a = jnp.concatenate(rows, axis=0)
    o_ref[...] += jnp.dot(a, v_ref[...].astype(BF16), preferred_element_type=F32)


def _peer(hn, u, v, w, tm=1024, te=512):
    n, d = hn.shape
    ne = u.shape[0]
    groups = te // PEER_NKEYS
    tiles = tm // W_TILE
    return pl.pallas_call(
        functools.partial(_peer_kernel, groups=groups, tiles=tiles),
        grid=(n // tm, ne // te),
        in_specs=[pl.BlockSpec((tm, d), lambda i, c: (i, 0)),
                  pl.BlockSpec((te, d), lambda i, c: (c, 0)),
                  pl.BlockSpec((te, d), lambda i, c: (c, 0)),
                  pl.BlockSpec((tiles, groups, W_PITCH, PEER_NKEYS), lambda i, c: (i, c, 0, 0))],
        out_specs=pl.BlockSpec((tm, d), lambda i, c: (i, 0)),
        out_shape=jax.ShapeDtypeStruct((n, d), F32),
        compiler_params=_params("parallel", "arbitrary"),
        name="peer_experts",
    )(hn, u, v, w)


def _ple_kernel(x1_ref, po_ref, p_ref, wpg_ref, wpe_ref, gp_ref, gf_ref, o_ref):
    x2 = x1_ref[...] + po_ref[...]
    hp = _rms(x2, gp_ref[...]).astype(BF16)
    gate = jax.nn.sigmoid(jnp.dot(hp, wpg_ref[...], preferred_element_type=F32))
    pe = jnp.dot(p_ref[...], wpe_ref[...], preferred_element_type=F32)
    x3 = x2 + gate * pe
    o_ref[...] = _rms(x3, gf_ref[...]).astype(o_ref.dtype)


def _ple(x1, po, p, wpg, wpe, g_ple, g_final, tm=512):
    n, d = x1.shape
    tok = lambda w: pl.BlockSpec((tm, w), lambda i: (i, 0))
    res = lambda shape: pl.BlockSpec(shape, lambda i: (0, 0), pipeline_mode=pl.Buffered(1))
    return pl.pallas_call(
        _ple_kernel,
        grid=(n // tm,),
        in_specs=[tok(d), tok(d), tok(p.shape[1]), res(wpg.shape), res(wpe.shape), res((1, d)), res((1, d))],
        out_specs=tok(d),
        out_shape=jax.ShapeDtypeStruct((n, d), F32),
        compiler_params=_params("parallel"),
        name="ple_final",
    )(x1, po, p, wpg, wpe, g_ple.reshape(1, d), g_final.reshape(1, d))


def _layer(x, p, positions, layer, norm_mix, w_in, lb_logits, hg_norm, mla_q_norm, mla_kv_norm,
           w_uq, w_ukv, w_a, w_b, w_o, norm_ffn, peer_wq, peer_k1, peer_k2, peer_u, peer_v,
           norm_ple, w_pg, w_pe, norm_out):
    n, d = x.shape
    hgw = 4 * HG_WIDTH
    mla0 = hgw
    kr0 = mla0 + MLA_Q_LORA + MLA_KV_LORA
    g0 = kr0 + MLA_ROPE
    half = MLA_ROPE // 2

    uq = w_uq.reshape(MLA_Q_LORA, MLA_HEADS, MLA_QK)
    uq_r = uq[:, :, MLA_NOPE:]
    uq_rot = jnp.concatenate([-uq_r[:, :, half:], uq_r[:, :, :half]], axis=2)
    wq = jnp.concatenate([uq[:, :, :MLA_NOPE].reshape(MLA_Q_LORA, -1),
                          jnp.concatenate([uq_r, uq_rot], axis=2).reshape(MLA_Q_LORA, -1)],
                         axis=1).astype(BF16)
    ukv = w_ukv.reshape(MLA_KV_LORA, MLA_HEADS, MLA_NOPE + MLA_V)
    wk = ukv[:, :, :MLA_NOPE].reshape(MLA_KV_LORA, -1).astype(BF16)
    wvt = ukv[:, :, MLA_NOPE:].reshape(MLA_KV_LORA, -1).T.astype(BF16)

    tn = 1024
    w_in_t = w_in.T
    main_cols = (w_in_t.shape[0] // tn) * tn
    w_tail_t = w_in_t[main_cols:]
    w_tail_t = jnp.pad(w_tail_t, ((0, LANES - w_tail_t.shape[0]), (0, 0))).astype(BF16)
    hn, tail = _rmsnorm(x, norm_mix, w_tail_t)
    proj = _matmul_ws(hn, w_in_t, main_cols, F32, 1024, tn, "in_proj_main")

    o_a = _hgrn(proj, lb_logits, hg_norm, layer)
    qn, qr, kn, vt, kr = _mla_prep(proj, positions, mla_q_norm, mla_kv_norm, wq, wk, wvt)
    o_b = _attention(qn, qr, kn, kr, vt)
    x1, hn2 = _merge(o_a, o_b, proj, tail, g0, x, w_a.astype(BF16), w_b.astype(BF16), w_o.astype(BF16),
                     norm_ffn)

    e4, g4 = _route(hn2, peer_wq.T.astype(BF16), peer_k1, peer_k2)
    by_token = lambda a: a.transpose(0, 2, 3, 1).reshape(n, a.shape[1])
    w = _wbuild(by_token(e4), by_token(g4))
    po = _peer(hn2, peer_u, peer_v, w)
    return _ple(x1, po, p.astype(BF16), w_pg.astype(BF16), w_pe.astype(BF16), norm_ple, norm_out)


def kernel(x, p, positions, norm_mix, w_in, lb_logits, hg_norm, mla_q_norm, mla_kv_norm, w_uq, w_ukv,
           w_a, w_b, w_o, norm_ffn, peer_wq, peer_k1, peer_k2, peer_u, peer_v, norm_ple, w_pg, w_pe,
           norm_final):
    b, t, d = x.shape
    depth = w_in.shape[0]
    assert b == 1 and depth == 1, "kernel supports one sequence and one layer"
    out = _layer(x.reshape(b * t, d), p[0].reshape(b * t, -1), positions.reshape(-1), 0,
                 norm_mix[0], w_in[0], lb_logits, hg_norm[0], mla_q_norm[0], mla_kv_norm[0],
                 w_uq[0], w_ukv[0], w_a[0], w_b[0], w_o[0], norm_ffn[0], peer_wq[0], peer_k1[0],
                 peer_k2[0], peer_u[0], peer_v[0], norm_ple[0], w_pg[0], w_pe[0], norm_final)
    return out.reshape(b, t, d)
```
